```python
import jax, jax.numpy as jnp
from jax import lax
import numpy as np

D_MODEL = 1024
BATCH = 8
SEQ = 2048
DEPTH = 1
DEC_BATCH = 128
DEC_SEQ = 4
PAST_LEN = 16384
PAGE_SIZE = 128

RET_HEADS = 4
RET_DK = D_MODEL // 8
RET_DV = 2 * RET_DK
RET_QK = RET_HEADS * RET_DK
RET_V = RET_HEADS * RET_DV
RET_CHUNK = 128
ROPE_BASE = 10000.0
SG_GROUPS = 4
SG_GW = D_MODEL // 8
SG_W = SG_GROUPS * SG_GW
SG_CHUNK = 128
X_HEADS = 4
X_DH = D_MODEL // 8
X_W = X_HEADS * X_DH
MEM_LEN = 256
N_BRANCH = 3
D_FF = 2816
CONV_W = 3
EPS = 1e-6
IN_WIDTHS = (RET_QK, RET_QK, RET_V, RET_V, SG_W, SG_W, X_W, N_BRANCH * D_MODEL)
IN_COLS = sum(IN_WIDTHS)

kernel_name = "retention_sgu_memxattn_convffn_step"

F32 = jnp.float32


def _rmsnorm(x, g):
    xf = x.astype(F32)
    y = xf * lax.rsqrt(jnp.mean(xf * xf, axis=-1, keepdims=True) + EPS)
    return (y * g.astype(F32)).astype(x.dtype)


def _stdnorm(xf):
    mu = jnp.mean(xf, axis=-1, keepdims=True)
    var = jnp.mean(jnp.square(xf - mu), axis=-1, keepdims=True)
    return (xf - mu) * lax.rsqrt(var + EPS)


def _rope(x, pos):
    d = x.shape[-1]
    inv = ROPE_BASE ** (-jnp.arange(0, d, 2, dtype=F32) / d)
    ang = pos.astype(F32)[:, None] * inv[None, :]
    cos = jnp.cos(ang)[None, :, None, :]
    sin = jnp.sin(ang)[None, :, None, :]
    x1, x2 = x[..., : d // 2], x[..., d // 2:]
    return jnp.concatenate([x1 * cos - x2 * sin, x1 * sin + x2 * cos], axis=-1)


def _retention_chunk(S, qkv, lg):
    q, k, v = qkv
    C = q.shape[1]
    n = jnp.arange(C, dtype=F32)
    diff = n[:, None] - n[None, :]
    causal = diff >= 0
    dmat = jnp.where(causal[None], jnp.exp(jnp.where(causal, diff, 0.0)[None] * lg[:, None, None]), 0.0)
    scores = jnp.einsum('bchd,bshd->bhcs', q, k) * dmat[None]
    inner = jnp.einsum('bhcs,bshe->bche', scores, v)
    q_decay = jnp.exp((n + 1.0)[:, None] * lg[None, :])
    cross = jnp.einsum('bchd,bhde->bche', q, S) * q_decay[None, :, :, None]
    k_decay = jnp.exp((C - 1.0 - n)[:, None] * lg[None, :])
    S_new = jnp.exp(C * lg)[None, :, None, None] * S + jnp.einsum('bchd,bche->bhde', k * k_decay[None, :, :, None], v)
    return S_new, inner + cross


def _retention_branch(q, k, v, g, pos, S0, gn_g, chunk):
    B, L = q.shape[:2]
    lg = jnp.log1p(-jnp.exp2(-5.0 - jnp.arange(RET_HEADS, dtype=F32)))
    qh = _rope(q.astype(F32).reshape(B, L, RET_HEADS, RET_DK), pos)
    kh = _rope(k.astype(F32).reshape(B, L, RET_HEADS, RET_DK), pos) * (RET_DK ** -0.5)
    vh = v.astype(F32).reshape(B, L, RET_HEADS, RET_DV)
    nc = L // chunk
    to_chunks = lambda t: jnp.swapaxes(t.reshape(B, nc, chunk, *t.shape[2:]), 0, 1)
    S_fin, o = lax.scan(lambda S, c: _retention_chunk(S, c, lg), S0.astype(F32),
                        (to_chunks(qh), to_chunks(kh), to_chunks(vh)))
    o = jnp.swapaxes(o, 0, 1).reshape(B, L, RET_HEADS, RET_DV)
    o = _stdnorm(o) * gn_g.astype(F32)
    out = jax.nn.silu(g.astype(F32)) * o.reshape(B, L, RET_V)
    return out.astype(q.dtype), S_fin


def _sgu_branch(u, v, ln_g, ws, bs):
    B, L, _ = u.shape
    C = min(SG_CHUNK, L)
    nc = L // C
    vn = _stdnorm(v.astype(F32)) * ln_g.astype(F32)
    vg = vn.reshape(B, nc, C, SG_GROUPS, SG_GW)
    w = jnp.tril(ws[:, :C, :C].astype(F32))
    bias = jnp.swapaxes(bs[:, :C].astype(F32), 0, 1)
    mixed = jnp.einsum('gts,bnsgc->bntgc', w, vg) + bias[None, None, :, :, None]
    out = u.astype(F32) * mixed.reshape(B, L, SG_W)
    return out.astype(u.dtype), vn.astype(u.dtype)


def _mem_kv(mem, g, w):
    B, M, _ = mem.shape
    kv = _rmsnorm(mem, g) @ w
    k, v = jnp.split(kv, 2, axis=-1)
    return k.reshape(B, M, X_HEADS, X_DH), v.reshape(B, M, X_HEADS, X_DH)


def _xattn(q, mk, mv):
    B, L, _ = q.shape
    qh = q.astype(F32).reshape(B, L, X_HEADS, X_DH)
    s = jnp.einsum('blhd,bmhd->bhlm', qh, mk.astype(F32)) * (X_DH ** -0.5)
    p = jax.nn.softmax(s, axis=-1)
    o = jnp.einsum('bhlm,bmhd->blhd', p, mv.astype(F32))
    return o.reshape(B, L, X_W).astype(q.dtype)


def _mixer_block(x, pos, S0, mk, mv, norm_g, w_in, b_gate, ret_gn_g, sg_ln_g, sg_ws, sg_bs,
                 w_br_ret, w_br_sg, w_br_x, w_o, ret_chunk):
    B, L, _ = x.shape
    h = _rmsnorm(x, norm_g)
    p = h @ w_in
    cuts = np.cumsum(IN_WIDTHS)[:-1].tolist()
    q, k, v, g, su, sv, xq, gt = jnp.split(p, cuts, axis=-1)
    o_ret, S_new = _retention_branch(q, k, v, g, pos, S0, ret_gn_g, ret_chunk)
    o_sg, v_rows = _sgu_branch(jax.nn.gelu(su), jax.nn.gelu(sv), sg_ln_g, sg_ws, sg_bs)
    o_x = _xattn(xq, mk, mv)
    g_ret, g_sg, g_x = jnp.split(jax.nn.sigmoid(gt + b_gate), N_BRANCH, axis=-1)
    merged = g_ret * (o_ret @ w_br_ret) + g_sg * (o_sg @ w_br_sg) + g_x * (o_x @ w_br_x)
    return x + merged @ w_o, S_new, v_rows


def _ffn_block(x, prev, norm_g, w_up, conv_w, conv_b, w_down):
    L = x.shape[1]
    z = _rmsnorm(x, norm_g) @ w_up
    zp = jnp.concatenate([prev.astype(z.dtype), z], axis=1)
    zc = conv_b + sum(conv_w[j] * zp[:, j:j + L] for j in range(CONV_W))
    a, b = jnp.split(zc, 2, axis=-1)
    return x + (jax.nn.gelu(a) * b) @ w_down, zp[:, L:]


def setup_inputs(seed: int = 0) -> dict:
    key = jax.random.key(seed)
    ks = jax.random.split(key, 32)
    nrm = lambda i, shape, s: jax.random.normal(ks[i], shape, F32) * s
    return {
        "x_prompt": nrm(0, (BATCH, SEQ, D_MODEL), 1.0),
        "x_sample": nrm(1, (DEC_BATCH, DEC_SEQ, D_MODEL), 1.0),
        "mem_prompt": nrm(2, (BATCH, MEM_LEN, D_MODEL), 1.0),
        "state_ret": nrm(3, (DEPTH, DEC_BATCH, RET_HEADS, RET_DK, RET_DV), 0.1),
        "state_conv": nrm(4, (DEPTH, DEC_BATCH, CONV_W - 1, 2 * D_FF), 1.0),
        "cache_mem_k": nrm(5, (DEPTH, DEC_BATCH, MEM_LEN, X_HEADS, X_DH), 1.0),
        "cache_mem_v": nrm(6, (DEPTH, DEC_BATCH, MEM_LEN, X_HEADS, X_DH), 1.0),
        "norm_mix_g": 1.0 + nrm(7, (DEPTH, D_MODEL), 0.02),
        "w_in": nrm(8, (DEPTH, D_MODEL, IN_COLS), D_MODEL ** -0.5),
        "b_gate": nrm(9, (DEPTH, N_BRANCH * D_MODEL), 0.01),
        "ret_gn_g": 1.0 + nrm(10, (DEPTH, RET_HEADS, RET_DV), 0.02),
        "sg_ln_g": 1.0 + nrm(11, (DEPTH, SG_W), 0.02),
        "sg_ws": nrm(12, (DEPTH, SG_GROUPS, SG_CHUNK, SG_CHUNK), SG_CHUNK ** -0.5),
        "sg_bs": 1.0 + nrm(13, (DEPTH, SG_GROUPS, SG_CHUNK), 0.01),
        "mem_norm_g": 1.0 + nrm(14, (DEPTH, D_MODEL), 0.02),
        "w_mem_kv": nrm(15, (DEPTH, D_MODEL, 2 * X_W), D_MODEL ** -0.5),
        "w_br_ret": nrm(16, (DEPTH, RET_V, D_MODEL), RET_V ** -0.5),
        "w_br_sg": nrm(17, (DEPTH, SG_W, D_MODEL), SG_W ** -0.5),
        "w_br_x": nrm(18, (DEPTH, X_W, D_MODEL), X_W ** -0.5),
        "w_o": nrm(19, (DEPTH, D_MODEL, D_MODEL), D_MODEL ** -0.5),
        "norm_ffn_g": 1.0 + nrm(20, (DEPTH, D_MODEL), 0.02),
        "w_up": nrm(21, (DEPTH, D_MODEL, 2 * D_FF), D_MODEL ** -0.5),
        "conv_w": nrm(22, (DEPTH, CONV_W, 2 * D_FF), CONV_W ** -0.5),
        "conv_b": nrm(23, (DEPTH, 2 * D_FF), 0.01),
        "w_down": nrm(24, (DEPTH, D_FF, D_MODEL), D_FF ** -0.5),
        "norm_final_g": 1.0 + nrm(25, (D_MODEL,), 0.02),
    }


def reference(x_prompt, x_sample, mem_prompt, state_ret, state_conv, cache_mem_k, cache_mem_v,
              norm_mix_g, w_in, b_gate, ret_gn_g, sg_ln_g, sg_ws, sg_bs, mem_norm_g, w_mem_kv,
              w_br_ret, w_br_sg, w_br_x, w_o, norm_ffn_g, w_up, conv_w, conv_b, w_down, norm_final_g):
    Bp, Lp, _ = x_prompt.shape
    Bs, Ls, _ = x_sample.shape
    pos_p = jnp.arange(Lp, dtype=jnp.int32)
    pos_s = PAST_LEN + jnp.arange(Ls, dtype=jnp.int32)
    xp, xs = x_prompt, x_sample
    ret_p, conv_p, mk_p_l, mv_p_l, ret_s, conv_s, sgv_s = [], [], [], [], [], [], []
    for l in range(DEPTH):
        mk_p, mv_p = _mem_kv(mem_prompt, mem_norm_g[l], w_mem_kv[l])
        S0 = jnp.zeros((Bp, RET_HEADS, RET_DK, RET_DV), F32)
        xp, Sp, _ = _mixer_block(xp, pos_p, S0, mk_p, mv_p, norm_mix_g[l], w_in[l], b_gate[l],
                                 ret_gn_g[l], sg_ln_g[l], sg_ws[l], sg_bs[l], w_br_ret[l],
                                 w_br_sg[l], w_br_x[l], w_o[l], min(RET_CHUNK, Lp))
        c0 = jnp.zeros((Bp, CONV_W - 1, 2 * D_FF), xp.dtype)
        xp, cp = _ffn_block(xp, c0, norm_ffn_g[l], w_up[l], conv_w[l], conv_b[l], w_down[l])
        xs, Ss, vs = _mixer_block(xs, pos_s, state_ret[l], cache_mem_k[l], cache_mem_v[l],
                                  norm_mix_g[l], w_in[l], b_gate[l], ret_gn_g[l], sg_ln_g[l],
                                  sg_ws[l], sg_bs[l], w_br_ret[l], w_br_sg[l], w_br_x[l], w_o[l], Ls)
        xs, cs = _ffn_block(xs, state_conv[l], norm_ffn_g[l], w_up[l], conv_w[l], conv_b[l], w_down[l])
        ret_p.append(Sp.astype(x_prompt.dtype))
        conv_p.append(cp)
        mk_p_l.append(mk_p)
        mv_p_l.append(mv_p)
        ret_s.append(Ss.astype(x_sample.dtype))
        conv_s.append(cs)
        sgv_s.append(vs)
    y_prompt = _rmsnorm(xp, norm_final_g)
    y_sample = _rmsnorm(xs, norm_final_g)
    return (y_prompt, y_sample, jnp.stack(ret_p), jnp.stack(conv_p), jnp.stack(mk_p_l),
            jnp.stack(mv_p_l), jnp.stack(ret_s), jnp.stack(conv_s), jnp.stack(sgv_s))
```

```python
import functools

import numpy as np
import jax
import jax.numpy as jnp
from jax import lax
from jax.experimental import pallas as pl
from jax.experimental.pallas import tpu as pltpu

F32 = jnp.float32
BF16 = jnp.bfloat16

D_MODEL = 1024
HEADS = 4
DK = 128
DV = 256
QK_W = HEADS * DK
V_W = HEADS * DV
SG_W = 512
X_W = 512
MEM_LEN = 256
D_FF = 2816
CHUNK = 128
ROPE_BASE = 10000.0
EPS = 1e-6
PAST_LEN = 16384

C_Q, C_K, C_V, C_G, C_SU, C_SV, C_XQ, C_GT, C_END = 0, 512, 1024, 2048, 3072, 3584, 4096, 4608, 7680

PROMPT_TILE = 256
SAMPLE_BB = 8
EXP = HEADS
FFN_CW = 256
VMEM_LIMIT = 56 * 1024 * 1024


def _rms(x, g):
    return x * lax.rsqrt(jnp.mean(x * x, axis=-1, keepdims=True) + EPS) * g


def _stdnorm(x):
    mu = jnp.mean(x, axis=-1, keepdims=True)
    xc = x - mu
    var = jnp.mean(xc * xc, axis=-1, keepdims=True)
    return xc * lax.rsqrt(var + EPS)


def _gelu(x):
    return 0.5 * x * (1.0 + jnp.tanh(np.float32(np.sqrt(2.0 / np.pi)) * (x + 0.044715 * (x * x * x))))


def _sigmoid(x):
    return 1.0 / (1.0 + jnp.exp(-x))


def _softmax(s):
    e = jnp.exp(s - jnp.max(s, axis=-1, keepdims=True))
    return e / jnp.sum(e, axis=-1, keepdims=True)


def _mm(a, b):
    return jnp.dot(a.astype(BF16), b.astype(BF16), preferred_element_type=F32)


def _mm_nt(a, b):
    return lax.dot_general(a.astype(BF16), b.astype(BF16), (((1,), (1,)), ((), ())),
                           preferred_element_type=F32)


def _mm_tn(a, b):
    return lax.dot_general(a.astype(BF16), b.astype(BF16), (((0,), (0,)), ((), ())),
                           preferred_element_type=F32)


def _rope(x, cos, sin):
    return x * cos + pltpu.roll(x, DK // 2, 1) * sin


def _tril(w):
    r = lax.broadcasted_iota(jnp.int32, w.shape, 0)
    c = lax.broadcasted_iota(jnp.int32, w.shape, 1)
    return jnp.where(r >= c, w, 0.0)


def _sgu(u, vn, ws_ref, sgb_ref, out_ref):
    for g in range(HEADS):
        w = _tril(ws_ref[g]).astype(BF16)
        cols = slice(g * DK, (g + 1) * DK)
        for c in range(u.shape[0] // CHUNK):
            rows = slice(c * CHUNK, (c + 1) * CHUNK)
            mixed = jnp.dot(w, vn[rows, cols].astype(BF16), preferred_element_type=F32) + sgb_ref[:, cols]
            out_ref[rows, cols] = (u[rows, cols] * mixed).astype(out_ref.dtype)


def _conv_gate(zs_ref, n, lo, cw_a, cb_a, cw_b, cb_b, fix=None):
    def conv(cols, cw, cb, k):
        z0 = zs_ref[8:8 + n, cols]
        z1 = zs_ref[7:7 + n, cols]
        z2 = zs_ref[6:6 + n, cols]
        if fix is not None:
            z1, z2 = fix(z1, z2, k)
        return cb + cw[0:1] * z2 + cw[1:2] * z1 + cw[2:3] * z0
    a = conv(slice(lo, lo + cw_a.shape[1]), cw_a, cb_a, 0)
    b = conv(slice(lo + cw_a.shape[1], lo + 2 * cw_a.shape[1]), cw_b, cb_b, 1)
    return _gelu(a) * b


def _const_spec(shape):
    nd = len(shape)
    return pl.BlockSpec(shape, lambda *_: (0,) * nd, pipeline_mode=pl.Buffered(1))


def _smem_spec():
    return pl.BlockSpec(memory_space=pltpu.SMEM)


def _params(sem):
    return pltpu.CompilerParams(dimension_semantics=sem, vmem_limit_bytes=VMEM_LIMIT)


def _memkv_kernel(mem_ref, g_ref, w_ref, k_ref, v_ref, kb_ref, vb_ref):
    kv = _mm(_rms(mem_ref[...], g_ref[...]), w_ref[...])
    k, v = kv[:, :X_W], kv[:, X_W:]
    k_ref[...] = k
    v_ref[...] = v
    kb_ref[...] = k.astype(BF16)
    vb_ref[...] = v.astype(BF16)


def _memkv(mem, g, w):
    b = mem.shape[0]
    blk = pl.BlockSpec((None, MEM_LEN, X_W), lambda i: (i, 0, 0))
    return pl.pallas_call(
        _memkv_kernel,
        grid=(b,),
        in_specs=[pl.BlockSpec((None, MEM_LEN, D_MODEL), lambda i: (i, 0, 0)),
                  _const_spec((1, D_MODEL)), _const_spec((D_MODEL, 2 * X_W))],
        out_specs=[blk, blk, blk, blk],
        out_shape=[jax.ShapeDtypeStruct((b, MEM_LEN, X_W), F32)] * 2
        + [jax.ShapeDtypeStruct((b, MEM_LEN, X_W), BF16)] * 2,
        compiler_params=_params(("arbitrary",)),
        name="mem_kv",
    )(mem, g, w)


def _mixer_kernel(gc_ref, x_ref, cq_ref, sq_ref, ck_ref, sk_ref, mk_ref, mv_ref, ng_ref, win_ref, bg_ref,
                  gn_ref, lng_ref, ws_ref, sgb_ref, dmat_ref, qdec_ref, kdec_ref,
                  wr_ref, wsg_ref, wx_ref, wo_ref,
                  xo_ref, s_ref, oret_ref, osg_ref, ox_ref):
    tile = x_ref.shape[0]

    @pl.when(pl.program_id(1) == 0)
    def _():
        s_ref[...] = jnp.zeros_like(s_ref)

    x = x_ref[...]
    h = _rms(x, ng_ref[...]).astype(BF16)

    def proj(a, b):
        return jnp.dot(h, win_ref[:, a:b], preferred_element_type=F32)

    q = proj(C_Q, C_K)
    k = proj(C_K, C_V)
    v = proj(C_V, C_G)
    gsil = proj(C_G, C_SU)
    gsil = gsil * _sigmoid(gsil)
    cq, sq, ck, sk = cq_ref[...], sq_ref[...], ck_ref[...], sk_ref[...]
    for hd in range(HEADS):
        kcols = slice(hd * DK, (hd + 1) * DK)
        vcols = slice(hd * DV, (hd + 1) * DV)
        qr = _rope(q[:, kcols], cq, sq)
        kr = _rope(k[:, kcols], ck, sk)
        for c in range(tile // CHUNK):
            rows = slice(c * CHUNK, (c + 1) * CHUNK)
            qc, kc, vc = qr[rows], kr[rows], v[rows, vcols].astype(BF16)
            state = s_ref[hd]
            sc = _mm_nt(qc, kc) * dmat_ref[hd]
            lhs = jnp.concatenate([sc.astype(BF16), (qc * qdec_ref[hd]).astype(BF16)], axis=1)
            rhs = jnp.concatenate([vc, state.astype(BF16)], axis=0)
            o = jnp.dot(lhs, rhs, preferred_element_type=F32)
            s_ref[hd] = gc_ref[hd] * state + _mm_tn(kc * kdec_ref[hd], vc)
            on = _stdnorm(o) * gn_ref[:, vcols]
            oret_ref[rows, vcols] = (gsil[rows, vcols] * on).astype(BF16)

    u = _gelu(proj(C_SU, C_SV))
    vn = _stdnorm(_gelu(proj(C_SV, C_XQ))) * lng_ref[...]
    _sgu(u, vn, ws_ref, sgb_ref, osg_ref)

    xq = proj(C_XQ, C_GT)
    for hd in range(HEADS):
        cols = slice(hd * DK, (hd + 1) * DK)
        p = _softmax(_mm_nt(xq[:, cols], mk_ref[:, cols]) * np.float32(DK ** -0.5))
        ox_ref[:, cols] = _mm(p, mv_ref[:, cols]).astype(BF16)

    gates = _sigmoid(proj(C_GT, C_END) + bg_ref[...])
    merged = (gates[:, :D_MODEL] * jnp.dot(oret_ref[...], wr_ref[...], preferred_element_type=F32)
              + gates[:, D_MODEL:2 * D_MODEL] * jnp.dot(osg_ref[...], wsg_ref[...], preferred_element_type=F32)
              + gates[:, 2 * D_MODEL:] * jnp.dot(ox_ref[...], wx_ref[...], preferred_element_type=F32))
    xo_ref[...] = x + _mm(merged, wo_ref[...])


def _prompt_mixer(x, tabs, mk_b, mv_b, dec, p):
    b, l, _ = x.shape
    t = PROMPT_TILE
    tok = lambda w: pl.BlockSpec((None, t, w), lambda i, j: (i, j, 0))
    tab = pl.BlockSpec((t, DK), lambda i, j: (j, 0))
    mem = pl.BlockSpec((None, MEM_LEN, X_W), lambda i, j: (i, 0, 0))
    consts = [p["norm_mix_g"], p["w_in"], p["b_gate"], p["gn_g"], p["sg_ln_g"], p["sg_ws"], p["sg_bias"],
              dec["dmat"], dec["qdec"], dec["kdec"], p["w_br_ret"], p["w_br_sg"], p["w_br_x"], p["w_o"]]
    return pl.pallas_call(
        _mixer_kernel,
        grid=(b, l // t),
        in_specs=[_smem_spec(), tok(D_MODEL), tab, tab, tab, tab, mem, mem]
        + [_const_spec(c.shape) for c in consts],
        out_specs=[tok(D_MODEL), pl.BlockSpec((None, HEADS, DK, DV), lambda i, j: (i, 0, 0, 0))],
        out_shape=[jax.ShapeDtypeStruct((b, l, D_MODEL), F32),
                   jax.ShapeDtypeStruct((b, HEADS, DK, DV), F32)],
        scratch_shapes=[pltpu.VMEM((t, V_W), BF16), pltpu.VMEM((t, SG_W), BF16), pltpu.VMEM((t, X_W), BF16)],
        compiler_params=_params(("arbitrary", "arbitrary")),
        name="prompt_mixer",
    )(dec["gc"], x, *tabs, mk_b, mv_b, *consts)


def _ffn_kernel(x_ref, ng_ref, wup_ref, cw_ref, cb_ref, wdn_ref, nf_ref, y_ref, tail_ref, zs_ref):
    tile = x_ref.shape[0]
    first = pl.program_id(1) == 0

    @pl.when(first)
    def _():
        zs_ref[0:8, :] = jnp.zeros((8, 2 * D_FF), F32)

    @pl.when(jnp.logical_not(first))
    def _():
        zs_ref[0:8, :] = zs_ref[tile:tile + 8, :]

    x = x_ref[...]
    zs_ref[8:8 + tile, :] = _mm(_rms(x, ng_ref[...]), wup_ref[...])
    gate = _conv_gate(zs_ref, tile, 0, cw_ref[:, :D_FF], cb_ref[:, :D_FF], cw_ref[:, D_FF:], cb_ref[:, D_FF:])
    y = x + _mm(gate, wdn_ref[...])
    y_ref[...] = _rms(y, nf_ref[...])
    tail_ref[...] = zs_ref[tile:tile + 8, :]


def _prompt_ffn(x, p):
    b, l, _ = x.shape
    t = PROMPT_TILE
    tok = pl.BlockSpec((None, t, D_MODEL), lambda i, j: (i, j, 0))
    consts = [p["norm_ffn_g"], p["w_up"], p["conv_w"], p["conv_b"], p["w_down"], p["norm_final_g"]]
    return pl.pallas_call(
        _ffn_kernel,
        grid=(b, l // t),
        in_specs=[tok] + [_const_spec(c.shape) for c in consts],
        out_specs=[tok, pl.BlockSpec((None, 8, 2 * D_FF), lambda i, j: (i, 0, 0))],
        out_shape=[jax.ShapeDtypeStruct((b, l, D_MODEL), F32), jax.ShapeDtypeStruct((b, 8, 2 * D_FF), F32)],
        scratch_shapes=[pltpu.VMEM((t + 8, 2 * D_FF), F32)],
        compiler_params=_params(("arbitrary", "arbitrary")),
        name="prompt_ffn",
    )(x, *consts)


def _row_head(n):
    return (lax.broadcasted_iota(jnp.int32, (n, 1), 0) >> 2) & (HEADS - 1)


def _sample_pre_kernel(x_ref, cq_ref, sq_ref, ck_ref, sk_ref, kdec_ref, ng_ref, win_ref, bg_ref, lng_ref,
                       ws_ref, sgb_ref,
                       qe_ref, kre_ref, kde_ref, ve_ref, ge_ref, xqe_ref, osg_ref, vrows_ref, gates_ref):
    n = x_ref.shape[0]
    ne = n * EXP
    h = _rms(x_ref[...], ng_ref[...]).astype(BF16)
    r = lax.broadcasted_iota(jnp.int32, (ne, n), 0)
    c = lax.broadcasted_iota(jnp.int32, (ne, n), 1)
    rep = jnp.where(c == ((r >> 4) << 2) + (r & 3), 1.0, 0.0).astype(BF16)
    he = jnp.dot(rep, h, preferred_element_type=F32).astype(BF16)
    hh = _row_head(ne)

    def proj(hm, a, b):
        return jnp.dot(hm, win_ref[:, a:b], preferred_element_type=F32)

    q, k = proj(he, C_Q, C_K), proj(he, C_K, C_V)
    xq = proj(he, C_XQ, C_GT)
    cq, sq, ck, sk, kdec = cq_ref[...], sq_ref[...], ck_ref[...], sk_ref[...], kdec_ref[...]
    for hd in range(HEADS):
        cols = slice(hd * DK, (hd + 1) * DK)
        own = hh == hd
        kr = _rope(k[:, cols], ck, sk)
        qe_ref[:, cols] = jnp.where(own, _rope(q[:, cols], cq, sq), 0.0).astype(BF16)
        kre_ref[:, cols] = jnp.where(own, kr, 0.0).astype(BF16)
        kde_ref[:, cols] = jnp.where(own, kr * kdec, 0.0).astype(BF16)
        xqe_ref[:, cols] = jnp.where(own, xq[:, cols], 0.0).astype(BF16)
    v = proj(he, C_V, C_G)
    g = proj(he, C_G, C_SU)
    ve = jnp.zeros((ne, DV), F32)
    ge = jnp.zeros((ne, DV), F32)
    for hd in range(HEADS):
        cols = slice(hd * DV, (hd + 1) * DV)
        own = hh == hd
        ve = jnp.where(own, v[:, cols], ve)
        ge = jnp.where(own, g[:, cols], ge)
    ve_ref[...] = ve.astype(BF16)
    ge_ref[...] = ge * _sigmoid(ge)

    u = _gelu(proj(h, C_SU, C_SV))
    vn = _stdnorm(_gelu(proj(h, C_SV, C_XQ))) * lng_ref[...]
    vrows_ref[...] = vn
    _sgu(u, vn, ws_ref, sgb_ref, osg_ref)
    gates_ref[...] = _sigmoid(proj(h, C_GT, C_END) + bg_ref[...])


def _sample_pre(x, tabs, kdec_rows, p, ws_k, sgb_k):
    n = x.shape[0]
    t = CHUNK
    te = t * EXP
    consts = [p["norm_mix_g"], p["w_in"], p["b_gate"], p["sg_ln_g"], ws_k, sgb_k]
    rowblk = lambda rows, w: pl.BlockSpec((rows, w), lambda i: (i, 0))
    outs = [(te, QK_W, BF16), (te, QK_W, BF16), (te, QK_W, BF16), (te, DV, BF16), (te, DV, F32),
            (te, X_W, BF16), (t, SG_W, BF16), (t, SG_W, F32), (t, 3 * D_MODEL, F32)]
    return pl.pallas_call(
        _sample_pre_kernel,
        grid=(n // t,),
        in_specs=[rowblk(t, D_MODEL)] + [rowblk(te, DK)] * 5 + [_const_spec(c.shape) for c in consts],
        out_specs=[rowblk(r, w) for r, w, _ in outs],
        out_shape=[jax.ShapeDtypeStruct((r * (n // t), w), d) for r, w, d in outs],
        compiler_params=_params(("arbitrary",)),
        name="sample_pre",
    )(x, *tabs, kdec_rows, *consts)


def _sample_state_kernel(gc_ref, qe_ref, kre_ref, kde_ref, ve_ref, ge_ref, xqe_ref, s_ref, mk_ref, mv_ref,
                         dmat_ref, qdec_ref, gn_ref,
                         oret_ref, ox_ref, so_ref, cross_ref, oxs_ref):
    rows_n = qe_ref.shape[0]
    per = HEADS * EXP
    q, ve, xq = qe_ref[...], ve_ref[...], xqe_ref[...]
    sc = _mm_nt(q, kre_ref[...]) * dmat_ref[...]
    inner = jnp.dot(sc.astype(BF16), ve, preferred_element_type=F32)
    kd = kde_ref[...]
    rowb = lax.broadcasted_iota(jnp.int32, (rows_n, 1), 0) >> 4
    for b in range(rows_n // per):
        rows = slice(b * per, (b + 1) * per)
        state = s_ref[b]
        cross_ref[rows, :] = jnp.dot(q[rows], state.astype(BF16), preferred_element_type=F32)
        upd = _mm_tn(kd, jnp.where(rowb == b, ve, jnp.zeros_like(ve)))
        for hd in range(HEADS):
            hr = slice(hd * DK, (hd + 1) * DK)
            so_ref[b, hr, :] = gc_ref[hd] * state[hr] + upd[hr]
        p = _softmax(_mm_nt(xq[rows], mk_ref[b]) * np.float32(DK ** -0.5))
        oxs_ref[rows, :] = _mm(p, mv_ref[b])
    o = inner + cross_ref[...] * qdec_ref[...]
    og = ge_ref[...] * (_stdnorm(o) * gn_ref[...])
    hh = _row_head(rows_n)
    for hd in range(HEADS):
        oret_ref[:, hd * DV:(hd + 1) * DV] = jnp.where(hh == hd, og, 0.0)
    colh = lax.broadcasted_iota(jnp.int32, (rows_n, X_W), 1) >> 7
    ox_ref[...] = jnp.where(colh == hh, oxs_ref[...], 0.0)


def _sample_state(pre, state, mk, mv, dec):
    qe, kre, kde, ve, ge, xqe = pre
    nb = state.shape[0]
    bb = SAMPLE_BB
    rb = bb * HEADS * EXP
    rowblk = lambda w: pl.BlockSpec((rb, w), lambda i: (i, 0))
    batblk = lambda a, c: pl.BlockSpec((bb, a, c), lambda i: (i, 0, 0))
    ne = qe.shape[0]
    return pl.pallas_call(
        _sample_state_kernel,
        grid=(nb // bb,),
        in_specs=[_smem_spec(), rowblk(QK_W), rowblk(QK_W), rowblk(QK_W), rowblk(DV), rowblk(DV), rowblk(X_W),
                  batblk(HEADS * DK, DV), batblk(MEM_LEN, X_W), batblk(MEM_LEN, X_W),
                  _const_spec((rb, rb)), _const_spec((rb, DV)), _const_spec((rb, DV))],
        out_specs=[rowblk(V_W), rowblk(X_W), batblk(HEADS * DK, DV)],
        out_shape=[jax.ShapeDtypeStruct((ne, V_W), F32), jax.ShapeDtypeStruct((ne, X_W), F32),
                   jax.ShapeDtypeStruct(state.shape, F32)],
        scratch_shapes=[pltpu.VMEM((rb, DV), F32), pltpu.VMEM((rb, X_W), F32)],
        compiler_params=_params(("arbitrary",)),
        name="sample_state",
    )(dec["gc"], qe, kre, kde, ve, ge, xqe, state, mk, mv, dec["dmat"], dec["qdec"], dec["gn"])


def _sample_merge_kernel(x_ref, orete_ref, oxe_ref, osg_ref, gates_ref, wr_ref, wsg_ref, wx_ref, wo_ref, xo_ref):
    n = x_ref.shape[0]
    ne = n * EXP
    r = lax.broadcasted_iota(jnp.int32, (n, ne), 0)
    c = lax.broadcasted_iota(jnp.int32, (n, ne), 1)
    col = jnp.where(r == ((c >> 4) << 2) + (c & 3), 1.0, 0.0).astype(BF16)
    oret = jnp.dot(col, orete_ref[...].astype(BF16), preferred_element_type=F32)
    ox = jnp.dot(col, oxe_ref[...].astype(BF16), preferred_element_type=F32)
    gates = gates_ref[...]
    merged = (gates[:, :D_MODEL] * _mm(oret, wr_ref[...])
              + gates[:, D_MODEL:2 * D_MODEL] * jnp.dot(osg_ref[...], wsg_ref[...], preferred_element_type=F32)
              + gates[:, 2 * D_MODEL:] * _mm(ox, wx_ref[...]))
    xo_ref[...] = x_ref[...] + _mm(merged, wo_ref[...])


def _sample_merge(x, orete, oxe, osg, gates, p):
    ins = [x, orete, oxe, osg, gates, p["w_br_ret"], p["w_br_sg"], p["w_br_x"], p["w_o"]]
    return pl.pallas_call(
        _sample_merge_kernel,
        grid=(1,),
        in_specs=[_const_spec(a.shape) for a in ins],
        out_specs=pl.BlockSpec(x.shape, lambda i: (0, 0)),
        out_shape=jax.ShapeDtypeStruct(x.shape, F32),
        compiler_params=_params(("arbitrary",)),
        name="sample_merge",
    )(*ins)


def _sample_ffn_kernel(x_ref, ng_ref, wa_ref, wb_ref, cwa_ref, cwb_ref, cba_ref, cbb_ref, p1a_ref, p1b_ref,
                       p2a_ref, p2b_ref, wdn_ref, nf_ref,
                       y_ref, za_ref, zb_ref, h_ref, acc_ref, zs_ref):
    n = x_ref.shape[0]
    cw = wa_ref.shape[1]
    j = pl.program_id(0)

    @pl.when(j == 0)
    def _():
        h_ref[...] = _rms(x_ref[...], ng_ref[...]).astype(BF16)
        acc_ref[...] = jnp.zeros_like(acc_ref)
        zs_ref[0:8, :] = jnp.zeros((8, 2 * cw), F32)

    h = h_ref[...]
    za = jnp.dot(h, wa_ref[...], preferred_element_type=F32)
    zb = jnp.dot(h, wb_ref[...], preferred_element_type=F32)
    za_ref[...] = za
    zb_ref[...] = zb
    zs_ref[8:8 + n, :cw] = za
    zs_ref[8:8 + n, cw:] = zb
    pos = lax.broadcasted_iota(jnp.int32, (n, 1), 0) & 3
    carry = ((p1a_ref, p2a_ref), (p1b_ref, p2b_ref))

    def fix(z1, z2, k):
        return jnp.where(pos >= 1, z1, carry[k][0][...]), jnp.where(pos >= 2, z2, carry[k][1][...])

    gate = _conv_gate(zs_ref, n, 0, cwa_ref[...], cba_ref[...], cwb_ref[...], cbb_ref[...], fix)
    acc_ref[...] += _mm(gate, wdn_ref[...])

    @pl.when(j == pl.num_programs(0) - 1)
    def _():
        y_ref[...] = _rms(x_ref[...] + acc_ref[...], nf_ref[...])


def _sample_ffn(x, p1, p2, p):
    n = x.shape[0]
    cw = FFN_CW
    nch = D_FF // cw
    ca = lambda rows: pl.BlockSpec((rows, cw), lambda j: (0, j))
    cb = lambda rows: pl.BlockSpec((rows, cw), lambda j: (0, nch + j))
    full = pl.BlockSpec((n, D_MODEL), lambda j: (0, 0))
    vec = pl.BlockSpec((1, D_MODEL), lambda j: (0, 0))
    return pl.pallas_call(
        _sample_ffn_kernel,
        grid=(nch,),
        in_specs=[full, vec, ca(D_MODEL), cb(D_MODEL), ca(3), cb(3), ca(1), cb(1), ca(n), cb(n), ca(n), cb(n),
                  pl.BlockSpec((cw, D_MODEL), lambda j: (j, 0)), vec],
        out_specs=[full, pl.BlockSpec((n, cw), lambda j: (0, j)), pl.BlockSpec((n, cw), lambda j: (0, j))],
        out_shape=[jax.ShapeDtypeStruct((n, D_MODEL), F32), jax.ShapeDtypeStruct((n, D_FF), F32),
                   jax.ShapeDtypeStruct((n, D_FF), F32)],
        scratch_shapes=[pltpu.VMEM((n, D_MODEL), BF16), pltpu.VMEM((n, D_MODEL), F32),
                        pltpu.VMEM((n + 8, 2 * cw), F32)],
        compiler_params=_params(("arbitrary",)),
        name="sample_ffn",
    )(x, p["norm_ffn_g"], p["w_up"], p["w_up"], p["conv_w"], p["conv_w"], p["conv_b"], p["conv_b"],
      p1, p1, p2, p2, p["w_down"], p["norm_final_g"])


def _rope_tables(pos, scale):
    inv = ROPE_BASE ** (-jnp.arange(0, DK, 2, dtype=F32) / DK)
    ang = pos.astype(F32)[:, None] * inv[None, :]
    cos, sin = jnp.cos(ang), jnp.sin(ang)
    return jnp.concatenate([cos, cos], -1) * scale, jnp.concatenate([-sin, sin], -1) * scale


def _decay(chunk):
    lg = jnp.log1p(-jnp.exp2(-5.0 - jnp.arange(HEADS, dtype=F32)))
    n = jnp.arange(chunk, dtype=F32)
    diff = n[:, None] - n[None, :]
    causal = diff >= 0
    dmat = jnp.where(causal[None], jnp.exp(jnp.where(causal, diff, 0.0)[None] * lg[:, None, None]), 0.0)
    qdec = jnp.exp((n + 1.0)[None, :] * lg[:, None])
    kdec = jnp.exp((chunk - 1.0 - n)[None, :] * lg[:, None])
    return dmat, qdec, kdec, jnp.exp(chunk * lg)


def kernel(x_prompt, x_sample, mem_prompt, state_ret, state_conv, cache_mem_k, cache_mem_v, norm_mix_g, w_in,
           b_gate, ret_gn_g, sg_ln_g, sg_ws, sg_bs, mem_norm_g, w_mem_kv, w_br_ret, w_br_sg, w_br_x, w_o,
           norm_ffn_g, w_up, conv_w, conv_b, w_down, norm_final_g):
    bp, lp, _ = x_prompt.shape
    bs, ls, _ = x_sample.shape
    assert state_ret.shape[0] == 1 and ls == EXP and lp % PROMPT_TILE == 0 and bs % SAMPLE_BB == 0
    row = lambda a: a.reshape(1, -1)
    p = dict(norm_mix_g=row(norm_mix_g[0]), w_in=w_in[0].astype(BF16), b_gate=row(b_gate[0]),
             gn_g=row(ret_gn_g[0]), sg_ln_g=row(sg_ln_g[0]),
             w_br_ret=w_br_ret[0].astype(BF16), w_br_sg=w_br_sg[0].astype(BF16),
             w_br_x=w_br_x[0].astype(BF16), w_o=w_o[0].astype(BF16),
             norm_ffn_g=row(norm_ffn_g[0]), w_up=w_up[0].astype(BF16), conv_w=conv_w[0],
             conv_b=row(conv_b[0]), w_down=w_down[0].astype(BF16), norm_final_g=row(norm_final_g))
    scale = np.float32(DK ** -0.5)

    mk, mv, mk_b, mv_b = _memkv(mem_prompt, row(mem_norm_g[0]), w_mem_kv[0].astype(BF16))
    pos_p = jnp.arange(lp, dtype=jnp.int32)
    tabs_p = (*_rope_tables(pos_p, 1.0), *_rope_tables(pos_p, scale))
    dmat, qdec, kdec, gc = _decay(CHUNK)
    bcast = lambda a: jnp.broadcast_to(a[:, :, None], (HEADS, CHUNK, DK))
    dec_p = dict(dmat=dmat, qdec=bcast(qdec), kdec=bcast(kdec), gc=gc)
    pp = dict(p, sg_ws=sg_ws[0], sg_bias=jnp.repeat(sg_bs[0].T, DK, axis=1))
    x_mid, s_prompt = _prompt_mixer(x_prompt, tabs_p, mk_b, mv_b, dec_p, pp)
    y_prompt, tail = _prompt_ffn(x_mid, p)

    n = bs * ls
    per = HEADS * EXP
    pos_s = PAST_LEN + (jnp.arange(n * EXP, dtype=jnp.int32) & (ls - 1))
    tabs_s = (*_rope_tables(pos_s, 1.0), *_rope_tables(pos_s, scale))
    dmat4, qdec4, kdec4, gc4 = _decay(ls)
    kdec_rows = jnp.broadcast_to(jnp.tile(kdec4.reshape(per), bs)[:, None], (n * EXP, DK))
    eye = jnp.eye(CHUNK // ls, dtype=F32)
    ws_k = jnp.einsum("ab,gts->gatbs", eye, sg_ws[0][:, :ls, :ls]).reshape(HEADS, CHUNK, CHUNK)
    sgb_k = jnp.repeat(jnp.tile(sg_bs[0][:, :ls], (1, CHUNK // ls)).T, DK, axis=1)
    pre = _sample_pre(x_sample.reshape(n, D_MODEL), tabs_s, kdec_rows, p, ws_k, sgb_k)
    qe, kre, kde, ve, ge, xqe, osg, vrows, gates = pre

    rb = SAMPLE_BB * per
    blk16 = jnp.einsum("hk,hls->hlks", jnp.eye(HEADS, dtype=F32), dmat4).reshape(per, per)
    dec_s = dict(gc=gc4,
                 dmat=jnp.kron(jnp.eye(SAMPLE_BB, dtype=F32), blk16),
                 qdec=jnp.broadcast_to(jnp.tile(qdec4.reshape(per), SAMPLE_BB)[:, None], (rb, DV)),
                 gn=jnp.tile(jnp.repeat(ret_gn_g[0], ls, axis=0), (SAMPLE_BB, 1)))
    orete, oxe, s_sample = _sample_state(
        (qe, kre, kde, ve, ge, xqe), state_ret[0].reshape(bs, HEADS * DK, DV),
        cache_mem_k[0].reshape(bs, MEM_LEN, X_W), cache_mem_v[0].reshape(bs, MEM_LEN, X_W), dec_s)
    xs_mid = _sample_merge(x_sample.reshape(n, D_MODEL), orete, oxe, osg, gates, p)
    zero = jnp.zeros((bs, 1, 2 * D_FF), F32)
    sc = state_conv[0]
    p1 = jnp.concatenate([sc[:, 1:2], zero, zero, zero], axis=1).reshape(n, 2 * D_FF)
    p2 = jnp.concatenate([sc, zero, zero], axis=1).reshape(n, 2 * D_FF)
    y_sample, za, zb = _sample_ffn(xs_mid, p1, p2, p)
    z_s = jnp.concatenate([za, zb], axis=-1).reshape(bs, ls, 2 * D_FF)

    return (y_prompt, y_sample.reshape(bs, ls, D_MODEL),
            s_prompt[None], tail[None, :, 6:8],
            mk.reshape(1, bp, MEM_LEN, HEADS, DK), mv.reshape(1, bp, MEM_LEN, HEADS, DK),
            s_sample.reshape(1, bs, HEADS, DK, DV), z_s[None, :, ls - 2:],
            vrows.reshape(1, bs, ls, SG_W))
```

```python
import functools

import numpy as np
import jax
import jax.numpy as jnp
from jax import lax
from jax.experimental import pallas as pl
from jax.experimental.pallas import tpu as pltpu

F32 = jnp.float32
BF16 = jnp.bfloat16

D_MODEL = 1024
HEADS = 4
DK = 128
DV = 256
QK_W = HEADS * DK
V_W = HEADS * DV
SG_W = 512
X_W = 512
MEM_LEN = 256
D_FF = 2816
CHUNK = 128
ROPE_BASE = 10000.0
EPS = 1e-6
PAST_LEN = 16384

C_Q, C_K, C_V, C_G, C_SU, C_SV, C_XQ, C_GT, C_END = 0, 512, 1024, 2048, 3072, 3584, 4096, 4608, 7680

PROMPT_TILE = 256
SAMPLE_BB = 8
EXP = HEADS
FFN_CW = 256
VMEM_LIMIT = 56 * 1024 * 1024


def _rms(x, g):
    return x * lax.rsqrt(jnp.mean(x * x, axis=-1, keepdims=True) + EPS) * g


def _stdnorm(x):
    mu = jnp.mean(x, axis=-1, keepdims=True)
    xc = x - mu
    var = jnp.mean(xc * xc, axis=-1, keepdims=True)
    return xc * lax.rsqrt(var + EPS)


def _gelu(x):
    return 0.5 * x * (1.0 + jnp.tanh(np.float32(np.sqrt(2.0 / np.pi)) * (x + 0.044715 * (x * x * x))))


def _sigmoid(x):
    return 1.0 / (1.0 + jnp.exp(-x))


def _softmax(s):
    e = jnp.exp(s - jnp.max(s, axis=-1, keepdims=True))
    return e / jnp.sum(e, axis=-1, keepdims=True)


def _mm(a, b):
    return jnp.dot(a.astype(BF16), b.astype(BF16), preferred_element_type=F32)


def _mm_nt(a, b):
    return lax.dot_general(a.astype(BF16), b.astype(BF16), (((1,), (1,)), ((), ())),
                           preferred_element_type=F32)


def _mm_tn(a, b):
    return lax.dot_general(a.astype(BF16), b.astype(BF16), (((0,), (0,)), ((), ())),
                           preferred_element_type=F32)


def _rope(x, cos, sin):
    return x * cos + pltpu.roll(x, DK // 2, 1) * sin


def _tril(w):
    r = lax.broadcasted_iota(jnp.int32, w.shape, 0)
    c = lax.broadcasted_iota(jnp.int32, w.shape, 1)
    return jnp.where(r >= c, w, 0.0)


def _sgu(u, vn, ws_ref, sgb_ref, out_ref):
    for g in range(HEADS):
        w = _tril(ws_ref[g]).astype(BF16)
        cols = slice(g * DK, (g + 1) * DK)
        for c in range(u.shape[0] // CHUNK):
            rows = slice(c * CHUNK, (c + 1) * CHUNK)
            mixed = jnp.dot(w, vn[rows, cols].astype(BF16), preferred_element_type=F32) + sgb_ref[:, cols]
            out_ref[rows, cols] = (u[rows, cols] * mixed).astype(out_ref.dtype)


def _conv_gate(zs_ref, n, lo, cw_a, cb_a, cw_b, cb_b, fix=None):
    def conv(cols, cw, cb, k):
        z0 = zs_ref[8:8 + n, cols]
        z1 = zs_ref[7:7 + n, cols]
        z2 = zs_ref[6:6 + n, cols]
        if fix is not None:
            z1, z2 = fix(z1, z2, k)
        return cb + cw[0:1] * z2 + cw[1:2] * z1 + cw[2:3] * z0
    a = conv(slice(lo, lo + cw_a.shape[1]), cw_a, cb_a, 0)
    b = conv(slice(lo + cw_a.shape[1], lo + 2 * cw_a.shape[1]), cw_b, cb_b, 1)
    return _gelu(a) * b


def _const_spec(shape):
    nd = len(shape)
    return pl.BlockSpec(shape, lambda *_: (0,) * nd, pipeline_mode=pl.Buffered(1))


def _smem_spec():
    return pl.BlockSpec(memory_space=pltpu.SMEM)


def _params(sem):
    return pltpu.CompilerParams(dimension_semantics=sem, vmem_limit_bytes=VMEM_LIMIT)


def _memkv_kernel(mem_ref, g_ref, w_ref, k_ref, v_ref, kb_ref, vb_ref):
    kv = _mm(_rms(mem_ref[...], g_ref[...]), w_ref[...])
    k, v = kv[:, :X_W], kv[:, X_W:]
    for hd in range(HEADS):
        cols = slice(hd * DK, (hd + 1) * DK)
        k_ref[pl.ds(hd, MEM_LEN, stride=HEADS), :] = k[:, cols]
        v_ref[pl.ds(hd, MEM_LEN, stride=HEADS), :] = v[:, cols]
    kb_ref[...] = k.astype(BF16)
    vb_ref[...] = v.astype(BF16)


def _memkv(mem, g, w):
    b = mem.shape[0]
    blk = pl.BlockSpec((None, MEM_LEN, X_W), lambda i: (i, 0, 0))
    flat = pl.BlockSpec((None, MEM_LEN * HEADS, DK), lambda i: (i, 0, 0))
    return pl.pallas_call(
        _memkv_kernel,
        grid=(b,),
        in_specs=[pl.BlockSpec((None, MEM_LEN, D_MODEL), lambda i: (i, 0, 0)),
                  _const_spec((1, D_MODEL)), _const_spec((D_MODEL, 2 * X_W))],
        out_specs=[flat, flat, blk, blk],
        out_shape=[jax.ShapeDtypeStruct((b, MEM_LEN * HEADS, DK), F32)] * 2
        + [jax.ShapeDtypeStruct((b, MEM_LEN, X_W), BF16)] * 2,
        compiler_params=_params(("arbitrary",)),
        name="mem_kv",
    )(mem, g, w)


def _mixer_kernel(gc_ref, x_ref, cq_ref, sq_ref, ck_ref, sk_ref, mk_ref, mv_ref, ng_ref, win_ref, bg_ref,
                  gn_ref, lng_ref, ws_ref, sgb_ref, dmat_ref, qdec_ref, kdec_ref,
                  wr_ref, wsg_ref, wx_ref, wo_ref,
                  xo_ref, s_ref, oret_ref, osg_ref, ox_ref):
    tile = x_ref.shape[0]

    @pl.when(pl.program_id(1) == 0)
    def _():
        s_ref[...] = jnp.zeros_like(s_ref)

    x = x_ref[...]
    h = _rms(x, ng_ref[...]).astype(BF16)

    def proj(a, b):
        return jnp.dot(h, win_ref[:, a:b], preferred_element_type=F32)

    q = proj(C_Q, C_K)
    k = proj(C_K, C_V)
    v = proj(C_V, C_G)
    gsil = proj(C_G, C_SU)
    gsil = gsil * _sigmoid(gsil)
    cq, sq, ck, sk = cq_ref[...], sq_ref[...], ck_ref[...], sk_ref[...]
    for hd in range(HEADS):
        kcols = slice(hd * DK, (hd + 1) * DK)
        vcols = slice(hd * DV, (hd + 1) * DV)
        qr = _rope(q[:, kcols], cq, sq)
        kr = _rope(k[:, kcols], ck, sk)
        for c in range(tile // CHUNK):
            rows = slice(c * CHUNK, (c + 1) * CHUNK)
            qc, kc, vc = qr[rows], kr[rows], v[rows, vcols].astype(BF16)
            state = s_ref[hd]
            sc = _mm_nt(qc, kc) * dmat_ref[hd]
            lhs = jnp.concatenate([sc.astype(BF16), (qc * qdec_ref[hd]).astype(BF16)], axis=1)
            rhs = jnp.concatenate([vc, state.astype(BF16)], axis=0)
            o = jnp.dot(lhs, rhs, preferred_element_type=F32)
            s_ref[hd] = gc_ref[hd] * state + _mm_tn(kc * kdec_ref[hd], vc)
            on = _stdnorm(o) * gn_ref[:, vcols]
            oret_ref[rows, vcols] = (gsil[rows, vcols] * on).astype(BF16)

    u = _gelu(proj(C_SU, C_SV))
    vn = _stdnorm(_gelu(proj(C_SV, C_XQ))) * lng_ref[...]
    _sgu(u, vn, ws_ref, sgb_ref, osg_ref)

    xq = proj(C_XQ, C_GT)
    for hd in range(HEADS):
        cols = slice(hd * DK, (hd + 1) * DK)
        p = _softmax(_mm_nt(xq[:, cols], mk_ref[:, cols]) * np.float32(DK ** -0.5))
        ox_ref[:, cols] = _mm(p, mv_ref[:, cols]).astype(BF16)

    gates = _sigmoid(proj(C_GT, C_END) + bg_ref[...])
    merged = (gates[:, :D_MODEL] * jnp.dot(oret_ref[...], wr_ref[...], preferred_element_type=F32)
              + gates[:, D_MODEL:2 * D_MODEL] * jnp.dot(osg_ref[...], wsg_ref[...], preferred_element_type=F32)
              + gates[:, 2 * D_MODEL:] * jnp.dot(ox_ref[...], wx_ref[...], preferred_element_type=F32))
    xo_ref[...] = x + _mm(merged, wo_ref[...])


def _prompt_mixer(x, tabs, mk_b, mv_b, dec, p):
    b, l, _ = x.shape
    t = PROMPT_TILE
    tok = lambda w: pl.BlockSpec((None, t, w), lambda i, j: (i, j, 0))
    tab = pl.BlockSpec((t, DK), lambda i, j: (j, 0))
    mem = pl.BlockSpec((None, MEM_LEN, X_W), lambda i, j: (i, 0, 0))
    consts = [p["norm_mix_g"], p["w_in"], p["b_gate"], p["gn_g"], p["sg_ln_g"], p["sg_ws"], p["sg_bias"],
              dec["dmat"], dec["qdec"], dec["kdec"], p["w_br_ret"], p["w_br_sg"], p["w_br_x"], p["w_o"]]
    return pl.pallas_call(
        _mixer_kernel,
        grid=(b, l // t),
        in_specs=[_smem_spec(), tok(D_MODEL), tab, tab, tab, tab, mem, mem]
        + [_const_spec(c.shape) for c in consts],
        out_specs=[tok(D_MODEL), pl.BlockSpec((None, HEADS, DK, DV), lambda i, j: (i, 0, 0, 0))],
        out_shape=[jax.ShapeDtypeStruct((b, l, D_MODEL), F32),
                   jax.ShapeDtypeStruct((b, HEADS, DK, DV), F32)],
        scratch_shapes=[pltpu.VMEM((t, V_W), BF16), pltpu.VMEM((t, SG_W), BF16), pltpu.VMEM((t, X_W), BF16)],
        compiler_params=_params(("arbitrary", "arbitrary")),
        name="prompt_mixer",
    )(dec["gc"], x, *tabs, mk_b, mv_b, *consts)


def _ffn_kernel(x_ref, ng_ref, wup_ref, cw_ref, cb_ref, wdn_ref, nf_ref, y_ref, tail_ref, zs_ref):
    tile = x_ref.shape[0]
    first = pl.program_id(1) == 0

    @pl.when(first)
    def _():
        zs_ref[0:8, :] = jnp.zeros((8, 2 * D_FF), F32)

    @pl.when(jnp.logical_not(first))
    def _():
        zs_ref[0:8, :] = zs_ref[tile:tile + 8, :]

    x = x_ref[...]
    zs_ref[8:8 + tile, :] = _mm(_rms(x, ng_ref[...]), wup_ref[...])
    gate = _conv_gate(zs_ref, tile, 0, cw_ref[:, :D_FF], cb_ref[:, :D_FF], cw_ref[:, D_FF:], cb_ref[:, D_FF:])
    y = x + _mm(gate, wdn_ref[...])
    y_ref[...] = _rms(y, nf_ref[...])
    tail_ref[...] = zs_ref[tile:tile + 8, :]


def _prompt_ffn(x, p):
    b, l, _ = x.shape
    t = PROMPT_TILE
    tok = pl.BlockSpec((None, t, D_MODEL), lambda i, j: (i, j, 0))
    consts = [p["norm_ffn_g"], p["w_up"], p["conv_w"], p["conv_b"], p["w_down"], p["norm_final_g"]]
    return pl.pallas_call(
        _ffn_kernel,
        grid=(b, l // t),
        in_specs=[tok] + [_const_spec(c.shape) for c in consts],
        out_specs=[tok, pl.BlockSpec((None, 8, 2 * D_FF), lambda i, j: (i, 0, 0))],
        out_shape=[jax.ShapeDtypeStruct((b, l, D_MODEL), F32), jax.ShapeDtypeStruct((b, 8, 2 * D_FF), F32)],
        scratch_shapes=[pltpu.VMEM((t + 8, 2 * D_FF), F32)],
        compiler_params=_params(("arbitrary", "arbitrary")),
        name="prompt_ffn",
    )(x, *consts)


def _row_head(n):
    return (lax.broadcasted_iota(jnp.int32, (n, 1), 0) >> 2) & (HEADS - 1)


def _sample_pre_kernel(x_ref, cq_ref, sq_ref, ck_ref, sk_ref, kdec_ref, ng_ref, win_ref, bg_ref, lng_ref,
                       ws_ref, sgb_ref,
                       qe_ref, kre_ref, kde_ref, ve_ref, ge_ref, xqe_ref, osg_ref, vrows_ref, gates_ref):
    n = x_ref.shape[0]
    ne = n * EXP
    h = _rms(x_ref[...], ng_ref[...]).astype(BF16)
    r = lax.broadcasted_iota(jnp.int32, (ne, n), 0)
    c = lax.broadcasted_iota(jnp.int32, (ne, n), 1)
    rep = jnp.where(c == ((r >> 4) << 2) + (r & 3), 1.0, 0.0).astype(BF16)
    he = jnp.dot(rep, h, preferred_element_type=F32).astype(BF16)
    hh = _row_head(ne)

    def proj(hm, a, b):
        return jnp.dot(hm, win_ref[:, a:b], preferred_element_type=F32)

    q, k = proj(he, C_Q, C_K), proj(he, C_K, C_V)
    xq = proj(he, C_XQ, C_GT)
    cq, sq, ck, sk, kdec = cq_ref[...], sq_ref[...], ck_ref[...], sk_ref[...], kdec_ref[...]
    for hd in range(HEADS):
        cols = slice(hd * DK, (hd + 1) * DK)
        own = hh == hd
        kr = _rope(k[:, cols], ck, sk)
        qe_ref[:, cols] = jnp.where(own, _rope(q[:, cols], cq, sq), 0.0).astype(BF16)
        kre_ref[:, cols] = jnp.where(own, kr, 0.0).astype(BF16)
        kde_ref[:, cols] = jnp.where(own, kr * kdec, 0.0).astype(BF16)
    v = proj(he, C_V, C_G)
    g = proj(he, C_G, C_SU)
    ve = jnp.zeros((ne, DV), F32)
    ge = jnp.zeros((ne, DV), F32)
    xqc = jnp.zeros((ne, DK), F32)
    for hd in range(HEADS):
        cols = slice(hd * DV, (hd + 1) * DV)
        own = hh == hd
        ve = jnp.where(own, v[:, cols], ve)
        ge = jnp.where(own, g[:, cols], ge)
        xqc = jnp.where(own, xq[:, hd * DK:(hd + 1) * DK], xqc)
    xqe_ref[...] = xqc.astype(BF16)
    ve_ref[...] = ve.astype(BF16)
    ge_ref[...] = ge * _sigmoid(ge)

    u = _gelu(proj(h, C_SU, C_SV))
    vn = _stdnorm(_gelu(proj(h, C_SV, C_XQ))) * lng_ref[...]
    vrows_ref[...] = vn
    _sgu(u, vn, ws_ref, sgb_ref, osg_ref)
    gates_ref[...] = _sigmoid(proj(h, C_GT, C_END) + bg_ref[...])


def _sample_pre(x, tabs, kdec_rows, p, ws_k, sgb_k):
    n = x.shape[0]
    t = CHUNK
    te = t * EXP
    consts = [p["norm_mix_g"], p["w_in"], p["b_gate"], p["sg_ln_g"], ws_k, sgb_k]
    rowblk = lambda rows, w: pl.BlockSpec((rows, w), lambda i: (i, 0))
    outs = [(te, QK_W, BF16), (te, QK_W, BF16), (te, QK_W, BF16), (te, DV, BF16), (te, DV, F32),
            (te, DK, BF16), (t, SG_W, BF16), (t, SG_W, F32), (t, 3 * D_MODEL, F32)]
    return pl.pallas_call(
        _sample_pre_kernel,
        grid=(n // t,),
        in_specs=[rowblk(t, D_MODEL)] + [_const_spec((te, DK))] * 5 + [_const_spec(c.shape) for c in consts],
        out_specs=[rowblk(r, w) for r, w, _ in outs],
        out_shape=[jax.ShapeDtypeStruct((r * (n // t), w), d) for r, w, d in outs],
        compiler_params=_params(("arbitrary",)),
        name="sample_pre",
    )(x, *tabs, kdec_rows, *consts)


def _sample_state_kernel(gc_ref, qe_ref, kre_ref, kde_ref, ve_ref, ge_ref, xqe_ref, s_ref, mk_ref, mv_ref,
                         dmat_ref, qdec_ref, gn_ref,
                         oret_ref, ox_ref, so_ref, cross_ref, oxs_ref):
    rows_n = qe_ref.shape[0]
    per = HEADS * EXP
    q, ve, xq = qe_ref[...], ve_ref[...], xqe_ref[...]
    sc = _mm_nt(q, kre_ref[...]) * dmat_ref[...]
    inner = jnp.dot(sc.astype(BF16), ve, preferred_element_type=F32)
    kd = kde_ref[...]
    rowb = lax.broadcasted_iota(jnp.int32, (rows_n, 1), 0) >> 4
    own_col = (lax.broadcasted_iota(jnp.int32, (per, MEM_LEN * HEADS), 1) & (HEADS - 1)) == _row_head(per)
    for b in range(rows_n // per):
        rows = slice(b * per, (b + 1) * per)
        state = s_ref[b]
        cross_ref[rows, :] = jnp.dot(q[rows], state.astype(BF16), preferred_element_type=F32)
        upd = _mm_tn(kd, jnp.where(rowb == b, ve, jnp.zeros_like(ve)))
        for hd in range(HEADS):
            hr = slice(hd * DK, (hd + 1) * DK)
            so_ref[b, hr, :] = gc_ref[hd] * state[hr] + upd[hr]
        sx = _mm_nt(xq[rows], mk_ref[b]) * np.float32(DK ** -0.5)
        p = _softmax(jnp.where(own_col, sx, np.float32(-1e30)))
        oxs_ref[rows, :] = _mm(p, mv_ref[b])
    o = inner + cross_ref[...] * qdec_ref[...]
    og = ge_ref[...] * (_stdnorm(o) * gn_ref[...])
    oxs = oxs_ref[...]
    hh = _row_head(rows_n)
    for hd in range(HEADS):
        oret_ref[:, hd * DV:(hd + 1) * DV] = jnp.where(hh == hd, og, 0.0)
        ox_ref[:, hd * DK:(hd + 1) * DK] = jnp.where(hh == hd, oxs, 0.0)


def _sample_state(pre, state, mk, mv, dec):
    qe, kre, kde, ve, ge, xqe = pre
    nb = state.shape[0]
    bb = SAMPLE_BB
    rb = bb * HEADS * EXP
    rowblk = lambda w: pl.BlockSpec((rb, w), lambda i: (i, 0))
    batblk = lambda a, c: pl.BlockSpec((bb, a, c), lambda i: (i, 0, 0))
    ne = qe.shape[0]
    return pl.pallas_call(
        _sample_state_kernel,
        grid=(nb // bb,),
        in_specs=[_smem_spec(), rowblk(QK_W), rowblk(QK_W), rowblk(QK_W), rowblk(DV), rowblk(DV), rowblk(DK),
                  batblk(HEADS * DK, DV), batblk(MEM_LEN * HEADS, DK), batblk(MEM_LEN * HEADS, DK),
                  _const_spec((rb, rb)), _const_spec((rb, DV)), _const_spec((rb, DV))],
        out_specs=[rowblk(V_W), rowblk(X_W), batblk(HEADS * DK, DV)],
        out_shape=[jax.ShapeDtypeStruct((ne, V_W), F32), jax.ShapeDtypeStruct((ne, X_W), F32),
                   jax.ShapeDtypeStruct(state.shape, F32)],
        scratch_shapes=[pltpu.VMEM((rb, DV), F32), pltpu.VMEM((rb, DK), F32)],
        compiler_params=_params(("arbitrary",)),
        name="sample_state",
    )(dec["gc"], qe, kre, kde, ve, ge, xqe, state, mk, mv, dec["dmat"], dec["qdec"], dec["gn"])


def _sample_merge_kernel(x_ref, orete_ref, oxe_ref, osg_ref, gates_ref, wr_ref, wsg_ref, wx_ref, wo_ref, xo_ref):
    n = x_ref.shape[0]
    ne = n * EXP
    r = lax.broadcasted_iota(jnp.int32, (n, ne), 0)
    c = lax.broadcasted_iota(jnp.int32, (n, ne), 1)
    col = jnp.where(r == ((c >> 4) << 2) + (c & 3), 1.0, 0.0).astype(BF16)
    oret = jnp.dot(col, orete_ref[...].astype(BF16), preferred_element_type=F32)
    ox = jnp.dot(col, oxe_ref[...].astype(BF16), preferred_element_type=F32)
    gates = gates_ref[...]
    merged = (gates[:, :D_MODEL] * _mm(oret, wr_ref[...])
              + gates[:, D_MODEL:2 * D_MODEL] * jnp.dot(osg_ref[...], wsg_ref[...], preferred_element_type=F32)
              + gates[:, 2 * D_MODEL:] * _mm(ox, wx_ref[...]))
    xo_ref[...] = x_ref[...] + _mm(merged, wo_ref[...])


def _sample_merge(x, orete, oxe, osg, gates, p):
    ins = [x, orete, oxe, osg, gates, p["w_br_ret"], p["w_br_sg"], p["w_br_x"], p["w_o"]]
    return pl.pallas_call(
        _sample_merge_kernel,
        grid=(1,),
        in_specs=[_const_spec(a.shape) for a in ins],
        out_specs=pl.BlockSpec(x.shape, lambda i: (0, 0)),
        out_shape=jax.ShapeDtypeStruct(x.shape, F32),
        compiler_params=_params(("arbitrary",)),
        name="sample_merge",
    )(*ins)


def _sample_ffn_kernel(x_ref, ng_ref, wa_ref, wb_ref, cwa_ref, cwb_ref, cba_ref, cbb_ref, p1a_ref, p1b_ref,
                       p2a_ref, p2b_ref, wdn_ref, nf_ref,
                       y_ref, za_ref, zb_ref, h_ref, acc_ref, zs_ref):
    n = x_ref.shape[0]
    cw = wa_ref.shape[1]
    j = pl.program_id(0)

    @pl.when(j == 0)
    def _():
        h_ref[...] = _rms(x_ref[...], ng_ref[...]).astype(BF16)
        acc_ref[...] = jnp.zeros_like(acc_ref)
        zs_ref[0:8, :] = jnp.zeros((8, 2 * cw), F32)

    h = h_ref[...]
    za = jnp.dot(h, wa_ref[...], preferred_element_type=F32)
    zb = jnp.dot(h, wb_ref[...], preferred_element_type=F32)
    za_ref[...] = za
    zb_ref[...] = zb
    zs_ref[8:8 + n, :cw] = za
    zs_ref[8:8 + n, cw:] = zb
    pos = lax.broadcasted_iota(jnp.int32, (n, 1), 0) & 3
    carry = ((p1a_ref, p2a_ref), (p1b_ref, p2b_ref))

    def fix(z1, z2, k):
        return jnp.where(pos >= 1, z1, carry[k][0][...]), jnp.where(pos >= 2, z2, carry[k][1][...])

    gate = _conv_gate(zs_ref, n, 0, cwa_ref[...], cba_ref[...], cwb_ref[...], cbb_ref[...], fix)
    acc_ref[...] += _mm(gate, wdn_ref[...])

    @pl.when(j == pl.num_programs(0) - 1)
    def _():
        y_ref[...] = _rms(x_ref[...] + acc_ref[...], nf_ref[...])


def _sample_ffn(x, p1, p2, p):
    n = x.shape[0]
    cw = FFN_CW
    nch = D_FF // cw
    ca = lambda rows: pl.BlockSpec((rows, cw), lambda j: (0, j))
    cb = lambda rows: pl.BlockSpec((rows, cw), lambda j: (0, nch + j))
    full = pl.BlockSpec((n, D_MODEL), lambda j: (0, 0))
    vec = pl.BlockSpec((1, D_MODEL), lambda j: (0, 0))
    return pl.pallas_call(
        _sample_ffn_kernel,
        grid=(nch,),
        in_specs=[full, vec, ca(D_MODEL), cb(D_MODEL), ca(3), cb(3), ca(1), cb(1), ca(n), cb(n), ca(n), cb(n),
                  pl.BlockSpec((cw, D_MODEL), lambda j: (j, 0)), vec],
        out_specs=[full, pl.BlockSpec((n, cw), lambda j: (0, j)), pl.BlockSpec((n, cw), lambda j: (0, j))],
        out_shape=[jax.ShapeDtypeStruct((n, D_MODEL), F32), jax.ShapeDtypeStruct((n, D_FF), F32),
                   jax.ShapeDtypeStruct((n, D_FF), F32)],
        scratch_shapes=[pltpu.VMEM((n, D_MODEL), BF16), pltpu.VMEM((n, D_MODEL), F32),
                        pltpu.VMEM((n + 8, 2 * cw), F32)],
        compiler_params=_params(("arbitrary",)),
        name="sample_ffn",
    )(x, p["norm_ffn_g"], p["w_up"], p["w_up"], p["conv_w"], p["conv_w"], p["conv_b"], p["conv_b"],
      p1, p1, p2, p2, p["w_down"], p["norm_final_g"])


def _rope_tables(pos, scale):
    inv = ROPE_BASE ** (-np.arange(0, DK, 2, dtype=np.float64) / DK)
    ang = np.asarray(pos, np.float64)[:, None] * inv[None, :]
    cos, sin = np.cos(ang), np.sin(ang)
    return (np.concatenate([cos, cos], -1) * scale).astype(np.float32), \
        (np.concatenate([-sin, sin], -1) * scale).astype(np.float32)


def _decay(chunk):
    lg = np.log1p(-np.exp2(-5.0 - np.arange(HEADS, dtype=np.float64)))
    n = np.arange(chunk, dtype=np.float64)
    diff = n[:, None] - n[None, :]
    dmat = np.where(diff >= 0, np.exp(np.maximum(diff, 0.0)[None] * lg[:, None, None]), 0.0)
    qdec = np.exp((n + 1.0)[None, :] * lg[:, None])
    kdec = np.exp((chunk - 1.0 - n)[None, :] * lg[:, None])
    f32 = lambda a: a.astype(np.float32)
    return f32(dmat), f32(qdec), f32(kdec), f32(np.exp(chunk * lg))


def kernel(x_prompt, x_sample, mem_prompt, state_ret, state_conv, cache_mem_k, cache_mem_v, norm_mix_g, w_in,
           b_gate, ret_gn_g, sg_ln_g, sg_ws, sg_bs, mem_norm_g, w_mem_kv, w_br_ret, w_br_sg, w_br_x, w_o,
           norm_ffn_g, w_up, conv_w, conv_b, w_down, norm_final_g):
    bp, lp, _ = x_prompt.shape
    bs, ls, _ = x_sample.shape
    assert state_ret.shape[0] == 1 and ls == EXP and lp % PROMPT_TILE == 0 and bs % SAMPLE_BB == 0
    row = lambda a: a.reshape(1, -1)
    p = dict(norm_mix_g=row(norm_mix_g[0]), w_in=w_in[0].astype(BF16), b_gate=row(b_gate[0]),
             gn_g=row(ret_gn_g[0]), sg_ln_g=row(sg_ln_g[0]),
             w_br_ret=w_br_ret[0].astype(BF16), w_br_sg=w_br_sg[0].astype(BF16),
             w_br_x=w_br_x[0].astype(BF16), w_o=w_o[0].astype(BF16),
             norm_ffn_g=row(norm_ffn_g[0]), w_up=w_up[0].astype(BF16), conv_w=conv_w[0],
             conv_b=row(conv_b[0]), w_down=w_down[0].astype(BF16), norm_final_g=row(norm_final_g))
    scale = DK ** -0.5

    mk, mv, mk_b, mv_b = _memkv(mem_prompt, row(mem_norm_g[0]), w_mem_kv[0].astype(BF16))
    pos_p = np.arange(lp)
    tabs_p = (*_rope_tables(pos_p, 1.0), *_rope_tables(pos_p, scale))
    dmat, qdec, kdec, gc = _decay(CHUNK)
    bcast = lambda a: np.ascontiguousarray(np.broadcast_to(a[:, :, None], (HEADS, CHUNK, DK)))
    dec_p = dict(dmat=dmat, qdec=bcast(qdec), kdec=bcast(kdec), gc=gc)
    pp = dict(p, sg_ws=sg_ws[0], sg_bias=jnp.repeat(sg_bs[0].T, DK, axis=1))
    x_mid, s_prompt = _prompt_mixer(x_prompt, tabs_p, mk_b, mv_b, dec_p, pp)
    y_prompt, tail = _prompt_ffn(x_mid, p)

    n = bs * ls
    per = HEADS * EXP
    te = CHUNK * EXP
    pos_s = PAST_LEN + (np.arange(te) & (ls - 1))
    tabs_s = (*_rope_tables(pos_s, 1.0), *_rope_tables(pos_s, scale))
    dmat4, qdec4, kdec4, gc4 = _decay(ls)
    kdec_rows = np.ascontiguousarray(np.broadcast_to(np.tile(kdec4.reshape(per), te // per)[:, None], (te, DK)))
    eye = jnp.eye(CHUNK // ls, dtype=F32)
    ws_k = jnp.einsum("ab,gts->gatbs", eye, sg_ws[0][:, :ls, :ls]).reshape(HEADS, CHUNK, CHUNK)
    sgb_k = jnp.repeat(jnp.tile(sg_bs[0][:, :ls], (1, CHUNK // ls)).T, DK, axis=1)
    pre = _sample_pre(x_sample.reshape(n, D_MODEL), tabs_s, kdec_rows, p, ws_k, sgb_k)
    qe, kre, kde, ve, ge, xqe, osg, vrows, gates = pre

    rb = SAMPLE_BB * per
    blk16 = np.einsum("hk,hls->hlks", np.eye(HEADS, dtype=np.float32), dmat4).reshape(per, per)
    dec_s = dict(gc=gc4,
                 dmat=np.kron(np.eye(SAMPLE_BB, dtype=np.float32), blk16),
                 qdec=np.ascontiguousarray(
                     np.broadcast_to(np.tile(qdec4.reshape(per), SAMPLE_BB)[:, None], (rb, DV))),
                 gn=jnp.tile(jnp.repeat(ret_gn_g[0], ls, axis=0), (SAMPLE_BB, 1)))
    orete, oxe, s_sample = _sample_state(
        (qe, kre, kde, ve, ge, xqe), state_ret[0].reshape(bs, HEADS * DK, DV),
        cache_mem_k[0].reshape(bs, MEM_LEN * HEADS, DK), cache_mem_v[0].reshape(bs, MEM_LEN * HEADS, DK), dec_s)
    xs_mid = _sample_merge(x_sample.reshape(n, D_MODEL), orete, oxe, osg, gates, p)
    zero = jnp.zeros((bs, 1, 2 * D_FF), F32)
    sc = state_conv[0]
    p1 = jnp.concatenate([sc[:, 1:2], zero, zero, zero], axis=1).reshape(n, 2 * D_FF)
    p2 = jnp.concatenate([sc, zero, zero], axis=1).reshape(n, 2 * D_FF)
    y_sample, za, zb = _sample_ffn(xs_mid, p1, p2, p)
    z_s = jnp.concatenate([za, zb], axis=-1).reshape(bs, ls, 2 * D_FF)

    return (y_prompt, y_sample.reshape(bs, ls, D_MODEL),
            s_prompt[None], tail[None, :, 6:8],
            mk.reshape(1, bp, MEM_LEN, HEADS, DK), mv.reshape(1, bp, MEM_LEN, HEADS, DK),
            s_sample.reshape(1, bs, HEADS, DK, DV), z_s[None, :, ls - 2:],
            vrows.reshape(1, bs, ls, SG_W))
```

```python
import functools

import numpy as np
import jax
import jax.numpy as jnp
from jax import lax
from jax.experimental import pallas as pl
from jax.experimental.pallas import tpu as pltpu

F32 = jnp.float32
BF16 = jnp.bfloat16

D_MODEL = 1024
HEADS = 4
DK = 128
DV = 256
QK_W = HEADS * DK
V_W = HEADS * DV
SG_W = 512
X_W = 512
MEM_LEN = 256
D_FF = 2816
CHUNK = 128
ROPE_BASE = 10000.0
EPS = 1e-6
PAST_LEN = 16384

C_Q, C_K, C_V, C_G, C_SU, C_SV, C_XQ, C_GT, C_END = 0, 512, 1024, 2048, 3072, 3584, 4096, 4608, 7680

PROMPT_TILE = 512
SAMPLE_BB = 8
EXP = HEADS
FFN_CW = 256
VMEM_LIMIT = 56 * 1024 * 1024


def _rms(x, g):
    return x * lax.rsqrt(jnp.mean(x * x, axis=-1, keepdims=True) + EPS) * g


def _stdnorm(x):
    mu = jnp.mean(x, axis=-1, keepdims=True)
    xc = x - mu
    var = jnp.mean(xc * xc, axis=-1, keepdims=True)
    return xc * lax.rsqrt(var + EPS)


def _gelu(x):
    return 0.5 * x * (1.0 + jnp.tanh(np.float32(np.sqrt(2.0 / np.pi)) * (x + 0.044715 * (x * x * x))))


def _sigmoid(x):
    return 1.0 / (1.0 + jnp.exp(-x))


def _softmax(s):
    e = jnp.exp(s - jnp.max(s, axis=-1, keepdims=True))
    return e / jnp.sum(e, axis=-1, keepdims=True)


def _mm(a, b):
    return jnp.dot(a.astype(BF16), b.astype(BF16), preferred_element_type=F32)


def _mm_nt(a, b):
    return lax.dot_general(a.astype(BF16), b.astype(BF16), (((1,), (1,)), ((), ())),
                           preferred_element_type=F32)


def _mm_tn(a, b):
    return lax.dot_general(a.astype(BF16), b.astype(BF16), (((0,), (0,)), ((), ())),
                           preferred_element_type=F32)


def _rope(x, cos, sin):
    return x * cos + pltpu.roll(x, DK // 2, 1) * sin


def _tril(w):
    r = lax.broadcasted_iota(jnp.int32, w.shape, 0)
    c = lax.broadcasted_iota(jnp.int32, w.shape, 1)
    return jnp.where(r >= c, w, 0.0)


def _sgu(u, vn, ws_ref, sgb_ref, out_ref):
    for g in range(HEADS):
        w = _tril(ws_ref[g]).astype(BF16)
        cols = slice(g * DK, (g + 1) * DK)
        for c in range(u.shape[0] // CHUNK):
            rows = slice(c * CHUNK, (c + 1) * CHUNK)
            mixed = jnp.dot(w, vn[rows, cols].astype(BF16), preferred_element_type=F32) + sgb_ref[:, cols]
            out_ref[rows, cols] = (u[rows, cols] * mixed).astype(out_ref.dtype)


def _conv_gate(zs_ref, n, cols_a, cols_b, cw_a, cb_a, cw_b, cb_b, fix=None):
    def conv(cols, cw, cb, k):
        z0 = zs_ref[8:8 + n, cols]
        z1 = zs_ref[7:7 + n, cols]
        z2 = zs_ref[6:6 + n, cols]
        if fix is not None:
            z1, z2 = fix(z1, z2, k)
        return cb + cw[0:1] * z2 + cw[1:2] * z1 + cw[2:3] * z0
    return _gelu(conv(cols_a, cw_a, cb_a, 0)) * conv(cols_b, cw_b, cb_b, 1)


def _const_spec(shape):
    nd = len(shape)
    return pl.BlockSpec(shape, lambda *_: (0,) * nd, pipeline_mode=pl.Buffered(1))


def _smem_spec():
    return pl.BlockSpec(memory_space=pltpu.SMEM)


def _params(sem):
    return pltpu.CompilerParams(dimension_semantics=sem, vmem_limit_bytes=VMEM_LIMIT)


def _memkv_kernel(mem_ref, g_ref, w_ref, k_ref, v_ref, kb_ref, vb_ref):
    kv = _mm(_rms(mem_ref[...], g_ref[...]), w_ref[...])
    k, v = kv[:, :X_W], kv[:, X_W:]
    for hd in range(HEADS):
        cols = slice(hd * DK, (hd + 1) * DK)
        k_ref[pl.ds(hd, MEM_LEN, stride=HEADS), :] = k[:, cols]
        v_ref[pl.ds(hd, MEM_LEN, stride=HEADS), :] = v[:, cols]
    kb_ref[...] = k.astype(BF16)
    vb_ref[...] = v.astype(BF16)


def _memkv(mem, g, w):
    b = mem.shape[0]
    blk = pl.BlockSpec((None, MEM_LEN, X_W), lambda i: (i, 0, 0))
    flat = pl.BlockSpec((None, MEM_LEN * HEADS, DK), lambda i: (i, 0, 0))
    return pl.pallas_call(
        _memkv_kernel,
        grid=(b,),
        in_specs=[pl.BlockSpec((None, MEM_LEN, D_MODEL), lambda i: (i, 0, 0)),
                  _const_spec((1, D_MODEL)), _const_spec((D_MODEL, 2 * X_W))],
        out_specs=[flat, flat, blk, blk],
        out_shape=[jax.ShapeDtypeStruct((b, MEM_LEN * HEADS, DK), F32)] * 2
        + [jax.ShapeDtypeStruct((b, MEM_LEN, X_W), BF16)] * 2,
        compiler_params=_params(("arbitrary",)),
        name="mem_kv",
    )(mem, g, w)


def _mixer_kernel(gc_ref, x_ref, cq_ref, sq_ref, ck_ref, sk_ref, mk_ref, mv_ref, ng_ref, win_ref, bg_ref,
                  gn_ref, lng_ref, ws_ref, sgb_ref, dmat_ref, qdec_ref, kdec_ref,
                  wr_ref, wsg_ref, wx_ref, wo_ref,
                  xo_ref, s_ref, oret_ref, osg_ref, ox_ref):
    tile = x_ref.shape[0]

    @pl.when(pl.program_id(1) == 0)
    def _():
        s_ref[...] = jnp.zeros_like(s_ref)

    x = x_ref[...]
    h = _rms(x, ng_ref[...]).astype(BF16)

    def proj(a, b):
        return jnp.dot(h, win_ref[:, a:b], preferred_element_type=F32)

    q = proj(C_Q, C_K)
    k = proj(C_K, C_V)
    v = proj(C_V, C_G)
    gsil = proj(C_G, C_SU)
    gsil = gsil * _sigmoid(gsil)
    cq, sq, ck, sk = cq_ref[...], sq_ref[...], ck_ref[...], sk_ref[...]
    for hd in range(HEADS):
        kcols = slice(hd * DK, (hd + 1) * DK)
        vcols = slice(hd * DV, (hd + 1) * DV)
        qr = _rope(q[:, kcols], cq, sq)
        kr = _rope(k[:, kcols], ck, sk)
        for c in range(tile // CHUNK):
            rows = slice(c * CHUNK, (c + 1) * CHUNK)
            qc, kc, vc = qr[rows], kr[rows], v[rows, vcols].astype(BF16)
            state = s_ref[hd]
            sc = _mm_nt(qc, kc) * dmat_ref[hd]
            lhs = jnp.concatenate([sc.astype(BF16), (qc * qdec_ref[hd]).astype(BF16)], axis=1)
            rhs = jnp.concatenate([vc, state.astype(BF16)], axis=0)
            o = jnp.dot(lhs, rhs, preferred_element_type=F32)
            s_ref[hd] = gc_ref[hd] * state + _mm_tn(kc * kdec_ref[hd], vc)
            on = _stdnorm(o) * gn_ref[:, vcols]
            oret_ref[rows, vcols] = (gsil[rows, vcols] * on).astype(BF16)

    u = _gelu(proj(C_SU, C_SV))
    vn = _stdnorm(_gelu(proj(C_SV, C_XQ))) * lng_ref[...]
    _sgu(u, vn, ws_ref, sgb_ref, osg_ref)

    xq = proj(C_XQ, C_GT)
    for hd in range(HEADS):
        cols = slice(hd * DK, (hd + 1) * DK)
        p = _softmax(_mm_nt(xq[:, cols], mk_ref[:, cols]) * np.float32(DK ** -0.5))
        ox_ref[:, cols] = _mm(p, mv_ref[:, cols]).astype(BF16)

    gates = _sigmoid(proj(C_GT, C_END) + bg_ref[...])
    merged = (gates[:, :D_MODEL] * jnp.dot(oret_ref[...], wr_ref[...], preferred_element_type=F32)
              + gates[:, D_MODEL:2 * D_MODEL] * jnp.dot(osg_ref[...], wsg_ref[...], preferred_element_type=F32)
              + gates[:, 2 * D_MODEL:] * jnp.dot(ox_ref[...], wx_ref[...], preferred_element_type=F32))
    xo_ref[...] = x + _mm(merged, wo_ref[...])


def _prompt_mixer(x, tabs, mk_b, mv_b, dec, p):
    b, l, _ = x.shape
    t = PROMPT_TILE
    tok = lambda w: pl.BlockSpec((None, t, w), lambda i, j: (i, j, 0))
    tab = pl.BlockSpec((t, DK), lambda i, j: (j, 0))
    mem = pl.BlockSpec((None, MEM_LEN, X_W), lambda i, j: (i, 0, 0))
    consts = [p["norm_mix_g"], p["w_in"], p["b_gate"], p["gn_g"], p["sg_ln_g"], p["sg_ws"], p["sg_bias"],
              dec["dmat"], dec["qdec"], dec["kdec"], p["w_br_ret"], p["w_br_sg"], p["w_br_x"], p["w_o"]]
    return pl.pallas_call(
        _mixer_kernel,
        grid=(b, l // t),
        in_specs=[_smem_spec(), tok(D_MODEL), tab, tab, tab, tab, mem, mem]
        + [_const_spec(c.shape) for c in consts],
        out_specs=[tok(D_MODEL), pl.BlockSpec((None, HEADS, DK, DV), lambda i, j: (i, 0, 0, 0))],
        out_shape=[jax.ShapeDtypeStruct((b, l, D_MODEL), F32),
                   jax.ShapeDtypeStruct((b, HEADS, DK, DV), F32)],
        scratch_shapes=[pltpu.VMEM((t, V_W), BF16), pltpu.VMEM((t, SG_W), BF16), pltpu.VMEM((t, X_W), BF16)],
        compiler_params=_params(("arbitrary", "arbitrary")),
        name="prompt_mixer",
    )(dec["gc"], x, *tabs, mk_b, mv_b, *consts)


def _ffn_kernel(x_ref, ng_ref, wup_ref, cw_ref, cb_ref, wdn_ref, nf_ref, y_ref, tail_ref, zs_ref):
    tile = x_ref.shape[0]
    first = pl.program_id(1) == 0

    @pl.when(first)
    def _():
        zs_ref[0:8, :] = jnp.zeros((8, 2 * D_FF), F32)

    @pl.when(jnp.logical_not(first))
    def _():
        zs_ref[0:8, :] = zs_ref[tile:tile + 8, :]

    x = x_ref[...]
    zs_ref[8:8 + tile, :] = _mm(_rms(x, ng_ref[...]), wup_ref[...])
    ca, cb = slice(0, D_FF), slice(D_FF, 2 * D_FF)
    gate = _conv_gate(zs_ref, tile, ca, cb, cw_ref[:, ca], cb_ref[:, ca], cw_ref[:, cb], cb_ref[:, cb])
    y = x + _mm(gate, wdn_ref[...])
    y_ref[...] = _rms(y, nf_ref[...])
    tail_ref[...] = zs_ref[tile:tile + 8, :]


def _prompt_ffn(x, p):
    b, l, _ = x.shape
    t = PROMPT_TILE
    tok = pl.BlockSpec((None, t, D_MODEL), lambda i, j: (i, j, 0))
    consts = [p["norm_ffn_g"], p["w_up"], p["conv_w"], p["conv_b"], p["w_down"], p["norm_final_g"]]
    return pl.pallas_call(
        _ffn_kernel,
        grid=(b, l // t),
        in_specs=[tok] + [_const_spec(c.shape) for c in consts],
        out_specs=[tok, pl.BlockSpec((None, 8, 2 * D_FF), lambda i, j: (i, 0, 0))],
        out_shape=[jax.ShapeDtypeStruct((b, l, D_MODEL), F32), jax.ShapeDtypeStruct((b, 8, 2 * D_FF), F32)],
        scratch_shapes=[pltpu.VMEM((t + 8, 2 * D_FF), F32)],
        compiler_params=_params(("arbitrary", "arbitrary")),
        name="prompt_ffn",
    )(x, *consts)


def _row_head(n):
    return (lax.broadcasted_iota(jnp.int32, (n, 1), 0) >> 2) & (HEADS - 1)


def _sample_pre_kernel(ws_ref, bs_ref, x_ref, cq_ref, sq_ref, ck_ref, sk_ref, kdec_ref, ng_ref, win_ref, bg_ref,
                       lng_ref,
                       qe_ref, kre_ref, kde_ref, ve_ref, ge_ref, xqe_ref, osg_ref, vrows_ref, gates_ref,
                       h_ref, vn_ref):
    n = x_ref.shape[0]
    nb = n // EXP
    ne = qe_ref.shape[0]
    i = pl.program_id(0)

    @pl.when(i == 0)
    def _():
        h_ref[...] = _rms(x_ref[...], ng_ref[...]).astype(BF16)
        vn_ref[...] = jnp.zeros_like(vn_ref)

    r = lax.broadcasted_iota(jnp.int32, (ne, n), 0)
    c = lax.broadcasted_iota(jnp.int32, (ne, n), 1)
    rep = jnp.where(c == (r & (EXP - 1)) * nb + i * (ne // (HEADS * EXP)) + (r >> 4), 1.0, 0.0).astype(BF16)
    he = jnp.dot(rep, h_ref[...], preferred_element_type=F32).astype(BF16)
    hh = _row_head(ne)
    h = h_ref[pl.ds(pl.multiple_of(i * nb, nb), nb), :]

    def proj(hm, a, b):
        return jnp.dot(hm, win_ref[:, a:b], preferred_element_type=F32)

    q, k = proj(he, C_Q, C_K), proj(he, C_K, C_V)
    xq = proj(he, C_XQ, C_GT)
    cq, sq, ck, sk, kdec = cq_ref[...], sq_ref[...], ck_ref[...], sk_ref[...], kdec_ref[...]
    for hd in range(HEADS):
        cols = slice(hd * DK, (hd + 1) * DK)
        own = hh == hd
        kr = _rope(k[:, cols], ck, sk)
        qe_ref[:, cols] = jnp.where(own, _rope(q[:, cols], cq, sq), 0.0).astype(BF16)
        kre_ref[:, cols] = jnp.where(own, kr, 0.0).astype(BF16)
        kde_ref[:, cols] = jnp.where(own, kr * kdec, 0.0).astype(BF16)
    v = proj(he, C_V, C_G)
    g = proj(he, C_G, C_SU)
    ve = jnp.zeros((ne, DV), F32)
    ge = jnp.zeros((ne, DV), F32)
    xqc = jnp.zeros((ne, DK), F32)
    for hd in range(HEADS):
        cols = slice(hd * DV, (hd + 1) * DV)
        own = hh == hd
        ve = jnp.where(own, v[:, cols], ve)
        ge = jnp.where(own, g[:, cols], ge)
        xqc = jnp.where(own, xq[:, hd * DK:(hd + 1) * DK], xqc)
    xqe_ref[...] = xqc.astype(BF16)
    ve_ref[...] = ve.astype(BF16)
    ge_ref[...] = ge * _sigmoid(ge)

    u = _gelu(proj(h, C_SU, C_SV))
    vn = _stdnorm(_gelu(proj(h, C_SV, C_XQ))) * lng_ref[...]
    vrows_ref[...] = vn
    vn_ref[i] = vn
    for g in range(HEADS):
        cols = slice(g * DK, (g + 1) * DK)
        mixed = jnp.full((nb, DK), bs_ref[g * EXP + i], F32)
        for s in range(EXP):
            w = jnp.where(s <= i, ws_ref[(g * EXP + i) * EXP + s], 0.0)
            mixed = mixed + w * vn_ref[s, :, cols]
        osg_ref[:, cols] = (u[:, cols] * mixed).astype(BF16)
    gates_ref[...] = _sigmoid(proj(h, C_GT, C_END) + bg_ref[...])


def _sample_pre(x, tabs, kdec_rows, p, ws4, bs4):
    n = x.shape[0]
    t = n // EXP
    te = (t // EXP) * HEADS * EXP
    consts = [p["norm_mix_g"], p["w_in"], p["b_gate"], p["sg_ln_g"]]
    rowblk = lambda rows, w: pl.BlockSpec((rows, w), lambda i: (i, 0))
    outs = [(te, QK_W, BF16), (te, QK_W, BF16), (te, QK_W, BF16), (te, DV, BF16), (te, DV, F32),
            (te, DK, BF16), (t, SG_W, BF16), (t, SG_W, F32), (t, 3 * D_MODEL, F32)]
    return pl.pallas_call(
        _sample_pre_kernel,
        grid=(EXP,),
        in_specs=[_smem_spec(), _smem_spec(), _const_spec(x.shape)] + [_const_spec((te, DK))] * 5
        + [_const_spec(c.shape) for c in consts],
        out_specs=[rowblk(r, w) for r, w, _ in outs],
        out_shape=[jax.ShapeDtypeStruct((r * EXP, w), d) for r, w, d in outs],
        scratch_shapes=[pltpu.VMEM((n, D_MODEL), BF16), pltpu.VMEM((EXP, t, SG_W), F32)],
        compiler_params=_params(("arbitrary",)),
        name="sample_pre",
    )(ws4, bs4, x, *tabs, kdec_rows, *consts)


def _sample_state_kernel(gc_ref, qe_ref, kre_ref, kde_ref, ve_ref, ge_ref, xqe_ref, s_ref, mk_ref, mv_ref,
                         dmat_ref, qdec_ref, gn_ref,
                         oret_ref, ox_ref, so_ref, cross_ref, oxs_ref):
    rows_n = qe_ref.shape[0]
    per = HEADS * EXP
    q, ve, xq = qe_ref[...], ve_ref[...], xqe_ref[...]
    sc = _mm_nt(q, kre_ref[...]) * dmat_ref[...]
    inner = jnp.dot(sc.astype(BF16), ve, preferred_element_type=F32)
    kd = kde_ref[...]
    rowb = lax.broadcasted_iota(jnp.int32, (rows_n, 1), 0) >> 4
    own_col = (lax.broadcasted_iota(jnp.int32, (per, MEM_LEN * HEADS), 1) & (HEADS - 1)) == _row_head(per)
    for b in range(rows_n // per):
        rows = slice(b * per, (b + 1) * per)
        state = s_ref[b]
        cross_ref[rows, :] = jnp.dot(q[rows], state.astype(BF16), preferred_element_type=F32)
        upd = _mm_tn(kd, jnp.where(rowb == b, ve, jnp.zeros_like(ve)))
        for hd in range(HEADS):
            hr = slice(hd * DK, (hd + 1) * DK)
            so_ref[b, hr, :] = gc_ref[hd] * state[hr] + upd[hr]
        sx = _mm_nt(xq[rows], mk_ref[b]) * np.float32(DK ** -0.5)
        p = _softmax(jnp.where(own_col, sx, np.float32(-1e30)))
        oxs_ref[rows, :] = _mm(p, mv_ref[b])
    o = inner + cross_ref[...] * qdec_ref[...]
    og = ge_ref[...] * (_stdnorm(o) * gn_ref[...])
    oxs = oxs_ref[...]
    hh = _row_head(rows_n)
    for hd in range(HEADS):
        oret_ref[:, hd * DV:(hd + 1) * DV] = jnp.where(hh == hd, og, 0.0)
        ox_ref[:, hd * DK:(hd + 1) * DK] = jnp.where(hh == hd, oxs, 0.0)


def _sample_state(pre, state, mk, mv, dec):
    qe, kre, kde, ve, ge, xqe = pre
    nb = state.shape[0]
    bb = SAMPLE_BB
    rb = bb * HEADS * EXP
    rowblk = lambda w: pl.BlockSpec((rb, w), lambda i: (i, 0))
    batblk = lambda a, c: pl.BlockSpec((bb, a, c), lambda i: (i, 0, 0))
    ne = qe.shape[0]
    return pl.pallas_call(
        _sample_state_kernel,
        grid=(nb // bb,),
        in_specs=[_smem_spec(), rowblk(QK_W), rowblk(QK_W), rowblk(QK_W), rowblk(DV), rowblk(DV), rowblk(DK),
                  batblk(HEADS * DK, DV), batblk(MEM_LEN * HEADS, DK), batblk(MEM_LEN * HEADS, DK),
                  _const_spec((rb, rb)), _const_spec((rb, DV)), _const_spec((rb, DV))],
        out_specs=[rowblk(V_W), rowblk(X_W), batblk(HEADS * DK, DV)],
        out_shape=[jax.ShapeDtypeStruct((ne, V_W), F32), jax.ShapeDtypeStruct((ne, X_W), F32),
                   jax.ShapeDtypeStruct(state.shape, F32)],
        scratch_shapes=[pltpu.VMEM((rb, DV), F32), pltpu.VMEM((rb, DK), F32)],
        compiler_params=_params(("arbitrary",)),
        name="sample_state",
    )(dec["gc"], qe, kre, kde, ve, ge, xqe, state, mk, mv, dec["dmat"], dec["qdec"], dec["gn"])


def _sample_merge_kernel(x_ref, orete_ref, oxe_ref, osg_ref, gates_ref, wr_ref, wsg_ref, wx_ref, wo_ref, xo_ref):
    n = x_ref.shape[0]
    ne = n * EXP
    r = lax.broadcasted_iota(jnp.int32, (n, ne), 0)
    c = lax.broadcasted_iota(jnp.int32, (n, ne), 1)
    col = jnp.where(r == (c & (EXP - 1)) * (n // EXP) + (c >> 4), 1.0, 0.0).astype(BF16)
    oret = jnp.dot(col, orete_ref[...].astype(BF16), preferred_element_type=F32)
    ox = jnp.dot(col, oxe_ref[...].astype(BF16), preferred_element_type=F32)
    gates = gates_ref[...]
    merged = (gates[:, :D_MODEL] * _mm(oret, wr_ref[...])
              + gates[:, D_MODEL:2 * D_MODEL] * jnp.dot(osg_ref[...], wsg_ref[...], preferred_element_type=F32)
              + gates[:, 2 * D_MODEL:] * _mm(ox, wx_ref[...]))
    xo_ref[...] = x_ref[...] + _mm(merged, wo_ref[...])


def _sample_merge(x, orete, oxe, osg, gates, p):
    ins = [x, orete, oxe, osg, gates, p["w_br_ret"], p["w_br_sg"], p["w_br_x"], p["w_o"]]
    return pl.pallas_call(
        _sample_merge_kernel,
        grid=(1,),
        in_specs=[_const_spec(a.shape) for a in ins],
        out_specs=pl.BlockSpec(x.shape, lambda i: (0, 0)),
        out_shape=jax.ShapeDtypeStruct(x.shape, F32),
        compiler_params=_params(("arbitrary",)),
        name="sample_merge",
    )(*ins)


def _sample_ffn_kernel(x_ref, ng_ref, wa_ref, wb_ref, cwa_ref, cwb_ref, cba_ref, cbb_ref, s0a_ref, s0b_ref,
                       s1a_ref, s1b_ref, wdn_ref, nf_ref,
                       y_ref, c2a_ref, c2b_ref, c3a_ref, c3b_ref, h_ref, acc_ref):
    nb = x_ref.shape[0] // EXP
    j = pl.program_id(0)

    @pl.when(j == 0)
    def _():
        h_ref[...] = _rms(x_ref[...], ng_ref[...]).astype(BF16)
        acc_ref[...] = jnp.zeros_like(acc_ref)

    h = h_ref[...]

    def conv(w_ref, cw_ref, cb_ref, s0_ref, s1_ref, c2_ref, c3_ref):
        z = jnp.dot(h, w_ref[...], preferred_element_type=F32)
        zp = [s0_ref[...], s1_ref[...]] + [z[l * nb:(l + 1) * nb] for l in range(EXP)]
        c2_ref[...] = zp[EXP]
        c3_ref[...] = zp[EXP + 1]
        cw, cb = cw_ref[...], cb_ref[...]
        return jnp.concatenate([cb + cw[0:1] * zp[l] + cw[1:2] * zp[l + 1] + cw[2:3] * zp[l + 2]
                                for l in range(EXP)], axis=0)

    a = conv(wa_ref, cwa_ref, cba_ref, s0a_ref, s1a_ref, c2a_ref, c3a_ref)
    b = conv(wb_ref, cwb_ref, cbb_ref, s0b_ref, s1b_ref, c2b_ref, c3b_ref)
    acc_ref[...] += _mm(_gelu(a) * b, wdn_ref[...])

    @pl.when(j == pl.num_programs(0) - 1)
    def _():
        y_ref[...] = _rms(x_ref[...] + acc_ref[...], nf_ref[...])


def _sample_ffn(x, sc0, sc1, p):
    n = x.shape[0]
    nb = n // EXP
    cw = FFN_CW
    nch = D_FF // cw
    ca = lambda rows: pl.BlockSpec((rows, cw), lambda j: (0, j))
    cb = lambda rows: pl.BlockSpec((rows, cw), lambda j: (0, nch + j))
    full = pl.BlockSpec((n, D_MODEL), lambda j: (0, 0))
    vec = pl.BlockSpec((1, D_MODEL), lambda j: (0, 0))
    return pl.pallas_call(
        _sample_ffn_kernel,
        grid=(nch,),
        in_specs=[full, vec, ca(D_MODEL), cb(D_MODEL), ca(3), cb(3), ca(1), cb(1), ca(nb), cb(nb), ca(nb), cb(nb),
                  pl.BlockSpec((cw, D_MODEL), lambda j: (j, 0)), vec],
        out_specs=[full] + [ca(nb)] * 4,
        out_shape=[jax.ShapeDtypeStruct((n, D_MODEL), F32)] + [jax.ShapeDtypeStruct((nb, D_FF), F32)] * 4,
        scratch_shapes=[pltpu.VMEM((n, D_MODEL), BF16), pltpu.VMEM((n, D_MODEL), F32)],
        compiler_params=_params(("arbitrary",)),
        name="sample_ffn",
    )(x, p["norm_ffn_g"], p["w_up"], p["w_up"], p["conv_w"], p["conv_w"], p["conv_b"], p["conv_b"],
      sc0, sc0, sc1, sc1, p["w_down"], p["norm_final_g"])


def _rope_tables(pos, scale):
    inv = ROPE_BASE ** (-np.arange(0, DK, 2, dtype=np.float64) / DK)
    ang = np.asarray(pos, np.float64)[:, None] * inv[None, :]
    cos, sin = np.cos(ang), np.sin(ang)
    return (np.concatenate([cos, cos], -1) * scale).astype(np.float32), \
        (np.concatenate([-sin, sin], -1) * scale).astype(np.float32)


def _decay(chunk):
    lg = np.log1p(-np.exp2(-5.0 - np.arange(HEADS, dtype=np.float64)))
    n = np.arange(chunk, dtype=np.float64)
    diff = n[:, None] - n[None, :]
    dmat = np.where(diff >= 0, np.exp(np.maximum(diff, 0.0)[None] * lg[:, None, None]), 0.0)
    qdec = np.exp((n + 1.0)[None, :] * lg[:, None])
    kdec = np.exp((chunk - 1.0 - n)[None, :] * lg[:, None])
    f32 = lambda a: a.astype(np.float32)
    return f32(dmat), f32(qdec), f32(kdec), f32(np.exp(chunk * lg))


def kernel(x_prompt, x_sample, mem_prompt, state_ret, state_conv, cache_mem_k, cache_mem_v, norm_mix_g, w_in,
           b_gate, ret_gn_g, sg_ln_g, sg_ws, sg_bs, mem_norm_g, w_mem_kv, w_br_ret, w_br_sg, w_br_x, w_o,
           norm_ffn_g, w_up, conv_w, conv_b, w_down, norm_final_g):
    bp, lp, _ = x_prompt.shape
    bs, ls, _ = x_sample.shape
    assert state_ret.shape[0] == 1 and ls == EXP and lp % PROMPT_TILE == 0 and bs % SAMPLE_BB == 0
    assert bs == CHUNK
    row = lambda a: a.reshape(1, -1)
    p = dict(norm_mix_g=row(norm_mix_g[0]), w_in=w_in[0].astype(BF16), b_gate=row(b_gate[0]),
             gn_g=row(ret_gn_g[0]), sg_ln_g=row(sg_ln_g[0]),
             w_br_ret=w_br_ret[0].astype(BF16), w_br_sg=w_br_sg[0].astype(BF16),
             w_br_x=w_br_x[0].astype(BF16), w_o=w_o[0].astype(BF16),
             norm_ffn_g=row(norm_ffn_g[0]), w_up=w_up[0].astype(BF16), conv_w=conv_w[0],
             conv_b=row(conv_b[0]), w_down=w_down[0].astype(BF16), norm_final_g=row(norm_final_g))
    scale = DK ** -0.5

    mk, mv, mk_b, mv_b = _memkv(mem_prompt, row(mem_norm_g[0]), w_mem_kv[0].astype(BF16))
    pos_p = np.arange(lp)
    tabs_p = (*_rope_tables(pos_p, 1.0), *_rope_tables(pos_p, scale))
    dmat, qdec, kdec, gc = _decay(CHUNK)
    bcast = lambda a: np.ascontiguousarray(np.broadcast_to(a[:, :, None], (HEADS, CHUNK, DK)))
    dec_p = dict(dmat=dmat, qdec=bcast(qdec), kdec=bcast(kdec), gc=gc)
    pp = dict(p, sg_ws=sg_ws[0], sg_bias=jnp.repeat(sg_bs[0].T, DK, axis=1))
    x_mid, s_prompt = _prompt_mixer(x_prompt, tabs_p, mk_b, mv_b, dec_p, pp)
    y_prompt, tail = _prompt_ffn(x_mid, p)

    n = bs * ls
    per = HEADS * EXP
    te = (bs // EXP) * per
    pos_s = PAST_LEN + (np.arange(te) & (ls - 1))
    tabs_s = (*_rope_tables(pos_s, 1.0), *_rope_tables(pos_s, scale))
    dmat4, qdec4, kdec4, gc4 = _decay(ls)
    kdec_rows = np.ascontiguousarray(np.broadcast_to(np.tile(kdec4.reshape(per), te // per)[:, None], (te, DK)))
    xs = jnp.swapaxes(x_sample, 0, 1).reshape(n, D_MODEL)
    pre = _sample_pre(xs, tabs_s, kdec_rows, p, sg_ws[0][:, :ls, :ls].reshape(-1), sg_bs[0][:, :ls].reshape(-1))
    qe, kre, kde, ve, ge, xqe, osg, vrows, gates = pre

    rb = SAMPLE_BB * per
    blk16 = np.einsum("hk,hls->hlks", np.eye(HEADS, dtype=np.float32), dmat4).reshape(per, per)
    dec_s = dict(gc=gc4,
                 dmat=np.kron(np.eye(SAMPLE_BB, dtype=np.float32), blk16),
                 qdec=np.ascontiguousarray(
                     np.broadcast_to(np.tile(qdec4.reshape(per), SAMPLE_BB)[:, None], (rb, DV))),
                 gn=jnp.tile(jnp.repeat(ret_gn_g[0], ls, axis=0), (SAMPLE_BB, 1)))
    orete, oxe, s_sample = _sample_state(
        (qe, kre, kde, ve, ge, xqe), state_ret[0].reshape(bs, HEADS * DK, DV),
        cache_mem_k[0].reshape(bs, MEM_LEN * HEADS, DK), cache_mem_v[0].reshape(bs, MEM_LEN * HEADS, DK), dec_s)
    xs_mid = _sample_merge(xs, orete, oxe, osg, gates, p)
    y_sample, c2a, c2b, c3a, c3b = _sample_ffn(xs_mid, state_conv[0, :, 0], state_conv[0, :, 1], p)
    conv_s = jnp.stack([jnp.concatenate([c2a, c2b], -1), jnp.concatenate([c3a, c3b], -1)], axis=1)
    unpos = lambda a: jnp.swapaxes(a.reshape(ls, bs, a.shape[-1]), 0, 1)

    return (y_prompt, unpos(y_sample),
            s_prompt[None], tail[None, :, 6:8],
            mk.reshape(1, bp, MEM_LEN, HEADS, DK), mv.reshape(1, bp, MEM_LEN, HEADS, DK),
            s_sample.reshape(1, bs, HEADS, DK, DV), conv_s[None],
            unpos(vrows)[None])
```

```python
import functools

import numpy as np
import jax
import jax.numpy as jnp
from jax import lax
from jax.experimental import pallas as pl
from jax.experimental.pallas import tpu as pltpu

F32 = jnp.float32
BF16 = jnp.bfloat16

D_MODEL = 1024
HEADS = 4
DK = 128
DV = 256
QK_W = HEADS * DK
V_W = HEADS * DV
SG_W = 512
X_W = 512
MEM_LEN = 256
D_FF = 2816
CHUNK = 128
ROPE_BASE = 10000.0
EPS = 1e-6
PAST_LEN = 16384

C_Q, C_K, C_V, C_G, C_SU, C_SV, C_XQ, C_GT, C_END = 0, 512, 1024, 2048, 3072, 3584, 4096, 4608, 7680

PROMPT_TILE = 512
SAMPLE_BB = 8
EXP = HEADS
FFN_CW = 256
VMEM_LIMIT = 56 * 1024 * 1024


def _rms(x, g):
    return x * lax.rsqrt(jnp.mean(x * x, axis=-1, keepdims=True) + EPS) * g


def _stdnorm(x):
    mu = jnp.mean(x, axis=-1, keepdims=True)
    xc = x - mu
    var = jnp.mean(xc * xc, axis=-1, keepdims=True)
    return xc * lax.rsqrt(var + EPS)


def _gelu(x):
    return 0.5 * x * (1.0 + jnp.tanh(np.float32(np.sqrt(2.0 / np.pi)) * (x + 0.044715 * (x * x * x))))


def _sigmoid(x):
    return 1.0 / (1.0 + jnp.exp(-x))


def _softmax(s):
    e = jnp.exp(s - jnp.max(s, axis=-1, keepdims=True))
    return e / jnp.sum(e, axis=-1, keepdims=True)


def _mm(a, b):
    return jnp.dot(a.astype(BF16), b.astype(BF16), preferred_element_type=F32)


def _mm_nt(a, b):
    return lax.dot_general(a.astype(BF16), b.astype(BF16), (((1,), (1,)), ((), ())),
                           preferred_element_type=F32)


def _mm_tn(a, b):
    return lax.dot_general(a.astype(BF16), b.astype(BF16), (((0,), (0,)), ((), ())),
                           preferred_element_type=F32)


def _rope(x, cos, sin):
    return x * cos + pltpu.roll(x, DK // 2, 1) * sin


def _tril(w):
    r = lax.broadcasted_iota(jnp.int32, w.shape, 0)
    c = lax.broadcasted_iota(jnp.int32, w.shape, 1)
    return jnp.where(r >= c, w, 0.0)


def _sgu(u, vn, ws_ref, sgb_ref, out_ref):
    for g in range(HEADS):
        w = _tril(ws_ref[g]).astype(BF16)
        cols = slice(g * DK, (g + 1) * DK)
        for c in range(u.shape[0] // CHUNK):
            rows = slice(c * CHUNK, (c + 1) * CHUNK)
            mixed = jnp.dot(w, vn[rows, cols].astype(BF16), preferred_element_type=F32) + sgb_ref[:, cols]
            out_ref[rows, cols] = (u[rows, cols] * mixed).astype(out_ref.dtype)


def _conv_gate(zs_ref, n, cols_a, cols_b, cw_a, cb_a, cw_b, cb_b, fix=None):
    def conv(cols, cw, cb, k):
        z0 = zs_ref[8:8 + n, cols]
        z1 = zs_ref[7:7 + n, cols]
        z2 = zs_ref[6:6 + n, cols]
        if fix is not None:
            z1, z2 = fix(z1, z2, k)
        return cb + cw[0:1] * z2 + cw[1:2] * z1 + cw[2:3] * z0
    return _gelu(conv(cols_a, cw_a, cb_a, 0)) * conv(cols_b, cw_b, cb_b, 1)


def _const_spec(shape):
    nd = len(shape)
    return pl.BlockSpec(shape, lambda *_: (0,) * nd, pipeline_mode=pl.Buffered(1))


def _smem_spec():
    return pl.BlockSpec(memory_space=pltpu.SMEM)


def _params(sem):
    return pltpu.CompilerParams(dimension_semantics=sem, vmem_limit_bytes=VMEM_LIMIT)


def _memkv_kernel(mem_ref, g_ref, w_ref, k_ref, v_ref, kb_ref, vb_ref):
    kv = _mm(_rms(mem_ref[...], g_ref[...]), w_ref[...])
    k, v = kv[:, :X_W], kv[:, X_W:]
    for hd in range(HEADS):
        cols = slice(hd * DK, (hd + 1) * DK)
        k_ref[pl.ds(hd, MEM_LEN, stride=HEADS), :] = k[:, cols]
        v_ref[pl.ds(hd, MEM_LEN, stride=HEADS), :] = v[:, cols]
    kb_ref[...] = k.astype(BF16)
    vb_ref[...] = v.astype(BF16)


def _memkv(mem, g, w):
    b = mem.shape[0]
    blk = pl.BlockSpec((None, MEM_LEN, X_W), lambda i: (i, 0, 0))
    flat = pl.BlockSpec((None, MEM_LEN * HEADS, DK), lambda i: (i, 0, 0))
    return pl.pallas_call(
        _memkv_kernel,
        grid=(b,),
        in_specs=[pl.BlockSpec((None, MEM_LEN, D_MODEL), lambda i: (i, 0, 0)),
                  _const_spec((1, D_MODEL)), _const_spec((D_MODEL, 2 * X_W))],
        out_specs=[flat, flat, blk, blk],
        out_shape=[jax.ShapeDtypeStruct((b, MEM_LEN * HEADS, DK), F32)] * 2
        + [jax.ShapeDtypeStruct((b, MEM_LEN, X_W), BF16)] * 2,
        compiler_params=_params(("arbitrary",)),
        name="mem_kv",
    )(mem, g, w)


def _mixer_kernel(gc_ref, x_ref, cq_ref, sq_ref, ck_ref, sk_ref, mk_ref, mv_ref, ng_ref, win_ref, bg_ref,
                  gn_ref, lng_ref, ws_ref, sgb_ref, dmat_ref, qdec_ref, kdec_ref,
                  wr_ref, wsg_ref, wx_ref, wo_ref,
                  xo_ref, s_ref, oret_ref, osg_ref, ox_ref):
    tile = x_ref.shape[0]

    @pl.when(pl.program_id(1) == 0)
    def _():
        s_ref[...] = jnp.zeros_like(s_ref)

    x = x_ref[...]
    h = _rms(x, ng_ref[...]).astype(BF16)

    def proj(a, b):
        return jnp.dot(h, win_ref[:, a:b], preferred_element_type=F32)

    heads, chunks = range(HEADS), range(tile // CHUNK)
    rows = [slice(c * CHUNK, (c + 1) * CHUNK) for c in chunks]
    kcols = [slice(hd * DK, (hd + 1) * DK) for hd in heads]
    vcols = [slice(hd * DV, (hd + 1) * DV) for hd in heads]

    q = proj(C_Q, C_K)
    k = proj(C_K, C_V)
    vb = proj(C_V, C_G).astype(BF16)
    gsil = proj(C_G, C_SU)
    gsil = gsil * _sigmoid(gsil)
    cq, sq, ck, sk = cq_ref[...], sq_ref[...], ck_ref[...], sk_ref[...]
    qr = [_rope(q[:, kcols[hd]], cq, sq) for hd in heads]
    kr = [_rope(k[:, kcols[hd]], ck, sk) for hd in heads]
    sc = [[_mm_nt(qr[hd][rows[c]], kr[hd][rows[c]]) for c in chunks] for hd in heads]
    upd = [[_mm_tn(kr[hd][rows[c]] * kdec_ref[hd], vb[rows[c], vcols[hd]]) for c in chunks] for hd in heads]
    gt = proj(C_GT, C_END)
    states = []
    for hd in heads:
        st, before = s_ref[hd], []
        for c in chunks:
            before.append(st.astype(BF16))
            st = gc_ref[hd] * st + upd[hd][c]
        s_ref[hd] = st
        states.append(before)
    o = [[jnp.dot(jnp.concatenate([(sc[hd][c] * dmat_ref[hd]).astype(BF16),
                                   (qr[hd][rows[c]] * qdec_ref[hd]).astype(BF16)], axis=1),
                  jnp.concatenate([vb[rows[c], vcols[hd]], states[hd][c]], axis=0),
                  preferred_element_type=F32)
          for c in chunks] for hd in heads]
    for hd in heads:
        on = _stdnorm(jnp.concatenate(o[hd], axis=0)) * gn_ref[:, vcols[hd]]
        oret_ref[:, vcols[hd]] = (gsil[:, vcols[hd]] * on).astype(BF16)

    u = _gelu(proj(C_SU, C_SV))
    vn = (_stdnorm(_gelu(proj(C_SV, C_XQ))) * lng_ref[...]).astype(BF16)
    wsg = [_tril(ws_ref[g]).astype(BF16) for g in heads]
    mixed = [[jnp.dot(wsg[g], vn[rows[c], kcols[g]], preferred_element_type=F32) + sgb_ref[:, kcols[g]]
              for c in chunks] for g in heads]
    for g in heads:
        osg_ref[:, kcols[g]] = (u[:, kcols[g]] * jnp.concatenate(mixed[g], axis=0)).astype(BF16)

    xq = proj(C_XQ, C_GT)
    sx = [_mm_nt(xq[:, kcols[hd]], mk_ref[:, kcols[hd]]) * np.float32(DK ** -0.5) for hd in heads]
    px = [_softmax(sx[hd]) for hd in heads]
    for hd in heads:
        ox_ref[:, kcols[hd]] = _mm(px[hd], mv_ref[:, kcols[hd]]).astype(BF16)

    gates = _sigmoid(gt + bg_ref[...])
    merged = (gates[:, :D_MODEL] * jnp.dot(oret_ref[...], wr_ref[...], preferred_element_type=F32)
              + gates[:, D_MODEL:2 * D_MODEL] * jnp.dot(osg_ref[...], wsg_ref[...], preferred_element_type=F32)
              + gates[:, 2 * D_MODEL:] * jnp.dot(ox_ref[...], wx_ref[...], preferred_element_type=F32))
    xo_ref[...] = x + _mm(merged, wo_ref[...])


def _prompt_mixer(x, tabs, mk_b, mv_b, dec, p):
    b, l, _ = x.shape
    t = PROMPT_TILE
    tok = lambda w: pl.BlockSpec((None, t, w), lambda i, j: (i, j, 0))
    tab = pl.BlockSpec((t, DK), lambda i, j: (j, 0))
    mem = pl.BlockSpec((None, MEM_LEN, X_W), lambda i, j: (i, 0, 0))
    consts = [p["norm_mix_g"], p["w_in"], p["b_gate"], p["gn_g"], p["sg_ln_g"], p["sg_ws"], p["sg_bias"],
              dec["dmat"], dec["qdec"], dec["kdec"], p["w_br_ret"], p["w_br_sg"], p["w_br_x"], p["w_o"]]
    return pl.pallas_call(
        _mixer_kernel,
        grid=(b, l // t),
        in_specs=[_smem_spec(), tok(D_MODEL), tab, tab, tab, tab, mem, mem]
        + [_const_spec(c.shape) for c in consts],
        out_specs=[tok(D_MODEL), pl.BlockSpec((None, HEADS, DK, DV), lambda i, j: (i, 0, 0, 0))],
        out_shape=[jax.ShapeDtypeStruct((b, l, D_MODEL), F32),
                   jax.ShapeDtypeStruct((b, HEADS, DK, DV), F32)],
        scratch_shapes=[pltpu.VMEM((t, V_W), BF16), pltpu.VMEM((t, SG_W), BF16), pltpu.VMEM((t, X_W), BF16)],
        compiler_params=_params(("arbitrary", "arbitrary")),
        name="prompt_mixer",
    )(dec["gc"], x, *tabs, mk_b, mv_b, *consts)


def _ffn_kernel(x_ref, ng_ref, wup_ref, cw_ref, cb_ref, wdn_ref, nf_ref, y_ref, tail_ref, zs_ref):
    tile = x_ref.shape[0]
    first = pl.program_id(1) == 0

    @pl.when(first)
    def _():
        zs_ref[0:8, :] = jnp.zeros((8, 2 * D_FF), F32)

    @pl.when(jnp.logical_not(first))
    def _():
        zs_ref[0:8, :] = zs_ref[tile:tile + 8, :]

    x = x_ref[...]
    zs_ref[8:8 + tile, :] = _mm(_rms(x, ng_ref[...]), wup_ref[...])
    ca, cb = slice(0, D_FF), slice(D_FF, 2 * D_FF)
    gate = _conv_gate(zs_ref, tile, ca, cb, cw_ref[:, ca], cb_ref[:, ca], cw_ref[:, cb], cb_ref[:, cb])
    y = x + _mm(gate, wdn_ref[...])
    y_ref[...] = _rms(y, nf_ref[...])
    tail_ref[...] = zs_ref[tile:tile + 8, :]


def _prompt_ffn(x, p):
    b, l, _ = x.shape
    t = PROMPT_TILE
    tok = pl.BlockSpec((None, t, D_MODEL), lambda i, j: (i, j, 0))
    consts = [p["norm_ffn_g"], p["w_up"], p["conv_w"], p["conv_b"], p["w_down"], p["norm_final_g"]]
    return pl.pallas_call(
        _ffn_kernel,
        grid=(b, l // t),
        in_specs=[tok] + [_const_spec(c.shape) for c in consts],
        out_specs=[tok, pl.BlockSpec((None, 8, 2 * D_FF), lambda i, j: (i, 0, 0))],
        out_shape=[jax.ShapeDtypeStruct((b, l, D_MODEL), F32), jax.ShapeDtypeStruct((b, 8, 2 * D_FF), F32)],
        scratch_shapes=[pltpu.VMEM((t + 8, 2 * D_FF), F32)],
        compiler_params=_params(("arbitrary", "arbitrary")),
        name="prompt_ffn",
    )(x, *consts)


def _row_head(n):
    return (lax.broadcasted_iota(jnp.int32, (n, 1), 0) >> 2) & (HEADS - 1)


def _sample_pre_kernel(ws_ref, bs_ref, x_ref, cq_ref, sq_ref, ck_ref, sk_ref, kdec_ref, ng_ref, win_ref, bg_ref,
                       lng_ref,
                       qe_ref, kre_ref, kde_ref, ve_ref, ge_ref, xqe_ref, osg_ref, vrows_ref, gates_ref,
                       h_ref, vn_ref):
    n = x_ref.shape[0]
    nb = n // EXP
    ne = qe_ref.shape[0]
    i = pl.program_id(0)

    @pl.when(i == 0)
    def _():
        h_ref[...] = _rms(x_ref[...], ng_ref[...]).astype(BF16)
        vn_ref[...] = jnp.zeros_like(vn_ref)

    r = lax.broadcasted_iota(jnp.int32, (ne, n), 0)
    c = lax.broadcasted_iota(jnp.int32, (ne, n), 1)
    rep = jnp.where(c == (r & (EXP - 1)) * nb + i * (ne // (HEADS * EXP)) + (r >> 4), 1.0, 0.0).astype(BF16)
    he = jnp.dot(rep, h_ref[...], preferred_element_type=F32).astype(BF16)
    hh = _row_head(ne)
    h = h_ref[pl.ds(pl.multiple_of(i * nb, nb), nb), :]

    def proj(hm, a, b):
        return jnp.dot(hm, win_ref[:, a:b], preferred_element_type=F32)

    q, k = proj(he, C_Q, C_K), proj(he, C_K, C_V)
    xq = proj(he, C_XQ, C_GT)
    cq, sq, ck, sk, kdec = cq_ref[...], sq_ref[...], ck_ref[...], sk_ref[...], kdec_ref[...]
    for hd in range(HEADS):
        cols = slice(hd * DK, (hd + 1) * DK)
        own = hh == hd
        kr = _rope(k[:, cols], ck, sk)
        qe_ref[:, cols] = jnp.where(own, _rope(q[:, cols], cq, sq), 0.0).astype(BF16)
        kre_ref[:, cols] = jnp.where(own, kr, 0.0).astype(BF16)
        kde_ref[:, cols] = jnp.where(own, kr * kdec, 0.0).astype(BF16)
    v = proj(he, C_V, C_G)
    g = proj(he, C_G, C_SU)
    ve = jnp.zeros((ne, DV), F32)
    ge = jnp.zeros((ne, DV), F32)
    xqc = jnp.zeros((ne, DK), F32)
    for hd in range(HEADS):
        cols = slice(hd * DV, (hd + 1) * DV)
        own = hh == hd
        ve = jnp.where(own, v[:, cols], ve)
        ge = jnp.where(own, g[:, cols], ge)
        xqc = jnp.where(own, xq[:, hd * DK:(hd + 1) * DK], xqc)
    xqe_ref[...] = xqc.astype(BF16)
    ve_ref[...] = ve.astype(BF16)
    ge_ref[...] = ge * _sigmoid(ge)

    u = _gelu(proj(h, C_SU, C_SV))
    vn = _stdnorm(_gelu(proj(h, C_SV, C_XQ))) * lng_ref[...]
    vrows_ref[...] = vn
    vn_ref[i] = vn
    for g in range(HEADS):
        cols = slice(g * DK, (g + 1) * DK)
        mixed = jnp.full((nb, DK), bs_ref[g * EXP + i], F32)
        for s in range(EXP):
            w = jnp.where(s <= i, ws_ref[(g * EXP + i) * EXP + s], 0.0)
            mixed = mixed + w * vn_ref[s, :, cols]
        osg_ref[:, cols] = (u[:, cols] * mixed).astype(BF16)
    gates_ref[...] = _sigmoid(proj(h, C_GT, C_END) + bg_ref[...])


def _sample_pre(x, tabs, kdec_rows, p, ws4, bs4):
    n = x.shape[0]
    t = n // EXP
    te = (t // EXP) * HEADS * EXP
    consts = [p["norm_mix_g"], p["w_in"], p["b_gate"], p["sg_ln_g"]]
    rowblk = lambda rows, w: pl.BlockSpec((rows, w), lambda i: (i, 0))
    outs = [(te, QK_W, BF16), (te, QK_W, BF16), (te, QK_W, BF16), (te, DV, BF16), (te, DV, F32),
            (te, DK, BF16), (t, SG_W, BF16), (t, SG_W, F32), (t, 3 * D_MODEL, F32)]
    return pl.pallas_call(
        _sample_pre_kernel,
        grid=(EXP,),
        in_specs=[_smem_spec(), _smem_spec(), _const_spec(x.shape)] + [_const_spec((te, DK))] * 5
        + [_const_spec(c.shape) for c in consts],
        out_specs=[rowblk(r, w) for r, w, _ in outs],
        out_shape=[jax.ShapeDtypeStruct((r * EXP, w), d) for r, w, d in outs],
        scratch_shapes=[pltpu.VMEM((n, D_MODEL), BF16), pltpu.VMEM((EXP, t, SG_W), F32)],
        compiler_params=_params(("arbitrary",)),
        name="sample_pre",
    )(ws4, bs4, x, *tabs, kdec_rows, *consts)


def _sample_state_kernel(gc_ref, qe_ref, kre_ref, kde_ref, ve_ref, ge_ref, xqe_ref, s_ref, mk_ref, mv_ref,
                         dmat_ref, qdec_ref, gn_ref,
                         oret_ref, ox_ref, so_ref, cross_ref, oxs_ref):
    rows_n = qe_ref.shape[0]
    per = HEADS * EXP
    q, ve, xq = qe_ref[...], ve_ref[...], xqe_ref[...]
    sc = _mm_nt(q, kre_ref[...]) * dmat_ref[...]
    inner = jnp.dot(sc.astype(BF16), ve, preferred_element_type=F32)
    kd = kde_ref[...]
    rowb = lax.broadcasted_iota(jnp.int32, (rows_n, 1), 0) >> 4
    own_col = (lax.broadcasted_iota(jnp.int32, (per, MEM_LEN * HEADS), 1) & (HEADS - 1)) == _row_head(per)
    for b in range(rows_n // per):
        rows = slice(b * per, (b + 1) * per)
        state = s_ref[b]
        cross_ref[rows, :] = jnp.dot(q[rows], state.astype(BF16), preferred_element_type=F32)
        upd = _mm_tn(kd, jnp.where(rowb == b, ve, jnp.zeros_like(ve)))
        for hd in range(HEADS):
            hr = slice(hd * DK, (hd + 1) * DK)
            so_ref[b, hr, :] = gc_ref[hd] * state[hr] + upd[hr]
        sx = _mm_nt(xq[rows], mk_ref[b]) * np.float32(DK ** -0.5)
        p = _softmax(jnp.where(own_col, sx, np.float32(-1e30)))
        oxs_ref[rows, :] = _mm(p, mv_ref[b])
    o = inner + cross_ref[...] * qdec_ref[...]
    og = ge_ref[...] * (_stdnorm(o) * gn_ref[...])
    oxs = oxs_ref[...]
    hh = _row_head(rows_n)
    for hd in range(HEADS):
        oret_ref[:, hd * DV:(hd + 1) * DV] = jnp.where(hh == hd, og, 0.0)
        ox_ref[:, hd * DK:(hd + 1) * DK] = jnp.where(hh == hd, oxs, 0.0)


def _sample_state(pre, state, mk, mv, dec):
    qe, kre, kde, ve, ge, xqe = pre
    nb = state.shape[0]
    bb = SAMPLE_BB
    rb = bb * HEADS * EXP
    rowblk = lambda w: pl.BlockSpec((rb, w), lambda i: (i, 0))
    batblk = lambda a, c: pl.BlockSpec((bb, a, c), lambda i: (i, 0, 0))
    ne = qe.shape[0]
    return pl.pallas_call(
        _sample_state_kernel,
        grid=(nb // bb,),
        in_specs=[_smem_spec(), rowblk(QK_W), rowblk(QK_W), rowblk(QK_W), rowblk(DV), rowblk(DV), rowblk(DK),
                  batblk(HEADS * DK, DV), batblk(MEM_LEN * HEADS, DK), batblk(MEM_LEN * HEADS, DK),
                  _const_spec((rb, rb)), _const_spec((rb, DV)), _const_spec((rb, DV))],
        out_specs=[rowblk(V_W), rowblk(X_W), batblk(HEADS * DK, DV)],
        out_shape=[jax.ShapeDtypeStruct((ne, V_W), F32), jax.ShapeDtypeStruct((ne, X_W), F32),
                   jax.ShapeDtypeStruct(state.shape, F32)],
        scratch_shapes=[pltpu.VMEM((rb, DV), F32), pltpu.VMEM((rb, DK), F32)],
        compiler_params=_params(("arbitrary",)),
        name="sample_state",
    )(dec["gc"], qe, kre, kde, ve, ge, xqe, state, mk, mv, dec["dmat"], dec["qdec"], dec["gn"])


def _sample_merge_kernel(x_ref, orete_ref, oxe_ref, osg_ref, gates_ref, wr_ref, wsg_ref, wx_ref, wo_ref, xo_ref):
    n = x_ref.shape[0]
    ne = n * EXP
    r = lax.broadcasted_iota(jnp.int32, (n, ne), 0)
    c = lax.broadcasted_iota(jnp.int32, (n, ne), 1)
    col = jnp.where(r == (c & (EXP - 1)) * (n // EXP) + (c >> 4), 1.0, 0.0).astype(BF16)
    oret = jnp.dot(col, orete_ref[...].astype(BF16), preferred_element_type=F32)
    ox = jnp.dot(col, oxe_ref[...].astype(BF16), preferred_element_type=F32)
    gates = gates_ref[...]
    merged = (gates[:, :D_MODEL] * _mm(oret, wr_ref[...])
              + gates[:, D_MODEL:2 * D_MODEL] * jnp.dot(osg_ref[...], wsg_ref[...], preferred_element_type=F32)
              + gates[:, 2 * D_MODEL:] * _mm(ox, wx_ref[...]))
    xo_ref[...] = x_ref[...] + _mm(merged, wo_ref[...])


def _sample_merge(x, orete, oxe, osg, gates, p):
    ins = [x, orete, oxe, osg, gates, p["w_br_ret"], p["w_br_sg"], p["w_br_x"], p["w_o"]]
    return pl.pallas_call(
        _sample_merge_kernel,
        grid=(1,),
        in_specs=[_const_spec(a.shape) for a in ins],
        out_specs=pl.BlockSpec(x.shape, lambda i: (0, 0)),
        out_shape=jax.ShapeDtypeStruct(x.shape, F32),
        compiler_params=_params(("arbitrary",)),
        name="sample_merge",
    )(*ins)


def _sample_ffn_kernel(x_ref, ng_ref, wa_ref, wb_ref, cwa_ref, cwb_ref, cba_ref, cbb_ref, s0a_ref, s0b_ref,
                       s1a_ref, s1b_ref, wdn_ref, nf_ref,
                       y_ref, c2a_ref, c2b_ref, c3a_ref, c3b_ref, h_ref, acc_ref):
    nb = x_ref.shape[0] // EXP
    j = pl.program_id(0)

    @pl.when(j == 0)
    def _():
        h_ref[...] = _rms(x_ref[...], ng_ref[...]).astype(BF16)
        acc_ref[...] = jnp.zeros_like(acc_ref)

    h = h_ref[...]

    def conv(w_ref, cw_ref, cb_ref, s0_ref, s1_ref, c2_ref, c3_ref):
        z = jnp.dot(h, w_ref[...], preferred_element_type=F32)
        zp = [s0_ref[...], s1_ref[...]] + [z[l * nb:(l + 1) * nb] for l in range(EXP)]
        c2_ref[...] = zp[EXP]
        c3_ref[...] = zp[EXP + 1]
        cw, cb = cw_ref[...], cb_ref[...]
        return jnp.concatenate([cb + cw[0:1] * zp[l] + cw[1:2] * zp[l + 1] + cw[2:3] * zp[l + 2]
                                for l in range(EXP)], axis=0)

    a = conv(wa_ref, cwa_ref, cba_ref, s0a_ref, s1a_ref, c2a_ref, c3a_ref)
    b = conv(wb_ref, cwb_ref, cbb_ref, s0b_ref, s1b_ref, c2b_ref, c3b_ref)
    acc_ref[...] += _mm(_gelu(a) * b, wdn_ref[...])

    @pl.when(j == pl.num_programs(0) - 1)
    def _():
        y_ref[...] = _rms(x_ref[...] + acc_ref[...], nf_ref[...])


def _sample_ffn(x, sc0, sc1, p):
    n = x.shape[0]
    nb = n // EXP
    cw = FFN_CW
    nch = D_FF // cw
    ca = lambda rows: pl.BlockSpec((rows, cw), lambda j: (0, j))
    cb = lambda rows: pl.BlockSpec((rows, cw), lambda j: (0, nch + j))
    full = pl.BlockSpec((n, D_MODEL), lambda j: (0, 0))
    vec = pl.BlockSpec((1, D_MODEL), lambda j: (0, 0))
    return pl.pallas_call(
        _sample_ffn_kernel,
        grid=(nch,),
        in_specs=[full, vec, ca(D_MODEL), cb(D_MODEL), ca(3), cb(3), ca(1), cb(1), ca(nb), cb(nb), ca(nb), cb(nb),
                  pl.BlockSpec((cw, D_MODEL), lambda j: (j, 0)), vec],
        out_specs=[full] + [ca(nb)] * 4,
        out_shape=[jax.ShapeDtypeStruct((n, D_MODEL), F32)] + [jax.ShapeDtypeStruct((nb, D_FF), F32)] * 4,
        scratch_shapes=[pltpu.VMEM((n, D_MODEL), BF16), pltpu.VMEM((n, D_MODEL), F32)],
        compiler_params=_params(("arbitrary",)),
        name="sample_ffn",
    )(x, p["norm_ffn_g"], p["w_up"], p["w_up"], p["conv_w"], p["conv_w"], p["conv_b"], p["conv_b"],
      sc0, sc0, sc1, sc1, p["w_down"], p["norm_final_g"])


def _rope_tables(pos, scale):
    inv = ROPE_BASE ** (-np.arange(0, DK, 2, dtype=np.float64) / DK)
    ang = np.asarray(pos, np.float64)[:, None] * inv[None, :]
    cos, sin = np.cos(ang), np.sin(ang)
    return (np.concatenate([cos, cos], -1) * scale).astype(np.float32), \
        (np.concatenate([-sin, sin], -1) * scale).astype(np.float32)


def _decay(chunk):
    lg = np.log1p(-np.exp2(-5.0 - np.arange(HEADS, dtype=np.float64)))
    n = np.arange(chunk, dtype=np.float64)
    diff = n[:, None] - n[None, :]
    dmat = np.where(diff >= 0, np.exp(np.maximum(diff, 0.0)[None] * lg[:, None, None]), 0.0)
    qdec = np.exp((n + 1.0)[None, :] * lg[:, None])
    kdec = np.exp((chunk - 1.0 - n)[None, :] * lg[:, None])
    f32 = lambda a: a.astype(np.float32)
    return f32(dmat), f32(qdec), f32(kdec), f32(np.exp(chunk * lg))


def kernel(x_prompt, x_sample, mem_prompt, state_ret, state_conv, cache_mem_k, cache_mem_v, norm_mix_g, w_in,
           b_gate, ret_gn_g, sg_ln_g, sg_ws, sg_bs, mem_norm_g, w_mem_kv, w_br_ret, w_br_sg, w_br_x, w_o,
           norm_ffn_g, w_up, conv_w, conv_b, w_down, norm_final_g):
    bp, lp, _ = x_prompt.shape
    bs, ls, _ = x_sample.shape
    assert state_ret.shape[0] == 1 and ls == EXP and lp % PROMPT_TILE == 0 and bs % SAMPLE_BB == 0
    assert bs == CHUNK
    row = lambda a: a.reshape(1, -1)
    p = dict(norm_mix_g=row(norm_mix_g[0]), w_in=w_in[0].astype(BF16), b_gate=row(b_gate[0]),
             gn_g=row(ret_gn_g[0]), sg_ln_g=row(sg_ln_g[0]),
             w_br_ret=w_br_ret[0].astype(BF16), w_br_sg=w_br_sg[0].astype(BF16),
             w_br_x=w_br_x[0].astype(BF16), w_o=w_o[0].astype(BF16),
             norm_ffn_g=row(norm_ffn_g[0]), w_up=w_up[0].astype(BF16), conv_w=conv_w[0],
             conv_b=row(conv_b[0]), w_down=w_down[0].astype(BF16), norm_final_g=row(norm_final_g))
    scale = DK ** -0.5

    mk, mv, mk_b, mv_b = _memkv(mem_prompt, row(mem_norm_g[0]), w_mem_kv[0].astype(BF16))
    pos_p = np.arange(lp)
    tabs_p = (*_rope_tables(pos_p, 1.0), *_rope_tables(pos_p, scale))
    dmat, qdec, kdec, gc = _decay(CHUNK)
    bcast = lambda a: np.ascontiguousarray(np.broadcast_to(a[:, :, None], (HEADS, CHUNK, DK)))
    dec_p = dict(dmat=dmat, qdec=bcast(qdec), kdec=bcast(kdec), gc=gc)
    pp = dict(p, sg_ws=sg_ws[0], sg_bias=jnp.repeat(sg_bs[0].T, DK, axis=1))
    x_mid, s_prompt = _prompt_mixer(x_prompt, tabs_p, mk_b, mv_b, dec_p, pp)
    y_prompt, tail = _prompt_ffn(x_mid, p)

    n = bs * ls
    per = HEADS * EXP
    te = (bs // EXP) * per
    pos_s = PAST_LEN + (np.arange(te) & (ls - 1))
    tabs_s = (*_rope_tables(pos_s, 1.0), *_rope_tables(pos_s, scale))
    dmat4, qdec4, kdec4, gc4 = _decay(ls)
    kdec_rows = np.ascontiguousarray(np.broadcast_to(np.tile(kdec4.reshape(per), te // per)[:, None], (te, DK)))
    xs = jnp.swapaxes(x_sample, 0, 1).reshape(n, D_MODEL)
    pre = _sample_pre(xs, tabs_s, kdec_rows, p, sg_ws[0][:, :ls, :ls].reshape(-1), sg_bs[0][:, :ls].reshape(-1))
    qe, kre, kde, ve, ge, xqe, osg, vrows, gates = pre

    rb = SAMPLE_BB * per
    blk16 = np.einsum("hk,hls->hlks", np.eye(HEADS, dtype=np.float32), dmat4).reshape(per, per)
    dec_s = dict(gc=gc4,
                 dmat=np.kron(np.eye(SAMPLE_BB, dtype=np.float32), blk16),
                 qdec=np.ascontiguousarray(
                     np.broadcast_to(np.tile(qdec4.reshape(per), SAMPLE_BB)[:, None], (rb, DV))),
                 gn=jnp.tile(jnp.repeat(ret_gn_g[0], ls, axis=0), (SAMPLE_BB, 1)))
    orete, oxe, s_sample = _sample_state(
        (qe, kre, kde, ve, ge, xqe), state_ret[0].reshape(bs, HEADS * DK, DV),
        cache_mem_k[0].reshape(bs, MEM_LEN * HEADS, DK), cache_mem_v[0].reshape(bs, MEM_LEN * HEADS, DK), dec_s)
    xs_mid = _sample_merge(xs, orete, oxe, osg, gates, p)
    y_sample, c2a, c2b, c3a, c3b = _sample_ffn(xs_mid, state_conv[0, :, 0], state_conv[0, :, 1], p)
    conv_s = jnp.stack([jnp.concatenate([c2a, c2b], -1), jnp.concatenate([c3a, c3b], -1)], axis=1)
    unpos = lambda a: jnp.swapaxes(a.reshape(ls, bs, a.shape[-1]), 0, 1)

    return (y_prompt, unpos(y_sample),
            s_prompt[None], tail[None, :, 6:8],
            mk.reshape(1, bp, MEM_LEN, HEADS, DK), mv.reshape(1, bp, MEM_LEN, HEADS, DK),
            s_sample.reshape(1, bs, HEADS, DK, DV), conv_s[None],
            unpos(vrows)[None])
```

```python
import functools

import numpy as np
import jax
import jax.numpy as jnp
from jax import lax
from jax.experimental import pallas as pl
from jax.experimental.pallas import tpu as pltpu

F32 = jnp.float32
BF16 = jnp.bfloat16

D_MODEL = 1024
HEADS = 4
DK = 128
DV = 256
QK_W = HEADS * DK
V_W = HEADS * DV
SG_W = 512
X_W = 512
MEM_LEN = 256
D_FF = 2816
CHUNK = 128
ROPE_BASE = 10000.0
EPS = 1e-6
PAST_LEN = 16384

C_Q, C_K, C_V, C_G, C_SU, C_SV, C_XQ, C_GT, C_END = 0, 512, 1024, 2048, 3072, 3584, 4096, 4608, 7680

PROMPT_TILE = 512
SAMPLE_BB = 8
EXP = HEADS
FFN_CW = 1408
VMEM_LIMIT = 56 * 1024 * 1024


def _rms(x, g):
    return x * lax.rsqrt(jnp.mean(x * x, axis=-1, keepdims=True) + EPS) * g


def _stdnorm(x):
    mu = jnp.mean(x, axis=-1, keepdims=True)
    xc = x - mu
    var = jnp.mean(xc * xc, axis=-1, keepdims=True)
    return xc * lax.rsqrt(var + EPS)


_GELU_C0 = np.float32(np.sqrt(2.0 / np.pi))
_GELU_C1 = np.float32(np.sqrt(2.0 / np.pi) * 0.044715)


def _gelu(x):
    hx = 0.5 * x
    return hx + hx * jnp.tanh(x * (_GELU_C0 + _GELU_C1 * (x * x)))


def _sigmoid(x):
    return 0.5 + 0.5 * jnp.tanh(0.5 * x)


def _silu(x):
    hx = 0.5 * x
    return hx + hx * jnp.tanh(hx)


def _softmax(s):
    e = jnp.exp(s - jnp.max(s, axis=-1, keepdims=True))
    return e * (1.0 / jnp.sum(e, axis=-1, keepdims=True))


def _mm(a, b):
    return jnp.dot(a.astype(BF16), b.astype(BF16), preferred_element_type=F32)


def _mm_nt(a, b):
    return lax.dot_general(a.astype(BF16), b.astype(BF16), (((1,), (1,)), ((), ())),
                           preferred_element_type=F32)


def _mm_tn(a, b):
    return lax.dot_general(a.astype(BF16), b.astype(BF16), (((0,), (0,)), ((), ())),
                           preferred_element_type=F32)


def _rope(x, cos, sin):
    return x * cos + pltpu.roll(x, DK // 2, 1) * sin


def _tril(w):
    r = lax.broadcasted_iota(jnp.int32, w.shape, 0)
    c = lax.broadcasted_iota(jnp.int32, w.shape, 1)
    return jnp.where(r >= c, w, 0.0)


def _conv_gate(zs_ref, n, cols_a, cols_b, cw_a, cb_a, cw_b, cb_b, fix=None):
    def conv(cols, cw, cb, k):
        z0 = zs_ref[8:8 + n, cols]
        z1 = zs_ref[7:7 + n, cols]
        z2 = zs_ref[6:6 + n, cols]
        if fix is not None:
            z1, z2 = fix(z1, z2, k)
        return cb + cw[0:1] * z2 + cw[1:2] * z1 + cw[2:3] * z0
    return _gelu(conv(cols_a, cw_a, cb_a, 0)) * conv(cols_b, cw_b, cb_b, 1)


def _const_spec(shape):
    nd = len(shape)
    return pl.BlockSpec(shape, lambda *_: (0,) * nd, pipeline_mode=pl.Buffered(1))


def _smem_spec():
    return pl.BlockSpec(memory_space=pltpu.SMEM)


def _params(sem):
    return pltpu.CompilerParams(dimension_semantics=sem, vmem_limit_bytes=VMEM_LIMIT)


def _memkv_kernel(mem_ref, g_ref, w_ref, k_ref, v_ref, kb_ref, vb_ref):
    kv = _mm(_rms(mem_ref[...], g_ref[...]), w_ref[...])
    k, v = kv[:, :X_W], kv[:, X_W:]
    for hd in range(HEADS):
        cols = slice(hd * DK, (hd + 1) * DK)
        k_ref[pl.ds(hd, MEM_LEN, stride=HEADS), :] = k[:, cols]
        v_ref[pl.ds(hd, MEM_LEN, stride=HEADS), :] = v[:, cols]
    kb_ref[...] = k.astype(BF16)
    vb_ref[...] = v.astype(BF16)


def _memkv(mem, g, w):
    b = mem.shape[0]
    blk = pl.BlockSpec((None, MEM_LEN, X_W), lambda i: (i, 0, 0))
    flat = pl.BlockSpec((None, MEM_LEN * HEADS, DK), lambda i: (i, 0, 0))
    return pl.pallas_call(
        _memkv_kernel,
        grid=(b,),
        in_specs=[pl.BlockSpec((None, MEM_LEN, D_MODEL), lambda i: (i, 0, 0)),
                  _const_spec((1, D_MODEL)), _const_spec((D_MODEL, 2 * X_W))],
        out_specs=[flat, flat, blk, blk],
        out_shape=[jax.ShapeDtypeStruct((b, MEM_LEN * HEADS, DK), F32)] * 2
        + [jax.ShapeDtypeStruct((b, MEM_LEN, X_W), BF16)] * 2,
        compiler_params=_params(("arbitrary",)),
        name="mem_kv",
    )(mem, g, w)


def _mixer_kernel(gc_ref, x_ref, cq_ref, sq_ref, ck_ref, sk_ref, mk_ref, mv_ref, ng_ref, win_ref, bg_ref,
                  gn_ref, lng_ref, ws_ref, sgb_ref, dmat_ref, qdec_ref, kdec_ref,
                  wr_ref, wsg_ref, wx_ref, wo_ref,
                  xo_ref, s_ref, oret_ref, osg_ref, ox_ref):
    tile = x_ref.shape[0]

    @pl.when(pl.program_id(1) == 0)
    def _():
        s_ref[...] = jnp.zeros_like(s_ref)

    x = x_ref[...]
    h = _rms(x, ng_ref[...]).astype(BF16)

    def proj(a, b):
        return jnp.dot(h, win_ref[:, a:b], preferred_element_type=F32)

    heads, chunks = range(HEADS), range(tile // CHUNK)
    rows = [slice(c * CHUNK, (c + 1) * CHUNK) for c in chunks]
    kcols = [slice(hd * DK, (hd + 1) * DK) for hd in heads]
    vcols = [slice(hd * DV, (hd + 1) * DV) for hd in heads]

    q = proj(C_Q, C_K)
    k = proj(C_K, C_V)
    vb = proj(C_V, C_G).astype(BF16)
    gsil = _silu(proj(C_G, C_SU))
    cq, sq, ck, sk = cq_ref[...], sq_ref[...], ck_ref[...], sk_ref[...]
    qr = [_rope(q[:, kcols[hd]], cq, sq) for hd in heads]
    kr = [_rope(k[:, kcols[hd]], ck, sk) for hd in heads]
    sc = [[_mm_nt(qr[hd][rows[c]], kr[hd][rows[c]]) for c in chunks] for hd in heads]
    upd = [[_mm_tn(kr[hd][rows[c]] * kdec_ref[hd], vb[rows[c], vcols[hd]]) for c in chunks] for hd in heads]
    gt = proj(C_GT, C_END)
    states = []
    for hd in heads:
        st, before = s_ref[hd], []
        for c in chunks:
            before.append(st.astype(BF16))
            st = gc_ref[hd] * st + upd[hd][c]
        s_ref[hd] = st
        states.append(before)
    o = [[jnp.dot(jnp.concatenate([(sc[hd][c] * dmat_ref[hd]).astype(BF16),
                                   (qr[hd][rows[c]] * qdec_ref[hd]).astype(BF16)], axis=1),
                  jnp.concatenate([vb[rows[c], vcols[hd]], states[hd][c]], axis=0),
                  preferred_element_type=F32)
          for c in chunks] for hd in heads]
    for hd in heads:
        on = _stdnorm(jnp.concatenate(o[hd], axis=0)) * gn_ref[:, vcols[hd]]
        oret_ref[:, vcols[hd]] = (gsil[:, vcols[hd]] * on).astype(BF16)

    u = _gelu(proj(C_SU, C_SV))
    vn = (_stdnorm(_gelu(proj(C_SV, C_XQ))) * lng_ref[...]).astype(BF16)
    wsg = [_tril(ws_ref[g]).astype(BF16) for g in heads]
    mixed = [[jnp.dot(wsg[g], vn[rows[c], kcols[g]], preferred_element_type=F32) + sgb_ref[:, kcols[g]]
              for c in chunks] for g in heads]
    for g in heads:
        osg_ref[:, kcols[g]] = (u[:, kcols[g]] * jnp.concatenate(mixed[g], axis=0)).astype(BF16)

    xq = proj(C_XQ, C_GT)
    sx = [_mm_nt(xq[:, kcols[hd]], mk_ref[:, kcols[hd]]) * np.float32(DK ** -0.5) for hd in heads]
    px = [_softmax(sx[hd]) for hd in heads]
    for hd in heads:
        ox_ref[:, kcols[hd]] = _mm(px[hd], mv_ref[:, kcols[hd]]).astype(BF16)

    gates = _sigmoid(gt + bg_ref[...])
    merged = (gates[:, :D_MODEL] * jnp.dot(oret_ref[...], wr_ref[...], preferred_element_type=F32)
              + gates[:, D_MODEL:2 * D_MODEL] * jnp.dot(osg_ref[...], wsg_ref[...], preferred_element_type=F32)
              + gates[:, 2 * D_MODEL:] * jnp.dot(ox_ref[...], wx_ref[...], preferred_element_type=F32))
    xo_ref[...] = x + _mm(merged, wo_ref[...])


def _prompt_mixer(x, tabs, mk_b, mv_b, dec, p):
    b, l, _ = x.shape
    t = PROMPT_TILE
    tok = lambda w: pl.BlockSpec((None, t, w), lambda i, j: (i, j, 0))
    tab = pl.BlockSpec((t, DK), lambda i, j: (j, 0))
    mem = pl.BlockSpec((None, MEM_LEN, X_W), lambda i, j: (i, 0, 0))
    consts = [p["norm_mix_g"], p["w_in"], p["b_gate"], p["gn_g"], p["sg_ln_g"], p["sg_ws"], p["sg_bias"],
              dec["dmat"], dec["qdec"], dec["kdec"], p["w_br_ret"], p["w_br_sg"], p["w_br_x"], p["w_o"]]
    return pl.pallas_call(
        _mixer_kernel,
        grid=(b, l // t),
        in_specs=[_smem_spec(), tok(D_MODEL), tab, tab, tab, tab, mem, mem]
        + [_const_spec(c.shape) for c in consts],
        out_specs=[tok(D_MODEL), pl.BlockSpec((None, HEADS, DK, DV), lambda i, j: (i, 0, 0, 0))],
        out_shape=[jax.ShapeDtypeStruct((b, l, D_MODEL), F32),
                   jax.ShapeDtypeStruct((b, HEADS, DK, DV), F32)],
        scratch_shapes=[pltpu.VMEM((t, V_W), BF16), pltpu.VMEM((t, SG_W), BF16), pltpu.VMEM((t, X_W), BF16)],
        compiler_params=_params(("arbitrary", "arbitrary")),
        name="prompt_mixer",
    )(dec["gc"], x, *tabs, mk_b, mv_b, *consts)


def _ffn_kernel(x_ref, ng_ref, wup_ref, cw_ref, cb_ref, wdn_ref, nf_ref, y_ref, tail_ref, zs_ref):
    tile = x_ref.shape[0]
    first = pl.program_id(1) == 0

    @pl.when(first)
    def _():
        zs_ref[0:8, :] = jnp.zeros((8, 2 * D_FF), F32)

    @pl.when(jnp.logical_not(first))
    def _():
        zs_ref[0:8, :] = zs_ref[tile:tile + 8, :]

    x = x_ref[...]
    zs_ref[8:8 + tile, :] = _mm(_rms(x, ng_ref[...]), wup_ref[...])
    ca, cb = slice(0, D_FF), slice(D_FF, 2 * D_FF)
    gate = _conv_gate(zs_ref, tile, ca, cb, cw_ref[:, ca], cb_ref[:, ca], cw_ref[:, cb], cb_ref[:, cb])
    y = x + _mm(gate, wdn_ref[...])
    y_ref[...] = _rms(y, nf_ref[...])
    tail_ref[...] = zs_ref[tile:tile + 8, :]


def _prompt_ffn(x, p):
    b, l, _ = x.shape
    t = PROMPT_TILE
    tok = pl.BlockSpec((None, t, D_MODEL), lambda i, j: (i, j, 0))
    consts = [p["norm_ffn_g"], p["w_up"], p["conv_w"], p["conv_b"], p["w_down"], p["norm_final_g"]]
    return pl.pallas_call(
        _ffn_kernel,
        grid=(b, l // t),
        in_specs=[tok] + [_const_spec(c.shape) for c in consts],
        out_specs=[tok, pl.BlockSpec((None, 8, 2 * D_FF), lambda i, j: (i, 0, 0))],
        out_shape=[jax.ShapeDtypeStruct((b, l, D_MODEL), F32), jax.ShapeDtypeStruct((b, 8, 2 * D_FF), F32)],
        scratch_shapes=[pltpu.VMEM((t + 8, 2 * D_FF), F32)],
        compiler_params=_params(("arbitrary", "arbitrary")),
        name="prompt_ffn",
    )(x, *consts)


def _row_head(n):
    return (lax.broadcasted_iota(jnp.int32, (n, 1), 0) >> 2) & (HEADS - 1)


def _sample_pre_kernel(ws_ref, bs_ref, x_ref, cq_ref, sq_ref, ck_ref, sk_ref, kdec_ref, ng_ref, win_ref, bg_ref,
                       lng_ref,
                       qe_ref, kre_ref, kde_ref, ve_ref, ge_ref, xqe_ref, osg_ref, vrows_ref, gates_ref,
                       h_ref, vn_ref):
    n = x_ref.shape[0]
    nb = n // EXP
    ne = qe_ref.shape[0]
    i = pl.program_id(0)

    @pl.when(i == 0)
    def _():
        h_ref[...] = _rms(x_ref[...], ng_ref[...]).astype(BF16)
        vn_ref[...] = jnp.zeros_like(vn_ref)

    r = lax.broadcasted_iota(jnp.int32, (ne, n), 0)
    c = lax.broadcasted_iota(jnp.int32, (ne, n), 1)
    rep = jnp.where(c == (r & (EXP - 1)) * nb + i * (ne // (HEADS * EXP)) + (r >> 4), 1.0, 0.0).astype(BF16)
    he = jnp.dot(rep, h_ref[...], preferred_element_type=F32).astype(BF16)
    hh = _row_head(ne)
    h = h_ref[pl.ds(pl.multiple_of(i * nb, nb), nb), :]

    def proj(hm, a, b):
        return jnp.dot(hm, win_ref[:, a:b], preferred_element_type=F32)

    q, k = proj(he, C_Q, C_K), proj(he, C_K, C_V)
    xq = proj(he, C_XQ, C_GT)
    cq, sq, ck, sk, kdec = cq_ref[...], sq_ref[...], ck_ref[...], sk_ref[...], kdec_ref[...]
    for hd in range(HEADS):
        cols = slice(hd * DK, (hd + 1) * DK)
        own = hh == hd
        kr = _rope(k[:, cols], ck, sk)
        qe_ref[:, cols] = jnp.where(own, _rope(q[:, cols], cq, sq), 0.0).astype(BF16)
        kre_ref[:, cols] = jnp.where(own, kr, 0.0).astype(BF16)
        kde_ref[:, cols] = jnp.where(own, kr * kdec, 0.0).astype(BF16)
    v = proj(he, C_V, C_G)
    g = proj(he, C_G, C_SU)
    ve = jnp.zeros((ne, DV), F32)
    ge = jnp.zeros((ne, DV), F32)
    xqc = jnp.zeros((ne, DK), F32)
    for hd in range(HEADS):
        cols = slice(hd * DV, (hd + 1) * DV)
        own = hh == hd
        ve = jnp.where(own, v[:, cols], ve)
        ge = jnp.where(own, g[:, cols], ge)
        xqc = jnp.where(own, xq[:, hd * DK:(hd + 1) * DK], xqc)
    xqe_ref[...] = xqc.astype(BF16)
    ve_ref[...] = ve.astype(BF16)
    ge_ref[...] = _silu(ge)

    u = _gelu(proj(h, C_SU, C_SV))
    vn = _stdnorm(_gelu(proj(h, C_SV, C_XQ))) * lng_ref[...]
    vrows_ref[...] = vn
    vn_ref[i] = vn
    for g in range(HEADS):
        cols = slice(g * DK, (g + 1) * DK)
        mixed = jnp.full((nb, DK), bs_ref[g * EXP + i], F32)
        for s in range(EXP):
            w = jnp.where(s <= i, ws_ref[(g * EXP + i) * EXP + s], 0.0)
            mixed = mixed + w * vn_ref[s, :, cols]
        osg_ref[:, cols] = (u[:, cols] * mixed).astype(BF16)
    gates_ref[...] = _sigmoid(proj(h, C_GT, C_END) + bg_ref[...])


def _sample_pre(x, tabs, kdec_rows, p, ws4, bs4):
    n = x.shape[0]
    t = n // EXP
    te = (t // EXP) * HEADS * EXP
    consts = [p["norm_mix_g"], p["w_in"], p["b_gate"], p["sg_ln_g"]]
    rowblk = lambda rows, w: pl.BlockSpec((rows, w), lambda i: (i, 0))
    outs = [(te, QK_W, BF16), (te, QK_W, BF16), (te, QK_W, BF16), (te, DV, BF16), (te, DV, F32),
            (te, DK, BF16), (t, SG_W, BF16), (t, SG_W, F32), (t, 3 * D_MODEL, F32)]
    return pl.pallas_call(
        _sample_pre_kernel,
        grid=(EXP,),
        in_specs=[_smem_spec(), _smem_spec(), _const_spec(x.shape)] + [_const_spec((te, DK))] * 5
        + [_const_spec(c.shape) for c in consts],
        out_specs=[rowblk(r, w) for r, w, _ in outs],
        out_shape=[jax.ShapeDtypeStruct((r * EXP, w), d) for r, w, d in outs],
        scratch_shapes=[pltpu.VMEM((n, D_MODEL), BF16), pltpu.VMEM((EXP, t, SG_W), F32)],
        compiler_params=_params(("arbitrary",)),
        name="sample_pre",
    )(ws4, bs4, x, *tabs, kdec_rows, *consts)


def _sample_state_kernel(gc_ref, qe_ref, kre_ref, kde_ref, ve_ref, ge_ref, xqe_ref, s_ref, mk_ref, mv_ref,
                         dmat_ref, qdec_ref, gn_ref,
                         oret_ref, ox_ref, so_ref):
    rows_n = qe_ref.shape[0]
    per = HEADS * EXP
    batches = range(rows_n // per)
    rows = [slice(b * per, (b + 1) * per) for b in batches]
    q, ve, xq = qe_ref[...], ve_ref[...], xqe_ref[...]
    sc = _mm_nt(q, kre_ref[...]) * dmat_ref[...]
    inner = jnp.dot(sc.astype(BF16), ve, preferred_element_type=F32)
    kd = kde_ref[...]
    rowb = lax.broadcasted_iota(jnp.int32, (rows_n, 1), 0) >> 4
    cross = jnp.concatenate(
        [jnp.dot(q[rows[b]], s_ref[b].astype(BF16), preferred_element_type=F32) for b in batches], axis=0)
    upd = [_mm_tn(kd, jnp.where(rowb == b, ve, jnp.zeros_like(ve))) for b in batches]
    for b in batches:
        for hd in range(HEADS):
            hr = slice(hd * DK, (hd + 1) * DK)
            so_ref[b, hr, :] = gc_ref[hd] * s_ref[b, hr, :] + upd[b][hr]
    hh = _row_head(rows_n)
    own_col = (lax.broadcasted_iota(jnp.int32, (rows_n, MEM_LEN * HEADS), 1) & (HEADS - 1)) == hh
    sx = jnp.concatenate([_mm_nt(xq[rows[b]], mk_ref[b]) for b in batches], axis=0) * np.float32(DK ** -0.5)
    p = _softmax(jnp.where(own_col, sx, np.float32(-1e30))).astype(BF16)
    oxs = jnp.concatenate([_mm(p[rows[b]], mv_ref[b]) for b in batches], axis=0)
    o = inner + cross * qdec_ref[...]
    og = ge_ref[...] * (_stdnorm(o) * gn_ref[...])
    for hd in range(HEADS):
        oret_ref[:, hd * DV:(hd + 1) * DV] = jnp.where(hh == hd, og, 0.0)
        ox_ref[:, hd * DK:(hd + 1) * DK] = jnp.where(hh == hd, oxs, 0.0)


def _sample_state(pre, state, mk, mv, dec):
    qe, kre, kde, ve, ge, xqe = pre
    nb = state.shape[0]
    bb = SAMPLE_BB
    rb = bb * HEADS * EXP
    rowblk = lambda w: pl.BlockSpec((rb, w), lambda i: (i, 0))
    batblk = lambda a, c: pl.BlockSpec((bb, a, c), lambda i: (i, 0, 0))
    ne = qe.shape[0]
    return pl.pallas_call(
        _sample_state_kernel,
        grid=(nb // bb,),
        in_specs=[_smem_spec(), rowblk(QK_W), rowblk(QK_W), rowblk(QK_W), rowblk(DV), rowblk(DV), rowblk(DK),
                  batblk(HEADS * DK, DV), batblk(MEM_LEN * HEADS, DK), batblk(MEM_LEN * HEADS, DK),
                  _const_spec((rb, rb)), _const_spec((rb, DV)), _const_spec((rb, DV))],
        out_specs=[rowblk(V_W), rowblk(X_W), batblk(HEADS * DK, DV)],
        out_shape=[jax.ShapeDtypeStruct((ne, V_W), F32), jax.ShapeDtypeStruct((ne, X_W), F32),
                   jax.ShapeDtypeStruct(state.shape, F32)],
        compiler_params=_params(("arbitrary",)),
        name="sample_state",
    )(dec["gc"], qe, kre, kde, ve, ge, xqe, state, mk, mv, dec["dmat"], dec["qdec"], dec["gn"])


def _sample_merge_kernel(x_ref, orete_ref, oxe_ref, osg_ref, gates_ref, wr_ref, wsg_ref, wx_ref, wo_ref, xo_ref):
    n = x_ref.shape[0]
    ne = n * EXP
    r = lax.broadcasted_iota(jnp.int32, (n, ne), 0)
    c = lax.broadcasted_iota(jnp.int32, (n, ne), 1)
    col = jnp.where(r == (c & (EXP - 1)) * (n // EXP) + (c >> 4), 1.0, 0.0).astype(BF16)
    oret = jnp.dot(col, orete_ref[...].astype(BF16), preferred_element_type=F32)
    ox = jnp.dot(col, oxe_ref[...].astype(BF16), preferred_element_type=F32)
    gates = gates_ref[...]
    merged = (gates[:, :D_MODEL] * _mm(oret, wr_ref[...])
              + gates[:, D_MODEL:2 * D_MODEL] * jnp.dot(osg_ref[...], wsg_ref[...], preferred_element_type=F32)
              + gates[:, 2 * D_MODEL:] * _mm(ox, wx_ref[...]))
    xo_ref[...] = x_ref[...] + _mm(merged, wo_ref[...])


def _sample_merge(x, orete, oxe, osg, gates, p):
    ins = [x, orete, oxe, osg, gates, p["w_br_ret"], p["w_br_sg"], p["w_br_x"], p["w_o"]]
    return pl.pallas_call(
        _sample_merge_kernel,
        grid=(1,),
        in_specs=[_const_spec(a.shape) for a in ins],
        out_specs=pl.BlockSpec(x.shape, lambda i: (0, 0)),
        out_shape=jax.ShapeDtypeStruct(x.shape, F32),
        compiler_params=_params(("arbitrary",)),
        name="sample_merge",
    )(*ins)


def _sample_ffn_kernel(x_ref, ng_ref, wa_ref, wb_ref, cwa_ref, cwb_ref, cba_ref, cbb_ref, s0a_ref, s0b_ref,
                       s1a_ref, s1b_ref, wdn_ref, nf_ref,
                       y_ref, c2a_ref, c2b_ref, c3a_ref, c3b_ref, h_ref, acc_ref):
    nb = x_ref.shape[0] // EXP
    j = pl.program_id(0)

    @pl.when(j == 0)
    def _():
        h_ref[...] = _rms(x_ref[...], ng_ref[...]).astype(BF16)
        acc_ref[...] = jnp.zeros_like(acc_ref)

    h = h_ref[...]

    def conv(w_ref, cw_ref, cb_ref, s0_ref, s1_ref, c2_ref, c3_ref):
        z = jnp.dot(h, w_ref[...], preferred_element_type=F32)
        zp = [s0_ref[...], s1_ref[...]] + [z[l * nb:(l + 1) * nb] for l in range(EXP)]
        c2_ref[...] = zp[EXP]
        c3_ref[...] = zp[EXP + 1]
        cw, cb = cw_ref[...], cb_ref[...]
        return jnp.concatenate([cb + cw[0:1] * zp[l] + cw[1:2] * zp[l + 1] + cw[2:3] * zp[l + 2]
                                for l in range(EXP)], axis=0)

    a = conv(wa_ref, cwa_ref, cba_ref, s0a_ref, s1a_ref, c2a_ref, c3a_ref)
    b = conv(wb_ref, cwb_ref, cbb_ref, s0b_ref, s1b_ref, c2b_ref, c3b_ref)
    acc_ref[...] += _mm(_gelu(a) * b, wdn_ref[...])

    @pl.when(j == pl.num_programs(0) - 1)
    def _():
        y_ref[...] = _rms(x_ref[...] + acc_ref[...], nf_ref[...])


def _sample_ffn(x, sc0, sc1, p):
    n = x.shape[0]
    nb = n // EXP
    cw = FFN_CW
    nch = D_FF // cw
    ca = lambda rows: pl.BlockSpec((rows, cw), lambda j: (0, j))
    cb = lambda rows: pl.BlockSpec((rows, cw), lambda j: (0, nch + j))
    full = pl.BlockSpec((n, D_MODEL), lambda j: (0, 0))
    vec = pl.BlockSpec((1, D_MODEL), lambda j: (0, 0))
    return pl.pallas_call(
        _sample_ffn_kernel,
        grid=(nch,),
        in_specs=[full, vec, ca(D_MODEL), cb(D_MODEL), ca(3), cb(3), ca(1), cb(1), ca(nb), cb(nb), ca(nb), cb(nb),
                  pl.BlockSpec((cw, D_MODEL), lambda j: (j, 0)), vec],
        out_specs=[full] + [ca(nb)] * 4,
        out_shape=[jax.ShapeDtypeStruct((n, D_MODEL), F32)] + [jax.ShapeDtypeStruct((nb, D_FF), F32)] * 4,
        scratch_shapes=[pltpu.VMEM((n, D_MODEL), BF16), pltpu.VMEM((n, D_MODEL), F32)],
        compiler_params=_params(("arbitrary",)),
        name="sample_ffn",
    )(x, p["norm_ffn_g"], p["w_up"], p["w_up"], p["conv_w"], p["conv_w"], p["conv_b"], p["conv_b"],
      sc0, sc0, sc1, sc1, p["w_down"], p["norm_final_g"])


def _rope_tables(pos, scale):
    inv = ROPE_BASE ** (-np.arange(0, DK, 2, dtype=np.float64) / DK)
    ang = np.asarray(pos, np.float64)[:, None] * inv[None, :]
    cos, sin = np.cos(ang), np.sin(ang)
    return (np.concatenate([cos, cos], -1) * scale).astype(np.float32), \
        (np.concatenate([-sin, sin], -1) * scale).astype(np.float32)


def _decay(chunk):
    lg = np.log1p(-np.exp2(-5.0 - np.arange(HEADS, dtype=np.float64)))
    n = np.arange(chunk, dtype=np.float64)
    diff = n[:, None] - n[None, :]
    dmat = np.where(diff >= 0, np.exp(np.maximum(diff, 0.0)[None] * lg[:, None, None]), 0.0)
    qdec = np.exp((n + 1.0)[None, :] * lg[:, None])
    kdec = np.exp((chunk - 1.0 - n)[None, :] * lg[:, None])
    f32 = lambda a: a.astype(np.float32)
    return f32(dmat), f32(qdec), f32(kdec), f32(np.exp(chunk * lg))


def kernel(x_prompt, x_sample, mem_prompt, state_ret, state_conv, cache_mem_k, cache_mem_v, norm_mix_g, w_in,
           b_gate, ret_gn_g, sg_ln_g, sg_ws, sg_bs, mem_norm_g, w_mem_kv, w_br_ret, w_br_sg, w_br_x, w_o,
           norm_ffn_g, w_up, conv_w, conv_b, w_down, norm_final_g):
    bp, lp, _ = x_prompt.shape
    bs, ls, _ = x_sample.shape
    assert state_ret.shape[0] == 1 and ls == EXP and lp % PROMPT_TILE == 0 and bs % SAMPLE_BB == 0
    assert bs == CHUNK
    row = lambda a: a.reshape(1, -1)
    p = dict(norm_mix_g=row(norm_mix_g[0]), w_in=w_in[0].astype(BF16), b_gate=row(b_gate[0]),
             gn_g=row(ret_gn_g[0]), sg_ln_g=row(sg_ln_g[0]),
             w_br_ret=w_br_ret[0].astype(BF16), w_br_sg=w_br_sg[0].astype(BF16),
             w_br_x=w_br_x[0].astype(BF16), w_o=w_o[0].astype(BF16),
             norm_ffn_g=row(norm_ffn_g[0]), w_up=w_up[0].astype(BF16), conv_w=conv_w[0],
             conv_b=row(conv_b[0]), w_down=w_down[0].astype(BF16), norm_final_g=row(norm_final_g))
    scale = DK ** -0.5

    mk, mv, mk_b, mv_b = _memkv(mem_prompt, row(mem_norm_g[0]), w_mem_kv[0].astype(BF16))
    pos_p = np.arange(lp)
    tabs_p = (*_rope_tables(pos_p, 1.0), *_rope_tables(pos_p, scale))
    dmat, qdec, kdec, gc = _decay(CHUNK)
    bcast = lambda a: np.ascontiguousarray(np.broadcast_to(a[:, :, None], (HEADS, CHUNK, DK)))
    dec_p = dict(dmat=dmat, qdec=bcast(qdec), kdec=bcast(kdec), gc=gc)
    pp = dict(p, sg_ws=sg_ws[0], sg_bias=jnp.repeat(sg_bs[0].T, DK, axis=1))
    x_mid, s_prompt = _prompt_mixer(x_prompt, tabs_p, mk_b, mv_b, dec_p, pp)
    y_prompt, tail = _prompt_ffn(x_mid, p)

    n = bs * ls
    per = HEADS * EXP
    te = (bs // EXP) * per
    pos_s = PAST_LEN + (np.arange(te) & (ls - 1))
    tabs_s = (*_rope_tables(pos_s, 1.0), *_rope_tables(pos_s, scale))
    dmat4, qdec4, kdec4, gc4 = _decay(ls)
    kdec_rows = np.ascontiguousarray(np.broadcast_to(np.tile(kdec4.reshape(per), te // per)[:, None], (te, DK)))
    xs = jnp.swapaxes(x_sample, 0, 1).reshape(n, D_MODEL)
    pre = _sample_pre(xs, tabs_s, kdec_rows, p, sg_ws[0][:, :ls, :ls].reshape(-1), sg_bs[0][:, :ls].reshape(-1))
    qe, kre, kde, ve, ge, xqe, osg, vrows, gates = pre

    rb = SAMPLE_BB * per
    blk16 = np.einsum("hk,hls->hlks", np.eye(HEADS, dtype=np.float32), dmat4).reshape(per, per)
    dec_s = dict(gc=gc4,
                 dmat=np.kron(np.eye(SAMPLE_BB, dtype=np.float32), blk16),
                 qdec=np.ascontiguousarray(
                     np.broadcast_to(np.tile(qdec4.reshape(per), SAMPLE_BB)[:, None], (rb, DV))),
                 gn=jnp.tile(jnp.repeat(ret_gn_g[0], ls, axis=0), (SAMPLE_BB, 1)))
    orete, oxe, s_sample = _sample_state(
        (qe, kre, kde, ve, ge, xqe), state_ret[0].reshape(bs, HEADS * DK, DV),
        cache_mem_k[0].reshape(bs, MEM_LEN * HEADS, DK), cache_mem_v[0].reshape(bs, MEM_LEN * HEADS, DK), dec_s)
    xs_mid = _sample_merge(xs, orete, oxe, osg, gates, p)
    y_sample, c2a, c2b, c3a, c3b = _sample_ffn(xs_mid, state_conv[0, :, 0], state_conv[0, :, 1], p)
    conv_s = jnp.stack([jnp.concatenate([c2a, c2b], -1), jnp.concatenate([c3a, c3b], -1)], axis=1)
    unpos = lambda a: jnp.swapaxes(a.reshape(ls, bs, a.shape[-1]), 0, 1)

    return (y_prompt, unpos(y_sample),
            s_prompt[None], tail[None, :, 6:8],
            mk.reshape(1, bp, MEM_LEN, HEADS, DK), mv.reshape(1, bp, MEM_LEN, HEADS, DK),
            s_sample.reshape(1, bs, HEADS, DK, DV), conv_s[None],
            unpos(vrows)[None])
```

```python
import functools

import numpy as np
import jax
import jax.numpy as jnp
from jax import lax
from jax.experimental import pallas as pl
from jax.experimental.pallas import tpu as pltpu

F32 = jnp.float32
BF16 = jnp.bfloat16

D_MODEL = 1024
HEADS = 4
DK = 128
DV = 256
QK_W = HEADS * DK
V_W = HEADS * DV
SG_W = 512
X_W = 512
MEM_LEN = 256
D_FF = 2816
CHUNK = 128
ROPE_BASE = 10000.0
EPS = 1e-6
PAST_LEN = 16384

C_Q, C_K, C_V, C_G, C_SU, C_SV, C_XQ, C_GT, C_END = 0, 512, 1024, 2048, 3072, 3584, 4096, 4608, 7680

PROMPT_TILE = 512
SAMPLE_BB = 8
EXP = HEADS
FFN_CW = 1408
VMEM_LIMIT = 56 * 1024 * 1024


def _rms(x, g):
    return x * lax.rsqrt(jnp.mean(x * x, axis=-1, keepdims=True) + EPS) * g


def _stdnorm(x):
    mu = jnp.mean(x, axis=-1, keepdims=True)
    xc = x - mu
    var = jnp.mean(xc * xc, axis=-1, keepdims=True)
    return xc * lax.rsqrt(var + EPS)


_GELU_C0 = np.float32(np.sqrt(2.0 / np.pi))
_GELU_C1 = np.float32(np.sqrt(2.0 / np.pi) * 0.044715)


def _gelu(x):
    hx = 0.5 * x
    return hx + hx * jnp.tanh(x * (_GELU_C0 + _GELU_C1 * (x * x)))


def _sigmoid(x):
    return 0.5 + 0.5 * jnp.tanh(0.5 * x)


def _silu(x):
    hx = 0.5 * x
    return hx + hx * jnp.tanh(hx)


def _softmax(s):
    e = jnp.exp(s - jnp.max(s, axis=-1, keepdims=True))
    return e * (1.0 / jnp.sum(e, axis=-1, keepdims=True))


def _mm(a, b):
    return jnp.dot(a.astype(BF16), b.astype(BF16), preferred_element_type=F32)


def _mm_nt(a, b):
    return lax.dot_general(a.astype(BF16), b.astype(BF16), (((1,), (1,)), ((), ())),
                           preferred_element_type=F32)


def _mm_tn(a, b):
    return lax.dot_general(a.astype(BF16), b.astype(BF16), (((0,), (0,)), ((), ())),
                           preferred_element_type=F32)


def _rope(x, cos, sin):
    return x * cos + pltpu.roll(x, DK // 2, 1) * sin


def _tril(w):
    r = lax.broadcasted_iota(jnp.int32, w.shape, 0)
    c = lax.broadcasted_iota(jnp.int32, w.shape, 1)
    return jnp.where(r >= c, w, 0.0)


def _conv_gate(zs_ref, n, cols_a, cols_b, cw_a, cb_a, cw_b, cb_b):
    def conv(cols, cw, cb):
        zz = zs_ref[0:8 + n, cols]
        z1 = pltpu.roll(zz, 1, 0)[8:]
        z2 = pltpu.roll(zz, 2, 0)[8:]
        return cb + cw[0:1] * z2 + cw[1:2] * z1 + cw[2:3] * zz[8:]
    return _gelu(conv(cols_a, cw_a, cb_a)) * conv(cols_b, cw_b, cb_b)


def _const_spec(shape):
    nd = len(shape)
    return pl.BlockSpec(shape, lambda *_: (0,) * nd, pipeline_mode=pl.Buffered(1))


def _smem_spec():
    return pl.BlockSpec(memory_space=pltpu.SMEM)


def _params(sem):
    return pltpu.CompilerParams(dimension_semantics=sem, vmem_limit_bytes=VMEM_LIMIT)


def _cast_blocks(refs):
    for src, dst in refs:
        dst[...] = src[...].astype(BF16)


def _cast_specs(weights, steps, flat_step):
    specs, shapes = [], []
    for w in weights:
        rows, cols = w.shape
        nblk = max(n for n in range(1, steps + 1) if steps % n == 0 and rows % (16 * n) == 0)
        specs.append(pl.BlockSpec((rows // nblk, cols), lambda *g, r=steps // nblk: (flat_step(*g) // r, 0)))
        shapes.append(jax.ShapeDtypeStruct(w.shape, BF16))
    return specs, shapes


def _memkv_kernel(mem_ref, g_ref, w_ref, *refs):
    ncast = (len(refs) - 4) // 2
    casts, (k_ref, v_ref, kb_ref, vb_ref) = refs[:ncast], refs[ncast:ncast + 4]
    kv = _mm(_rms(mem_ref[...], g_ref[...]), w_ref[...])
    k, v = kv[:, :X_W], kv[:, X_W:]
    for hd in range(HEADS):
        cols = slice(hd * DK, (hd + 1) * DK)
        k_ref[pl.ds(hd, MEM_LEN, stride=HEADS), :] = k[:, cols]
        v_ref[pl.ds(hd, MEM_LEN, stride=HEADS), :] = v[:, cols]
    kb_ref[...] = k.astype(BF16)
    vb_ref[...] = v.astype(BF16)
    _cast_blocks(zip(casts, refs[ncast + 4:]))


def _memkv(mem, g, w, cast_weights):
    b = mem.shape[0]
    blk = pl.BlockSpec((None, MEM_LEN, X_W), lambda i: (i, 0, 0))
    flat = pl.BlockSpec((None, MEM_LEN * HEADS, DK), lambda i: (i, 0, 0))
    cspec, cshape = _cast_specs(cast_weights, b, lambda i: i)
    return pl.pallas_call(
        _memkv_kernel,
        grid=(b,),
        in_specs=[pl.BlockSpec((None, MEM_LEN, D_MODEL), lambda i: (i, 0, 0)),
                  _const_spec((1, D_MODEL)), _const_spec((D_MODEL, 2 * X_W))] + cspec,
        out_specs=[flat, flat, blk, blk] + cspec,
        out_shape=[jax.ShapeDtypeStruct((b, MEM_LEN * HEADS, DK), F32)] * 2
        + [jax.ShapeDtypeStruct((b, MEM_LEN, X_W), BF16)] * 2 + cshape,
        compiler_params=_params(("arbitrary",)),
        name="mem_kv",
    )(mem, g, w, *cast_weights)


def _mixer_kernel(gc_ref, x_ref, cq_ref, sq_ref, ck_ref, sk_ref, mk_ref, mv_ref, ng_ref, win_ref, bg_ref,
                  gn_ref, lng_ref, ws_ref, sgb_ref, dmat_ref, qdec_ref, kdec_ref,
                  wr_ref, wsg_ref, wx_ref, wo_ref, wup_ref, wdn_ref,
                  xo_ref, s_ref, wup_o_ref, wdn_o_ref, oret_ref, osg_ref, ox_ref):
    tile = x_ref.shape[0]
    _cast_blocks(((wup_ref, wup_o_ref), (wdn_ref, wdn_o_ref)))

    @pl.when(pl.program_id(1) == 0)
    def _():
        s_ref[...] = jnp.zeros_like(s_ref)

    x = x_ref[...]
    h = _rms(x, ng_ref[...]).astype(BF16)

    def proj(a, b):
        return jnp.dot(h, win_ref[:, a:b], preferred_element_type=F32)

    heads, chunks = range(HEADS), range(tile // CHUNK)
    rows = [slice(c * CHUNK, (c + 1) * CHUNK) for c in chunks]
    kcols = [slice(hd * DK, (hd + 1) * DK) for hd in heads]
    vcols = [slice(hd * DV, (hd + 1) * DV) for hd in heads]

    q = proj(C_Q, C_K)
    k = proj(C_K, C_V)
    vb = proj(C_V, C_G).astype(BF16)
    gsil = _silu(proj(C_G, C_SU))
    cq, sq, ck, sk = cq_ref[...], sq_ref[...], ck_ref[...], sk_ref[...]
    qr = [_rope(q[:, kcols[hd]], cq, sq) for hd in heads]
    kr = [_rope(k[:, kcols[hd]], ck, sk) for hd in heads]
    sc = [[_mm_nt(qr[hd][rows[c]], kr[hd][rows[c]]) for c in chunks] for hd in heads]
    upd = [[_mm_tn(kr[hd][rows[c]] * kdec_ref[hd], vb[rows[c], vcols[hd]]) for c in chunks] for hd in heads]
    gt = proj(C_GT, C_END)
    states = []
    for hd in heads:
        st, before = s_ref[hd], []
        for c in chunks:
            before.append(st.astype(BF16))
            st = gc_ref[hd] * st + upd[hd][c]
        s_ref[hd] = st
        states.append(before)
    o = [[jnp.dot(jnp.concatenate([(sc[hd][c] * dmat_ref[hd]).astype(BF16),
                                   (qr[hd][rows[c]] * qdec_ref[hd]).astype(BF16)], axis=1),
                  jnp.concatenate([vb[rows[c], vcols[hd]], states[hd][c]], axis=0),
                  preferred_element_type=F32)
          for c in chunks] for hd in heads]
    for hd in heads:
        on = _stdnorm(jnp.concatenate(o[hd], axis=0)) * gn_ref[:, vcols[hd]]
        oret_ref[:, vcols[hd]] = (gsil[:, vcols[hd]] * on).astype(BF16)

    u = _gelu(proj(C_SU, C_SV))
    vn = (_stdnorm(_gelu(proj(C_SV, C_XQ))) * lng_ref[...]).astype(BF16)
    wsg = [_tril(ws_ref[g]).astype(BF16) for g in heads]
    mixed = [[jnp.dot(wsg[g], vn[rows[c], kcols[g]], preferred_element_type=F32) + sgb_ref[:, kcols[g]]
              for c in chunks] for g in heads]
    for g in heads:
        osg_ref[:, kcols[g]] = (u[:, kcols[g]] * jnp.concatenate(mixed[g], axis=0)).astype(BF16)

    xq = proj(C_XQ, C_GT)
    sx = [_mm_nt(xq[:, kcols[hd]], mk_ref[:, kcols[hd]]) * np.float32(DK ** -0.5) for hd in heads]
    px = [_softmax(sx[hd]) for hd in heads]
    for hd in heads:
        ox_ref[:, kcols[hd]] = _mm(px[hd], mv_ref[:, kcols[hd]]).astype(BF16)

    gates = _sigmoid(gt + bg_ref[...])
    merged = (gates[:, :D_MODEL] * jnp.dot(oret_ref[...], wr_ref[...], preferred_element_type=F32)
              + gates[:, D_MODEL:2 * D_MODEL] * jnp.dot(osg_ref[...], wsg_ref[...], preferred_element_type=F32)
              + gates[:, 2 * D_MODEL:] * jnp.dot(ox_ref[...], wx_ref[...], preferred_element_type=F32))
    xo_ref[...] = x + _mm(merged, wo_ref[...])


def _prompt_mixer(x, tabs, mk_b, mv_b, dec, p, cast_weights):
    b, l, _ = x.shape
    t = PROMPT_TILE
    nt = l // t
    cspec, cshape = _cast_specs(cast_weights, b * nt, lambda i, j: i * nt + j)
    tok = lambda w: pl.BlockSpec((None, t, w), lambda i, j: (i, j, 0))
    tab = pl.BlockSpec((t, DK), lambda i, j: (j, 0))
    mem = pl.BlockSpec((None, MEM_LEN, X_W), lambda i, j: (i, 0, 0))
    consts = [p["norm_mix_g"], p["w_in"], p["b_gate"], p["gn_g"], p["sg_ln_g"], p["sg_ws"], p["sg_bias"],
              dec["dmat"], dec["qdec"], dec["kdec"], p["w_br_ret"], p["w_br_sg"], p["w_br_x"], p["w_o"]]
    return pl.pallas_call(
        _mixer_kernel,
        grid=(b, l // t),
        in_specs=[_smem_spec(), tok(D_MODEL), tab, tab, tab, tab, mem, mem]
        + [_const_spec(c.shape) for c in consts] + cspec,
        out_specs=[tok(D_MODEL), pl.BlockSpec((None, HEADS, DK, DV), lambda i, j: (i, 0, 0, 0))] + cspec,
        out_shape=[jax.ShapeDtypeStruct((b, l, D_MODEL), F32),
                   jax.ShapeDtypeStruct((b, HEADS, DK, DV), F32)] + cshape,
        scratch_shapes=[pltpu.VMEM((t, V_W), BF16), pltpu.VMEM((t, SG_W), BF16), pltpu.VMEM((t, X_W), BF16)],
        compiler_params=_params(("arbitrary", "arbitrary")),
        name="prompt_mixer",
    )(dec["gc"], x, *tabs, mk_b, mv_b, *consts, *cast_weights)


def _ffn_kernel(x_ref, ng_ref, wup_ref, cw_ref, cb_ref, wdn_ref, nf_ref, y_ref, tail_ref, zs_ref):
    tile = x_ref.shape[0]
    first = pl.program_id(1) == 0

    @pl.when(first)
    def _():
        zs_ref[0:8, :] = jnp.zeros((8, 2 * D_FF), F32)

    @pl.when(jnp.logical_not(first))
    def _():
        zs_ref[0:8, :] = zs_ref[tile:tile + 8, :]

    x = x_ref[...]
    zs_ref[8:8 + tile, :] = _mm(_rms(x, ng_ref[...]), wup_ref[...])
    ca, cb = slice(0, D_FF), slice(D_FF, 2 * D_FF)
    gate = _conv_gate(zs_ref, tile, ca, cb, cw_ref[:, ca], cb_ref[:, ca], cw_ref[:, cb], cb_ref[:, cb])
    y = x + _mm(gate, wdn_ref[...])
    y_ref[...] = _rms(y, nf_ref[...])
    tail_ref[...] = zs_ref[tile:tile + 8, :]


def _prompt_ffn(x, p):
    b, l, _ = x.shape
    t = PROMPT_TILE
    tok = pl.BlockSpec((None, t, D_MODEL), lambda i, j: (i, j, 0))
    consts = [p["norm_ffn_g"], p["w_up"], p["conv_w"], p["conv_b"], p["w_down"], p["norm_final_g"]]
    return pl.pallas_call(
        _ffn_kernel,
        grid=(b, l // t),
        in_specs=[tok] + [_const_spec(c.shape) for c in consts],
        out_specs=[tok, pl.BlockSpec((None, 8, 2 * D_FF), lambda i, j: (i, 0, 0))],
        out_shape=[jax.ShapeDtypeStruct((b, l, D_MODEL), F32), jax.ShapeDtypeStruct((b, 8, 2 * D_FF), F32)],
        scratch_shapes=[pltpu.VMEM((t + 8, 2 * D_FF), F32)],
        compiler_params=_params(("arbitrary", "arbitrary")),
        name="prompt_ffn",
    )(x, *consts)


def _row_head(n):
    return (lax.broadcasted_iota(jnp.int32, (n, 1), 0) >> 2) & (HEADS - 1)


def _sample_pre_kernel(ws_ref, bs_ref, x_ref, cq_ref, sq_ref, ck_ref, sk_ref, kdec_ref, ng_ref, win_ref, bg_ref,
                       lng_ref,
                       qe_ref, kre_ref, kde_ref, ve_ref, ge_ref, xqe_ref, osg_ref, vrows_ref, gates_ref,
                       h_ref, vn_ref):
    n = x_ref.shape[0]
    nb = n // EXP
    ne = qe_ref.shape[0]
    i = pl.program_id(0)

    @pl.when(i == 0)
    def _():
        h_ref[...] = _rms(x_ref[...], ng_ref[...]).astype(BF16)
        vn_ref[...] = jnp.zeros_like(vn_ref)

    r = lax.broadcasted_iota(jnp.int32, (ne, n), 0)
    c = lax.broadcasted_iota(jnp.int32, (ne, n), 1)
    rep = jnp.where(c == (r & (EXP - 1)) * nb + i * (ne // (HEADS * EXP)) + (r >> 4), 1.0, 0.0).astype(BF16)
    he = jnp.dot(rep, h_ref[...], preferred_element_type=F32).astype(BF16)
    hh = _row_head(ne)
    h = h_ref[pl.ds(pl.multiple_of(i * nb, nb), nb), :]

    def proj(hm, a, b):
        return jnp.dot(hm, win_ref[:, a:b], preferred_element_type=F32)

    q, k = proj(he, C_Q, C_K), proj(he, C_K, C_V)
    xq = proj(he, C_XQ, C_GT)
    cq, sq, ck, sk, kdec = cq_ref[...], sq_ref[...], ck_ref[...], sk_ref[...], kdec_ref[...]
    for hd in range(HEADS):
        cols = slice(hd * DK, (hd + 1) * DK)
        own = hh == hd
        kr = _rope(k[:, cols], ck, sk)
        qe_ref[:, cols] = jnp.where(own, _rope(q[:, cols], cq, sq), 0.0).astype(BF16)
        kre_ref[:, cols] = jnp.where(own, kr, 0.0).astype(BF16)
        kde_ref[:, cols] = jnp.where(own, kr * kdec, 0.0).astype(BF16)
    v = proj(he, C_V, C_G)
    g = proj(he, C_G, C_SU)
    ve = jnp.zeros((ne, DV), F32)
    ge = jnp.zeros((ne, DV), F32)
    xqc = jnp.zeros((ne, DK), F32)
    for hd in range(HEADS):
        cols = slice(hd * DV, (hd + 1) * DV)
        own = hh == hd
        ve = jnp.where(own, v[:, cols], ve)
        ge = jnp.where(own, g[:, cols], ge)
        xqc = jnp.where(own, xq[:, hd * DK:(hd + 1) * DK], xqc)
    xqe_ref[...] = xqc.astype(BF16)
    ve_ref[...] = ve.astype(BF16)
    ge_ref[...] = _silu(ge)

    u = _gelu(proj(h, C_SU, C_SV))
    vn = _stdnorm(_gelu(proj(h, C_SV, C_XQ))) * lng_ref[...]
    vrows_ref[...] = vn
    vn_ref[i] = vn
    for g in range(HEADS):
        cols = slice(g * DK, (g + 1) * DK)
        mixed = jnp.full((nb, DK), bs_ref[g * EXP + i], F32)
        for s in range(EXP):
            w = jnp.where(s <= i, ws_ref[(g * EXP + i) * EXP + s], 0.0)
            mixed = mixed + w * vn_ref[s, :, cols]
        osg_ref[:, cols] = (u[:, cols] * mixed).astype(BF16)
    gates_ref[...] = _sigmoid(proj(h, C_GT, C_END) + bg_ref[...])


def _sample_pre(x, tabs, kdec_rows, p, ws4, bs4):
    n = x.shape[0]
    t = n // EXP
    te = (t // EXP) * HEADS * EXP
    consts = [p["norm_mix_g"], p["w_in"], p["b_gate"], p["sg_ln_g"]]
    rowblk = lambda rows, w: pl.BlockSpec((rows, w), lambda i: (i, 0))
    outs = [(te, QK_W, BF16), (te, QK_W, BF16), (te, QK_W, BF16), (te, DV, BF16), (te, DV, F32),
            (te, DK, BF16), (t, SG_W, BF16), (t, SG_W, F32), (t, 3 * D_MODEL, F32)]
    return pl.pallas_call(
        _sample_pre_kernel,
        grid=(EXP,),
        in_specs=[_smem_spec(), _smem_spec(), _const_spec(x.shape)] + [_const_spec((te, DK))] * 5
        + [_const_spec(c.shape) for c in consts],
        out_specs=[rowblk(r, w) for r, w, _ in outs],
        out_shape=[jax.ShapeDtypeStruct((r * EXP, w), d) for r, w, d in outs],
        scratch_shapes=[pltpu.VMEM((n, D_MODEL), BF16), pltpu.VMEM((EXP, t, SG_W), F32)],
        compiler_params=_params(("arbitrary",)),
        name="sample_pre",
    )(ws4, bs4, x, *tabs, kdec_rows, *consts)


def _sample_state_kernel(gc_ref, qe_ref, kre_ref, kde_ref, ve_ref, ge_ref, xqe_ref, s_ref, mk_ref, mv_ref,
                         dmat_ref, qdec_ref, gn_ref,
                         oret_ref, ox_ref, so_ref):
    rows_n = qe_ref.shape[0]
    per = HEADS * EXP
    batches = range(rows_n // per)
    rows = [slice(b * per, (b + 1) * per) for b in batches]
    q, ve, xq = qe_ref[...], ve_ref[...], xqe_ref[...]
    sc = _mm_nt(q, kre_ref[...]) * dmat_ref[...]
    inner = jnp.dot(sc.astype(BF16), ve, preferred_element_type=F32)
    kd = kde_ref[...]
    rowb = lax.broadcasted_iota(jnp.int32, (rows_n, 1), 0) >> 4
    cross = jnp.concatenate(
        [jnp.dot(q[rows[b]], s_ref[b].astype(BF16), preferred_element_type=F32) for b in batches], axis=0)
    upd = [_mm_tn(kd, jnp.where(rowb == b, ve, jnp.zeros_like(ve))) for b in batches]
    for b in batches:
        for hd in range(HEADS):
            hr = slice(hd * DK, (hd + 1) * DK)
            so_ref[b, hr, :] = gc_ref[hd] * s_ref[b, hr, :] + upd[b][hr]
    hh = _row_head(rows_n)
    own_col = (lax.broadcasted_iota(jnp.int32, (rows_n, MEM_LEN * HEADS), 1) & (HEADS - 1)) == hh
    sx = jnp.concatenate([_mm_nt(xq[rows[b]], mk_ref[b]) for b in batches], axis=0) * np.float32(DK ** -0.5)
    p = _softmax(jnp.where(own_col, sx, np.float32(-1e30))).astype(BF16)
    oxs = jnp.concatenate([_mm(p[rows[b]], mv_ref[b]) for b in batches], axis=0)
    o = inner + cross * qdec_ref[...]
    og = ge_ref[...] * (_stdnorm(o) * gn_ref[...])
    for hd in range(HEADS):
        oret_ref[:, hd * DV:(hd + 1) * DV] = jnp.where(hh == hd, og, 0.0)
        ox_ref[:, hd * DK:(hd + 1) * DK] = jnp.where(hh == hd, oxs, 0.0)


def _sample_state(pre, state, mk, mv, dec):
    qe, kre, kde, ve, ge, xqe = pre
    nb = state.shape[0]
    bb = SAMPLE_BB
    rb = bb * HEADS * EXP
    rowblk = lambda w: pl.BlockSpec((rb, w), lambda i: (i, 0))
    batblk = lambda a, c: pl.BlockSpec((bb, a, c), lambda i: (i, 0, 0))
    ne = qe.shape[0]
    return pl.pallas_call(
        _sample_state_kernel,
        grid=(nb // bb,),
        in_specs=[_smem_spec(), rowblk(QK_W), rowblk(QK_W), rowblk(QK_W), rowblk(DV), rowblk(DV), rowblk(DK),
                  batblk(HEADS * DK, DV), batblk(MEM_LEN * HEADS, DK), batblk(MEM_LEN * HEADS, DK),
                  _const_spec((rb, rb)), _const_spec((rb, DV)), _const_spec((rb, DV))],
        out_specs=[rowblk(V_W), rowblk(X_W), batblk(HEADS * DK, DV)],
        out_shape=[jax.ShapeDtypeStruct((ne, V_W), F32), jax.ShapeDtypeStruct((ne, X_W), F32),
                   jax.ShapeDtypeStruct(state.shape, F32)],
        compiler_params=_params(("arbitrary",)),
        name="sample_state",
    )(dec["gc"], qe, kre, kde, ve, ge, xqe, state, mk, mv, dec["dmat"], dec["qdec"], dec["gn"])


def _sample_merge_kernel(x_ref, orete_ref, oxe_ref, osg_ref, gates_ref, wr_ref, wsg_ref, wx_ref, wo_ref, xo_ref):
    n = x_ref.shape[0]
    ne = n * EXP
    r = lax.broadcasted_iota(jnp.int32, (n, ne), 0)
    c = lax.broadcasted_iota(jnp.int32, (n, ne), 1)
    col = jnp.where(r == (c & (EXP - 1)) * (n // EXP) + (c >> 4), 1.0, 0.0).astype(BF16)
    oret = jnp.dot(col, orete_ref[...].astype(BF16), preferred_element_type=F32)
    ox = jnp.dot(col, oxe_ref[...].astype(BF16), preferred_element_type=F32)
    gates = gates_ref[...]
    merged = (gates[:, :D_MODEL] * _mm(oret, wr_ref[...])
              + gates[:, D_MODEL:2 * D_MODEL] * jnp.dot(osg_ref[...], wsg_ref[...], preferred_element_type=F32)
              + gates[:, 2 * D_MODEL:] * _mm(ox, wx_ref[...]))
    xo_ref[...] = x_ref[...] + _mm(merged, wo_ref[...])


def _sample_merge(x, orete, oxe, osg, gates, p):
    ins = [x, orete, oxe, osg, gates, p["w_br_ret"], p["w_br_sg"], p["w_br_x"], p["w_o"]]
    return pl.pallas_call(
        _sample_merge_kernel,
        grid=(1,),
        in_specs=[_const_spec(a.shape) for a in ins],
        out_specs=pl.BlockSpec(x.shape, lambda i: (0, 0)),
        out_shape=jax.ShapeDtypeStruct(x.shape, F32),
        compiler_params=_params(("arbitrary",)),
        name="sample_merge",
    )(*ins)


def _sample_ffn_kernel(x_ref, ng_ref, wa_ref, wb_ref, cwa_ref, cwb_ref, cba_ref, cbb_ref, s0a_ref, s0b_ref,
                       s1a_ref, s1b_ref, wdn_ref, nf_ref,
                       y_ref, c2a_ref, c2b_ref, c3a_ref, c3b_ref, h_ref, acc_ref):
    nb = x_ref.shape[0] // EXP
    j = pl.program_id(0)

    @pl.when(j == 0)
    def _():
        h_ref[...] = _rms(x_ref[...], ng_ref[...]).astype(BF16)
        acc_ref[...] = jnp.zeros_like(acc_ref)

    h = h_ref[...]

    def conv(w_ref, cw_ref, cb_ref, s0_ref, s1_ref, c2_ref, c3_ref):
        z = jnp.dot(h, w_ref[...], preferred_element_type=F32)
        zp = [s0_ref[...], s1_ref[...]] + [z[l * nb:(l + 1) * nb] for l in range(EXP)]
        c2_ref[...] = zp[EXP]
        c3_ref[...] = zp[EXP + 1]
        cw, cb = cw_ref[...], cb_ref[...]
        return jnp.concatenate([cb + cw[0:1] * zp[l] + cw[1:2] * zp[l + 1] + cw[2:3] * zp[l + 2]
                                for l in range(EXP)], axis=0)

    a = conv(wa_ref, cwa_ref, cba_ref, s0a_ref, s1a_ref, c2a_ref, c3a_ref)
    b = conv(wb_ref, cwb_ref, cbb_ref, s0b_ref, s1b_ref, c2b_ref, c3b_ref)
    acc_ref[...] += _mm(_gelu(a) * b, wdn_ref[...])

    @pl.when(j == pl.num_programs(0) - 1)
    def _():
        y_ref[...] = _rms(x_ref[...] + acc_ref[...], nf_ref[...])


def _sample_ffn(x, sc, p):
    n = x.shape[0]
    nb = n // EXP
    cw = FFN_CW
    nch = D_FF // cw
    ca = lambda rows: pl.BlockSpec((rows, cw), lambda j: (0, j))
    cb = lambda rows: pl.BlockSpec((rows, cw), lambda j: (0, nch + j))
    sa = lambda k: pl.BlockSpec((None, nb, cw), lambda j: (k, 0, j))
    sb = lambda k: pl.BlockSpec((None, nb, cw), lambda j: (k, 0, nch + j))
    full = pl.BlockSpec((n, D_MODEL), lambda j: (0, 0))
    vec = pl.BlockSpec((1, D_MODEL), lambda j: (0, 0))
    return pl.pallas_call(
        _sample_ffn_kernel,
        grid=(nch,),
        in_specs=[full, vec, ca(D_MODEL), cb(D_MODEL), ca(3), cb(3), ca(1), cb(1), sa(0), sb(0), sa(1), sb(1),
                  pl.BlockSpec((cw, D_MODEL), lambda j: (j, 0)), vec],
        out_specs=[full] + [ca(nb)] * 4,
        out_shape=[jax.ShapeDtypeStruct((n, D_MODEL), F32)] + [jax.ShapeDtypeStruct((nb, D_FF), F32)] * 4,
        scratch_shapes=[pltpu.VMEM((n, D_MODEL), BF16), pltpu.VMEM((n, D_MODEL), F32)],
        compiler_params=_params(("arbitrary",)),
        name="sample_ffn",
    )(x, p["norm_ffn_g"], p["w_up"], p["w_up"], p["conv_w"], p["conv_w"], p["conv_b"], p["conv_b"],
      sc, sc, sc, sc, p["w_down"], p["norm_final_g"])


def _rope_tables(pos, scale):
    inv = ROPE_BASE ** (-np.arange(0, DK, 2, dtype=np.float64) / DK)
    ang = np.asarray(pos, np.float64)[:, None] * inv[None, :]
    cos, sin = np.cos(ang), np.sin(ang)
    return (np.concatenate([cos, cos], -1) * scale).astype(np.float32), \
        (np.concatenate([-sin, sin], -1) * scale).astype(np.float32)


def _decay(chunk):
    lg = np.log1p(-np.exp2(-5.0 - np.arange(HEADS, dtype=np.float64)))
    n = np.arange(chunk, dtype=np.float64)
    diff = n[:, None] - n[None, :]
    dmat = np.where(diff >= 0, np.exp(np.maximum(diff, 0.0)[None] * lg[:, None, None]), 0.0)
    qdec = np.exp((n + 1.0)[None, :] * lg[:, None])
    kdec = np.exp((chunk - 1.0 - n)[None, :] * lg[:, None])
    f32 = lambda a: a.astype(np.float32)
    return f32(dmat), f32(qdec), f32(kdec), f32(np.exp(chunk * lg))


def kernel(x_prompt, x_sample, mem_prompt, state_ret, state_conv, cache_mem_k, cache_mem_v, norm_mix_g, w_in,
           b_gate, ret_gn_g, sg_ln_g, sg_ws, sg_bs, mem_norm_g, w_mem_kv, w_br_ret, w_br_sg, w_br_x, w_o,
           norm_ffn_g, w_up, conv_w, conv_b, w_down, norm_final_g):
    bp, lp, _ = x_prompt.shape
    bs, ls, _ = x_sample.shape
    assert state_ret.shape[0] == 1 and ls == EXP and lp % PROMPT_TILE == 0 and bs % SAMPLE_BB == 0
    assert bs == CHUNK
    row = lambda a: a.reshape(1, -1)
    scale = DK ** -0.5

    mk, mv, mk_b, mv_b, w_in_b, w_br_ret_b, w_br_sg_b, w_br_x_b, w_o_b = _memkv(
        mem_prompt, row(mem_norm_g[0]), w_mem_kv[0], [w_in[0], w_br_ret[0], w_br_sg[0], w_br_x[0], w_o[0]])
    p = dict(norm_mix_g=row(norm_mix_g[0]), w_in=w_in_b, b_gate=row(b_gate[0]),
             gn_g=row(ret_gn_g[0]), sg_ln_g=row(sg_ln_g[0]),
             w_br_ret=w_br_ret_b, w_br_sg=w_br_sg_b, w_br_x=w_br_x_b, w_o=w_o_b,
             norm_ffn_g=row(norm_ffn_g[0]), conv_w=conv_w[0], conv_b=row(conv_b[0]),
             norm_final_g=row(norm_final_g))
    pos_p = np.arange(lp)
    tabs_p = (*_rope_tables(pos_p, 1.0), *_rope_tables(pos_p, scale))
    dmat, qdec, kdec, gc = _decay(CHUNK)
    bcast = lambda a: np.ascontiguousarray(np.broadcast_to(a[:, :, None], (HEADS, CHUNK, DK)))
    dec_p = dict(dmat=dmat, qdec=bcast(qdec), kdec=bcast(kdec), gc=gc)
    pp = dict(p, sg_ws=sg_ws[0], sg_bias=jnp.repeat(sg_bs[0].T, DK, axis=1))
    x_mid, s_prompt, w_up_b, w_down_b = _prompt_mixer(x_prompt, tabs_p, mk_b, mv_b, dec_p, pp,
                                                      [w_up[0], w_down[0]])
    p = dict(p, w_up=w_up_b, w_down=w_down_b)
    y_prompt, tail = _prompt_ffn(x_mid, p)

    n = bs * ls
    per = HEADS * EXP
    te = (bs // EXP) * per
    pos_s = PAST_LEN + (np.arange(te) & (ls - 1))
    tabs_s = (*_rope_tables(pos_s, 1.0), *_rope_tables(pos_s, scale))
    dmat4, qdec4, kdec4, gc4 = _decay(ls)
    kdec_rows = np.ascontiguousarray(np.broadcast_to(np.tile(kdec4.reshape(per), te // per)[:, None], (te, DK)))
    xs = jnp.swapaxes(x_sample, 0, 1).reshape(n, D_MODEL)
    pre = _sample_pre(xs, tabs_s, kdec_rows, p, sg_ws[0][:, :ls, :ls].reshape(-1), sg_bs[0][:, :ls].reshape(-1))
    qe, kre, kde, ve, ge, xqe, osg, vrows, gates = pre

    rb = SAMPLE_BB * per
    blk16 = np.einsum("hk,hls->hlks", np.eye(HEADS, dtype=np.float32), dmat4).reshape(per, per)
    dec_s = dict(gc=gc4,
                 dmat=np.kron(np.eye(SAMPLE_BB, dtype=np.float32), blk16),
                 qdec=np.ascontiguousarray(
                     np.broadcast_to(np.tile(qdec4.reshape(per), SAMPLE_BB)[:, None], (rb, DV))),
                 gn=jnp.tile(jnp.repeat(ret_gn_g[0], ls, axis=0), (SAMPLE_BB, 1)))
    orete, oxe, s_sample = _sample_state(
        (qe, kre, kde, ve, ge, xqe), state_ret[0].reshape(bs, HEADS * DK, DV),
        cache_mem_k[0].reshape(bs, MEM_LEN * HEADS, DK), cache_mem_v[0].reshape(bs, MEM_LEN * HEADS, DK), dec_s)
    xs_mid = _sample_merge(xs, orete, oxe, osg, gates, p)
    y_sample, c2a, c2b, c3a, c3b = _sample_ffn(xs_mid, jnp.swapaxes(state_conv[0], 0, 1), p)
    conv_s = jnp.stack([jnp.concatenate([c2a, c2b], -1), jnp.concatenate([c3a, c3b], -1)], axis=1)
    unpos = lambda a: jnp.swapaxes(a.reshape(ls, bs, a.shape[-1]), 0, 1)

    return (y_prompt, unpos(y_sample),
            s_prompt[None], tail[None, :, 6:8],
            mk.reshape(1, bp, MEM_LEN, HEADS, DK), mv.reshape(1, bp, MEM_LEN, HEADS, DK),
            s_sample.reshape(1, bs, HEADS, DK, DV), conv_s[None],
            unpos(vrows)[None])
```

```python
import functools

import numpy as np
import jax
import jax.numpy as jnp
from jax import lax
from jax.experimental import pallas as pl
from jax.experimental.pallas import tpu as pltpu

F32 = jnp.float32
BF16 = jnp.bfloat16

D_MODEL = 1024
HEADS = 4
DK = 128
DV = 256
QK_W = HEADS * DK
V_W = HEADS * DV
SG_W = 512
X_W = 512
MEM_LEN = 256
D_FF = 2816
CHUNK = 128
ROPE_BASE = 10000.0
EPS = 1e-6
PAST_LEN = 16384

C_Q, C_K, C_V, C_G, C_SU, C_SV, C_XQ, C_GT, C_END = 0, 512, 1024, 2048, 3072, 3584, 4096, 4608, 7680

PROMPT_TILE = 512
SAMPLE_BB = 8
EXP = HEADS
FFN_CW = 1408
VMEM_LIMIT = 56 * 1024 * 1024


def _rms(x, g):
    return x * lax.rsqrt(jnp.mean(x * x, axis=-1, keepdims=True) + EPS) * g


def _stdnorm(x):
    mu = jnp.mean(x, axis=-1, keepdims=True)
    xc = x - mu
    var = jnp.mean(xc * xc, axis=-1, keepdims=True)
    return xc * lax.rsqrt(var + EPS)


_GELU_C0 = np.float32(np.sqrt(2.0 / np.pi))
_GELU_C1 = np.float32(np.sqrt(2.0 / np.pi) * 0.044715)


def _gelu(x):
    hx = 0.5 * x
    return hx + hx * jnp.tanh(x * (_GELU_C0 + _GELU_C1 * (x * x)))


def _sigmoid(x):
    return 0.5 + 0.5 * jnp.tanh(0.5 * x)


def _silu(x):
    hx = 0.5 * x
    return hx + hx * jnp.tanh(hx)


def _softmax(s):
    e = jnp.exp(s - jnp.max(s, axis=-1, keepdims=True))
    return e * (1.0 / jnp.sum(e, axis=-1, keepdims=True))


def _mm(a, b):
    return jnp.dot(a.astype(BF16), b.astype(BF16), preferred_element_type=F32)


def _mm_nt(a, b):
    return lax.dot_general(a.astype(BF16), b.astype(BF16), (((1,), (1,)), ((), ())),
                           preferred_element_type=F32)


def _mm_tn(a, b):
    return lax.dot_general(a.astype(BF16), b.astype(BF16), (((0,), (0,)), ((), ())),
                           preferred_element_type=F32)


def _rope(x, cos, sin):
    return x * cos + pltpu.roll(x, DK // 2, 1) * sin


def _tril(w):
    r = lax.broadcasted_iota(jnp.int32, w.shape, 0)
    c = lax.broadcasted_iota(jnp.int32, w.shape, 1)
    return jnp.where(r >= c, w, 0.0)


def _conv_gate(zs_ref, n, cols_a, cols_b, cw_a, cb_a, cw_b, cb_b):
    def conv(cols, cw, cb):
        zz = zs_ref[0:8 + n, cols]
        z1 = pltpu.roll(zz, 1, 0)[8:]
        z2 = pltpu.roll(zz, 2, 0)[8:]
        return cb + cw[0:1] * z2 + cw[1:2] * z1 + cw[2:3] * zz[8:]
    return _gelu(conv(cols_a, cw_a, cb_a)) * conv(cols_b, cw_b, cb_b)


def _const_spec(shape):
    nd = len(shape)
    return pl.BlockSpec(shape, lambda *_: (0,) * nd, pipeline_mode=pl.Buffered(1))


def _smem_spec():
    return pl.BlockSpec(memory_space=pltpu.SMEM)


def _params(sem):
    return pltpu.CompilerParams(dimension_semantics=sem, vmem_limit_bytes=VMEM_LIMIT)


def _cast_blocks(refs):
    for src, dst in refs:
        dst[...] = src[...].astype(BF16)


def _cast_specs(weights, steps, flat_step):
    specs, shapes = [], []
    for w in weights:
        rows, cols = w.shape
        nblk = max(n for n in range(1, steps + 1) if steps % n == 0 and rows % (16 * n) == 0)
        specs.append(pl.BlockSpec((rows // nblk, cols), lambda *g, r=steps // nblk: (flat_step(*g) // r, 0)))
        shapes.append(jax.ShapeDtypeStruct(w.shape, BF16))
    return specs, shapes


def _memkv_kernel(mem_ref, g_ref, w_ref, *refs):
    ncast = (len(refs) - 4) // 2
    casts, (k_ref, v_ref, kb_ref, vb_ref) = refs[:ncast], refs[ncast:ncast + 4]
    kv = _mm(_rms(mem_ref[...], g_ref[...]), w_ref[...])
    k, v = kv[:, :X_W], kv[:, X_W:]
    for hd in range(HEADS):
        cols = slice(hd * DK, (hd + 1) * DK)
        k_ref[pl.ds(hd, MEM_LEN, stride=HEADS), :] = k[:, cols]
        v_ref[pl.ds(hd, MEM_LEN, stride=HEADS), :] = v[:, cols]
    kb_ref[...] = k.astype(BF16)
    vb_ref[...] = v.astype(BF16)
    _cast_blocks(zip(casts, refs[ncast + 4:]))


def _memkv(mem, g, w, cast_weights):
    b = mem.shape[0]
    blk = pl.BlockSpec((None, MEM_LEN, X_W), lambda i: (i, 0, 0))
    flat = pl.BlockSpec((None, MEM_LEN * HEADS, DK), lambda i: (i, 0, 0))
    cspec, cshape = _cast_specs(cast_weights, b, lambda i: i)
    return pl.pallas_call(
        _memkv_kernel,
        grid=(b,),
        in_specs=[pl.BlockSpec((None, MEM_LEN, D_MODEL), lambda i: (i, 0, 0)),
                  _const_spec((1, D_MODEL)), _const_spec((D_MODEL, 2 * X_W))] + cspec,
        out_specs=[flat, flat, blk, blk] + cspec,
        out_shape=[jax.ShapeDtypeStruct((b, MEM_LEN * HEADS, DK), F32)] * 2
        + [jax.ShapeDtypeStruct((b, MEM_LEN, X_W), BF16)] * 2 + cshape,
        compiler_params=_params(("arbitrary",)),
        name="mem_kv",
    )(mem, g, w, *cast_weights)


def _mixer_kernel(gc_ref, x_ref, cq_ref, sq_ref, ck_ref, sk_ref, mk_ref, mv_ref, ng_ref, win_ref, bg_ref,
                  gn_ref, lng_ref, ws_ref, sgb_ref, dmat_ref, qdec_ref, kdec_ref,
                  wr_ref, wsg_ref, wx_ref, wo_ref, wup_ref, wdn_ref,
                  xo_ref, s_ref, wup_o_ref, wdn_o_ref, oret_ref, osg_ref, ox_ref):
    tile = x_ref.shape[0]
    _cast_blocks(((wup_ref, wup_o_ref), (wdn_ref, wdn_o_ref)))

    @pl.when(pl.program_id(1) == 0)
    def _():
        s_ref[...] = jnp.zeros_like(s_ref)

    x = x_ref[...]
    h = _rms(x, ng_ref[...]).astype(BF16)

    def proj(a, b):
        return jnp.dot(h, win_ref[:, a:b], preferred_element_type=F32)

    heads, chunks = range(HEADS), range(tile // CHUNK)
    rows = [slice(c * CHUNK, (c + 1) * CHUNK) for c in chunks]
    kcols = [slice(hd * DK, (hd + 1) * DK) for hd in heads]
    vcols = [slice(hd * DV, (hd + 1) * DV) for hd in heads]

    su = proj(C_SU, C_SV)
    sv = proj(C_SV, C_XQ)
    q = proj(C_Q, C_K)
    k = proj(C_K, C_V)
    u = _gelu(su)
    vn = (_stdnorm(_gelu(sv)) * lng_ref[...]).astype(BF16)
    wsg = [_tril(ws_ref[g]).astype(BF16) for g in heads]
    vb = proj(C_V, C_G).astype(BF16)
    mixed = [[jnp.dot(wsg[g], vn[rows[c], kcols[g]], preferred_element_type=F32) + sgb_ref[:, kcols[g]]
              for c in chunks] for g in heads]
    gsil = _silu(proj(C_G, C_SU))
    for g in heads:
        osg_ref[:, kcols[g]] = (u[:, kcols[g]] * jnp.concatenate(mixed[g], axis=0)).astype(BF16)

    cq, sq, ck, sk = cq_ref[...], sq_ref[...], ck_ref[...], sk_ref[...]
    qr = [_rope(q[:, kcols[hd]], cq, sq) for hd in heads]
    kr = [_rope(k[:, kcols[hd]], ck, sk) for hd in heads]
    sc = [[_mm_nt(qr[hd][rows[c]], kr[hd][rows[c]]) for c in chunks] for hd in heads]
    upd = [[_mm_tn(kr[hd][rows[c]] * kdec_ref[hd], vb[rows[c], vcols[hd]]) for c in chunks] for hd in heads]
    xq = proj(C_XQ, C_GT)
    gmid = C_GT + (C_END - C_GT) // 2
    gt0 = proj(C_GT, gmid)
    states = []
    for hd in heads:
        st, before = s_ref[hd], []
        for c in chunks:
            before.append(st.astype(BF16))
            st = gc_ref[hd] * st + upd[hd][c]
        s_ref[hd] = st
        states.append(before)
    o = [[jnp.dot(jnp.concatenate([(sc[hd][c] * dmat_ref[hd]).astype(BF16),
                                   (qr[hd][rows[c]] * qdec_ref[hd]).astype(BF16)], axis=1),
                  jnp.concatenate([vb[rows[c], vcols[hd]], states[hd][c]], axis=0),
                  preferred_element_type=F32)
          for c in chunks] for hd in heads]
    sx = [_mm_nt(xq[:, kcols[hd]], mk_ref[:, kcols[hd]]) * np.float32(DK ** -0.5) for hd in heads]
    for hd in heads:
        on = _stdnorm(jnp.concatenate(o[hd], axis=0)) * gn_ref[:, vcols[hd]]
        oret_ref[:, vcols[hd]] = (gsil[:, vcols[hd]] * on).astype(BF16)
    gt1 = proj(gmid, C_END)
    px = [_softmax(sx[hd]) for hd in heads]
    for hd in heads:
        ox_ref[:, kcols[hd]] = _mm(px[hd], mv_ref[:, kcols[hd]]).astype(BF16)

    gates = _sigmoid(jnp.concatenate([gt0, gt1], axis=1) + bg_ref[...])
    merged = (gates[:, :D_MODEL] * jnp.dot(oret_ref[...], wr_ref[...], preferred_element_type=F32)
              + gates[:, D_MODEL:2 * D_MODEL] * jnp.dot(osg_ref[...], wsg_ref[...], preferred_element_type=F32)
              + gates[:, 2 * D_MODEL:] * jnp.dot(ox_ref[...], wx_ref[...], preferred_element_type=F32))
    xo_ref[...] = x + _mm(merged, wo_ref[...])


def _prompt_mixer(x, tabs, mk_b, mv_b, dec, p, cast_weights):
    b, l, _ = x.shape
    t = PROMPT_TILE
    nt = l // t
    cspec, cshape = _cast_specs(cast_weights, b * nt, lambda i, j: i * nt + j)
    tok = lambda w: pl.BlockSpec((None, t, w), lambda i, j: (i, j, 0))
    tab = pl.BlockSpec((t, DK), lambda i, j: (j, 0))
    mem = pl.BlockSpec((None, MEM_LEN, X_W), lambda i, j: (i, 0, 0))
    consts = [p["norm_mix_g"], p["w_in"], p["b_gate"], p["gn_g"], p["sg_ln_g"], p["sg_ws"], p["sg_bias"],
              dec["dmat"], dec["qdec"], dec["kdec"], p["w_br_ret"], p["w_br_sg"], p["w_br_x"], p["w_o"]]
    return pl.pallas_call(
        _mixer_kernel,
        grid=(b, l // t),
        in_specs=[_smem_spec(), tok(D_MODEL), tab, tab, tab, tab, mem, mem]
        + [_const_spec(c.shape) for c in consts] + cspec,
        out_specs=[tok(D_MODEL), pl.BlockSpec((None, HEADS, DK, DV), lambda i, j: (i, 0, 0, 0))] + cspec,
        out_shape=[jax.ShapeDtypeStruct((b, l, D_MODEL), F32),
                   jax.ShapeDtypeStruct((b, HEADS, DK, DV), F32)] + cshape,
        scratch_shapes=[pltpu.VMEM((t, V_W), BF16), pltpu.VMEM((t, SG_W), BF16), pltpu.VMEM((t, X_W), BF16)],
        compiler_params=_params(("arbitrary", "arbitrary")),
        name="prompt_mixer",
    )(dec["gc"], x, *tabs, mk_b, mv_b, *consts, *cast_weights)


def _ffn_kernel(x_ref, ng_ref, wup_ref, cw_ref, cb_ref, wdn_ref, nf_ref, y_ref, tail_ref, zs_ref):
    tile = x_ref.shape[0]
    first = pl.program_id(1) == 0

    @pl.when(first)
    def _():
        zs_ref[0:8, :] = jnp.zeros((8, 2 * D_FF), F32)

    @pl.when(jnp.logical_not(first))
    def _():
        zs_ref[0:8, :] = zs_ref[tile:tile + 8, :]

    x = x_ref[...]
    zs_ref[8:8 + tile, :] = _mm(_rms(x, ng_ref[...]), wup_ref[...])
    ca, cb = slice(0, D_FF), slice(D_FF, 2 * D_FF)
    gate = _conv_gate(zs_ref, tile, ca, cb, cw_ref[:, ca], cb_ref[:, ca], cw_ref[:, cb], cb_ref[:, cb])
    y = x + _mm(gate, wdn_ref[...])
    y_ref[...] = _rms(y, nf_ref[...])
    tail_ref[...] = zs_ref[tile:tile + 8, :]


def _prompt_ffn(x, p):
    b, l, _ = x.shape
    t = PROMPT_TILE
    tok = pl.BlockSpec((None, t, D_MODEL), lambda i, j: (i, j, 0))
    consts = [p["norm_ffn_g"], p["w_up"], p["conv_w"], p["conv_b"], p["w_down"], p["norm_final_g"]]
    return pl.pallas_call(
        _ffn_kernel,
        grid=(b, l // t),
        in_specs=[tok] + [_const_spec(c.shape) for c in consts],
        out_specs=[tok, pl.BlockSpec((None, 8, 2 * D_FF), lambda i, j: (i, 0, 0))],
        out_shape=[jax.ShapeDtypeStruct((b, l, D_MODEL), F32), jax.ShapeDtypeStruct((b, 8, 2 * D_FF), F32)],
        scratch_shapes=[pltpu.VMEM((t + 8, 2 * D_FF), F32)],
        compiler_params=_params(("arbitrary", "arbitrary")),
        name="prompt_ffn",
    )(x, *consts)


def _row_head(n):
    return (lax.broadcasted_iota(jnp.int32, (n, 1), 0) >> 2) & (HEADS - 1)


def _sample_pre_kernel(ws_ref, bs_ref, x_ref, cq_ref, sq_ref, ck_ref, sk_ref, kdec_ref, ng_ref, win_ref, bg_ref,
                       lng_ref,
                       qe_ref, kre_ref, kde_ref, ve_ref, ge_ref, xqe_ref, osg_ref, vrows_ref, gates_ref,
                       h_ref, vn_ref):
    n = x_ref.shape[0]
    nb = n // EXP
    ne = qe_ref.shape[0]
    i = pl.program_id(0)

    @pl.when(i == 0)
    def _():
        h_ref[...] = _rms(x_ref[...], ng_ref[...]).astype(BF16)
        vn_ref[...] = jnp.zeros_like(vn_ref)

    r = lax.broadcasted_iota(jnp.int32, (ne, n), 0)
    c = lax.broadcasted_iota(jnp.int32, (ne, n), 1)
    rep = jnp.where(c == (r & (EXP - 1)) * nb + i * (ne // (HEADS * EXP)) + (r >> 4), 1.0, 0.0).astype(BF16)
    he = jnp.dot(rep, h_ref[...], preferred_element_type=F32).astype(BF16)
    hh = _row_head(ne)
    h = h_ref[pl.ds(pl.multiple_of(i * nb, nb), nb), :]

    def proj(hm, a, b):
        return jnp.dot(hm, win_ref[:, a:b], preferred_element_type=F32)

    q, k = proj(he, C_Q, C_K), proj(he, C_K, C_V)
    xq = proj(he, C_XQ, C_GT)
    cq, sq, ck, sk, kdec = cq_ref[...], sq_ref[...], ck_ref[...], sk_ref[...], kdec_ref[...]
    for hd in range(HEADS):
        cols = slice(hd * DK, (hd + 1) * DK)
        own = hh == hd
        kr = _rope(k[:, cols], ck, sk)
        qe_ref[:, cols] = jnp.where(own, _rope(q[:, cols], cq, sq), 0.0).astype(BF16)
        kre_ref[:, cols] = jnp.where(own, kr, 0.0).astype(BF16)
        kde_ref[:, cols] = jnp.where(own, kr * kdec, 0.0).astype(BF16)
    v = proj(he, C_V, C_G)
    g = proj(he, C_G, C_SU)
    ve = jnp.zeros((ne, DV), F32)
    ge = jnp.zeros((ne, DV), F32)
    xqc = jnp.zeros((ne, DK), F32)
    for hd in range(HEADS):
        cols = slice(hd * DV, (hd + 1) * DV)
        own = hh == hd
        ve = jnp.where(own, v[:, cols], ve)
        ge = jnp.where(own, g[:, cols], ge)
        xqc = jnp.where(own, xq[:, hd * DK:(hd + 1) * DK], xqc)
    xqe_ref[...] = xqc.astype(BF16)
    ve_ref[...] = ve.astype(BF16)
    ge_ref[...] = _silu(ge)

    u = _gelu(proj(h, C_SU, C_SV))
    vn = _stdnorm(_gelu(proj(h, C_SV, C_XQ))) * lng_ref[...]
    vrows_ref[...] = vn
    vn_ref[i] = vn
    for g in range(HEADS):
        cols = slice(g * DK, (g + 1) * DK)
        mixed = jnp.full((nb, DK), bs_ref[g * EXP + i], F32)
        for s in range(EXP):
            w = jnp.where(s <= i, ws_ref[(g * EXP + i) * EXP + s], 0.0)
            mixed = mixed + w * vn_ref[s, :, cols]
        osg_ref[:, cols] = (u[:, cols] * mixed).astype(BF16)
    gates_ref[...] = _sigmoid(proj(h, C_GT, C_END) + bg_ref[...])


def _sample_pre(x, tabs, kdec_rows, p, ws4, bs4):
    n = x.shape[0]
    t = n // EXP
    te = (t // EXP) * HEADS * EXP
    consts = [p["norm_mix_g"], p["w_in"], p["b_gate"], p["sg_ln_g"]]
    rowblk = lambda rows, w: pl.BlockSpec((rows, w), lambda i: (i, 0))
    outs = [(te, QK_W, BF16), (te, QK_W, BF16), (te, QK_W, BF16), (te, DV, BF16), (te, DV, F32),
            (te, DK, BF16), (t, SG_W, BF16), (t, SG_W, F32), (t, 3 * D_MODEL, F32)]
    return pl.pallas_call(
        _sample_pre_kernel,
        grid=(EXP,),
        in_specs=[_smem_spec(), _smem_spec(), _const_spec(x.shape)] + [_const_spec((te, DK))] * 5
        + [_const_spec(c.shape) for c in consts],
        out_specs=[rowblk(r, w) for r, w, _ in outs],
        out_shape=[jax.ShapeDtypeStruct((r * EXP, w), d) for r, w, d in outs],
        scratch_shapes=[pltpu.VMEM((n, D_MODEL), BF16), pltpu.VMEM((EXP, t, SG_W), F32)],
        compiler_params=_params(("arbitrary",)),
        name="sample_pre",
    )(ws4, bs4, x, *tabs, kdec_rows, *consts)


def _sample_state_kernel(gc_ref, qe_ref, kre_ref, kde_ref, ve_ref, ge_ref, xqe_ref, s_ref, mk_ref, mv_ref,
                         dmat_ref, qdec_ref, gn_ref,
                         oret_ref, ox_ref, so_ref):
    rows_n = qe_ref.shape[0]
    per = HEADS * EXP
    batches = range(rows_n // per)
    rows = [slice(b * per, (b + 1) * per) for b in batches]
    q, ve, xq = qe_ref[...], ve_ref[...], xqe_ref[...]
    sc = _mm_nt(q, kre_ref[...]) * dmat_ref[...]
    inner = jnp.dot(sc.astype(BF16), ve, preferred_element_type=F32)
    kd = kde_ref[...]
    rowb = lax.broadcasted_iota(jnp.int32, (rows_n, 1), 0) >> 4
    cross = jnp.concatenate(
        [jnp.dot(q[rows[b]], s_ref[b].astype(BF16), preferred_element_type=F32) for b in batches], axis=0)
    upd = [_mm_tn(kd, jnp.where(rowb == b, ve, jnp.zeros_like(ve))) for b in batches]
    for b in batches:
        for hd in range(HEADS):
            hr = slice(hd * DK, (hd + 1) * DK)
            so_ref[b, hr, :] = gc_ref[hd] * s_ref[b, hr, :] + upd[b][hr]
    hh = _row_head(rows_n)
    own_col = (lax.broadcasted_iota(jnp.int32, (rows_n, MEM_LEN * HEADS), 1) & (HEADS - 1)) == hh
    sx = jnp.concatenate([_mm_nt(xq[rows[b]], mk_ref[b]) for b in batches], axis=0) * np.float32(DK ** -0.5)
    p = _softmax(jnp.where(own_col, sx, np.float32(-1e30))).astype(BF16)
    oxs = jnp.concatenate([_mm(p[rows[b]], mv_ref[b]) for b in batches], axis=0)
    o = inner + cross * qdec_ref[...]
    og = ge_ref[...] * (_stdnorm(o) * gn_ref[...])
    for hd in range(HEADS):
        oret_ref[:, hd * DV:(hd + 1) * DV] = jnp.where(hh == hd, og, 0.0)
        ox_ref[:, hd * DK:(hd + 1) * DK] = jnp.where(hh == hd, oxs, 0.0)


def _sample_state(pre, state, mk, mv, dec):
    qe, kre, kde, ve, ge, xqe = pre
    nb = state.shape[0]
    bb = SAMPLE_BB
    rb = bb * HEADS * EXP
    rowblk = lambda w: pl.BlockSpec((rb, w), lambda i: (i, 0))
    batblk = lambda a, c: pl.BlockSpec((bb, a, c), lambda i: (i, 0, 0))
    ne = qe.shape[0]
    return pl.pallas_call(
        _sample_state_kernel,
        grid=(nb // bb,),
        in_specs=[_smem_spec(), rowblk(QK_W), rowblk(QK_W), rowblk(QK_W), rowblk(DV), rowblk(DV), rowblk(DK),
                  batblk(HEADS * DK, DV), batblk(MEM_LEN * HEADS, DK), batblk(MEM_LEN * HEADS, DK),
                  _const_spec((rb, rb)), _const_spec((rb, DV)), _const_spec((rb, DV))],
        out_specs=[rowblk(V_W), rowblk(X_W), batblk(HEADS * DK, DV)],
        out_shape=[jax.ShapeDtypeStruct((ne, V_W), F32), jax.ShapeDtypeStruct((ne, X_W), F32),
                   jax.ShapeDtypeStruct(state.shape, F32)],
        compiler_params=_params(("arbitrary",)),
        name="sample_state",
    )(dec["gc"], qe, kre, kde, ve, ge, xqe, state, mk, mv, dec["dmat"], dec["qdec"], dec["gn"])


def _sample_merge_kernel(x_ref, orete_ref, oxe_ref, osg_ref, gates_ref, wr_ref, wsg_ref, wx_ref, wo_ref, xo_ref):
    n = x_ref.shape[0]
    ne = n * EXP
    r = lax.broadcasted_iota(jnp.int32, (n, ne), 0)
    c = lax.broadcasted_iota(jnp.int32, (n, ne), 1)
    col = jnp.where(r == (c & (EXP - 1)) * (n // EXP) + (c >> 4), 1.0, 0.0).astype(BF16)
    oret = jnp.dot(col, orete_ref[...].astype(BF16), preferred_element_type=F32)
    ox = jnp.dot(col, oxe_ref[...].astype(BF16), preferred_element_type=F32)
    gates = gates_ref[...]
    merged = (gates[:, :D_MODEL] * _mm(oret, wr_ref[...])
              + gates[:, D_MODEL:2 * D_MODEL] * jnp.dot(osg_ref[...], wsg_ref[...], preferred_element_type=F32)
              + gates[:, 2 * D_MODEL:] * _mm(ox, wx_ref[...]))
    xo_ref[...] = x_ref[...] + _mm(merged, wo_ref[...])


def _sample_merge(x, orete, oxe, osg, gates, p):
    ins = [x, orete, oxe, osg, gates, p["w_br_ret"], p["w_br_sg"], p["w_br_x"], p["w_o"]]
    return pl.pallas_call(
        _sample_merge_kernel,
        grid=(1,),
        in_specs=[_const_spec(a.shape) for a in ins],
        out_specs=pl.BlockSpec(x.shape, lambda i: (0, 0)),
        out_shape=jax.ShapeDtypeStruct(x.shape, F32),
        compiler_params=_params(("arbitrary",)),
        name="sample_merge",
    )(*ins)


def _sample_ffn_kernel(x_ref, ng_ref, wa_ref, wb_ref, cwa_ref, cwb_ref, cba_ref, cbb_ref, s0a_ref, s0b_ref,
                       s1a_ref, s1b_ref, wdn_ref, nf_ref,
                       y_ref, c2a_ref, c2b_ref, c3a_ref, c3b_ref, h_ref, acc_ref):
    nb = x_ref.shape[0] // EXP
    j = pl.program_id(0)

    @pl.when(j == 0)
    def _():
        h_ref[...] = _rms(x_ref[...], ng_ref[...]).astype(BF16)
        acc_ref[...] = jnp.zeros_like(acc_ref)

    h = h_ref[...]

    def conv(w_ref, cw_ref, cb_ref, s0_ref, s1_ref, c2_ref, c3_ref):
        z = jnp.dot(h, w_ref[...], preferred_element_type=F32)
        zp = [s0_ref[...], s1_ref[...]] + [z[l * nb:(l + 1) * nb] for l in range(EXP)]
        c2_ref[...] = zp[EXP]
        c3_ref[...] = zp[EXP + 1]
        cw, cb = cw_ref[...], cb_ref[...]
        return jnp.concatenate([cb + cw[0:1] * zp[l] + cw[1:2] * zp[l + 1] + cw[2:3] * zp[l + 2]
                                for l in range(EXP)], axis=0)

    a = conv(wa_ref, cwa_ref, cba_ref, s0a_ref, s1a_ref, c2a_ref, c3a_ref)
    b = conv(wb_ref, cwb_ref, cbb_ref, s0b_ref, s1b_ref, c2b_ref, c3b_ref)
    acc_ref[...] += _mm(_gelu(a) * b, wdn_ref[...])

    @pl.when(j == pl.num_programs(0) - 1)
    def _():
        y_ref[...] = _rms(x_ref[...] + acc_ref[...], nf_ref[...])


def _sample_ffn(x, sc, p):
    n = x.shape[0]
    nb = n // EXP
    cw = FFN_CW
    nch = D_FF // cw
    ca = lambda rows: pl.BlockSpec((rows, cw), lambda j: (0, j))
    cb = lambda rows: pl.BlockSpec((rows, cw), lambda j: (0, nch + j))
    sa = lambda k: pl.BlockSpec((None, nb, cw), lambda j: (k, 0, j))
    sb = lambda k: pl.BlockSpec((None, nb, cw), lambda j: (k, 0, nch + j))
    full = pl.BlockSpec((n, D_MODEL), lambda j: (0, 0))
    vec = pl.BlockSpec((1, D_MODEL), lambda j: (0, 0))
    return pl.pallas_call(
        _sample_ffn_kernel,
        grid=(nch,),
        in_specs=[full, vec, ca(D_MODEL), cb(D_MODEL), ca(3), cb(3), ca(1), cb(1), sa(0), sb(0), sa(1), sb(1),
                  pl.BlockSpec((cw, D_MODEL), lambda j: (j, 0)), vec],
        out_specs=[full] + [ca(nb)] * 4,
        out_shape=[jax.ShapeDtypeStruct((n, D_MODEL), F32)] + [jax.ShapeDtypeStruct((nb, D_FF), F32)] * 4,
        scratch_shapes=[pltpu.VMEM((n, D_MODEL), BF16), pltpu.VMEM((n, D_MODEL), F32)],
        compiler_params=_params(("arbitrary",)),
        name="sample_ffn",
    )(x, p["norm_ffn_g"], p["w_up"], p["w_up"], p["conv_w"], p["conv_w"], p["conv_b"], p["conv_b"],
      sc, sc, sc, sc, p["w_down"], p["norm_final_g"])


def _rope_tables(pos, scale):
    inv = ROPE_BASE ** (-np.arange(0, DK, 2, dtype=np.float64) / DK)
    ang = np.asarray(pos, np.float64)[:, None] * inv[None, :]
    cos, sin = np.cos(ang), np.sin(ang)
    return (np.concatenate([cos, cos], -1) * scale).astype(np.float32), \
        (np.concatenate([-sin, sin], -1) * scale).astype(np.float32)


def _decay(chunk):
    lg = np.log1p(-np.exp2(-5.0 - np.arange(HEADS, dtype=np.float64)))
    n = np.arange(chunk, dtype=np.float64)
    diff = n[:, None] - n[None, :]
    dmat = np.where(diff >= 0, np.exp(np.maximum(diff, 0.0)[None] * lg[:, None, None]), 0.0)
    qdec = np.exp((n + 1.0)[None, :] * lg[:, None])
    kdec = np.exp((chunk - 1.0 - n)[None, :] * lg[:, None])
    f32 = lambda a: a.astype(np.float32)
    return f32(dmat), f32(qdec), f32(kdec), f32(np.exp(chunk * lg))


def kernel(x_prompt, x_sample, mem_prompt, state_ret, state_conv, cache_mem_k, cache_mem_v, norm_mix_g, w_in,
           b_gate, ret_gn_g, sg_ln_g, sg_ws, sg_bs, mem_norm_g, w_mem_kv, w_br_ret, w_br_sg, w_br_x, w_o,
           norm_ffn_g, w_up, conv_w, conv_b, w_down, norm_final_g):
    bp, lp, _ = x_prompt.shape
    bs, ls, _ = x_sample.shape
    assert state_ret.shape[0] == 1 and ls == EXP and lp % PROMPT_TILE == 0 and bs % SAMPLE_BB == 0
    assert bs == CHUNK
    row = lambda a: a.reshape(1, -1)
    scale = DK ** -0.5

    mk, mv, mk_b, mv_b, w_in_b, w_br_ret_b, w_br_sg_b, w_br_x_b, w_o_b = _memkv(
        mem_prompt, row(mem_norm_g[0]), w_mem_kv[0], [w_in[0], w_br_ret[0], w_br_sg[0], w_br_x[0], w_o[0]])
    p = dict(norm_mix_g=row(norm_mix_g[0]), w_in=w_in_b, b_gate=row(b_gate[0]),
             gn_g=row(ret_gn_g[0]), sg_ln_g=row(sg_ln_g[0]),
             w_br_ret=w_br_ret_b, w_br_sg=w_br_sg_b, w_br_x=w_br_x_b, w_o=w_o_b,
             norm_ffn_g=row(norm_ffn_g[0]), conv_w=conv_w[0], conv_b=row(conv_b[0]),
             norm_final_g=row(norm_final_g))
    pos_p = np.arange(lp)
    tabs_p = (*_rope_tables(pos_p, 1.0), *_rope_tables(pos_p, scale))
    dmat, qdec, kdec, gc = _decay(CHUNK)
    bcast = lambda a: np.ascontiguousarray(np.broadcast_to(a[:, :, None], (HEADS, CHUNK, DK)))
    dec_p = dict(dmat=dmat, qdec=bcast(qdec), kdec=bcast(kdec), gc=gc)
    pp = dict(p, sg_ws=sg_ws[0], sg_bias=jnp.repeat(sg_bs[0].T, DK, axis=1))
    x_mid, s_prompt, w_up_b, w_down_b = _prompt_mixer(x_prompt, tabs_p, mk_b, mv_b, dec_p, pp,
                                                      [w_up[0], w_down[0]])
    p = dict(p, w_up=w_up_b, w_down=w_down_b)
    y_prompt, tail = _prompt_ffn(x_mid, p)

    n = bs * ls
    per = HEADS * EXP
    te = (bs // EXP) * per
    pos_s = PAST_LEN + (np.arange(te) & (ls - 1))
    tabs_s = (*_rope_tables(pos_s, 1.0), *_rope_tables(pos_s, scale))
    dmat4, qdec4, kdec4, gc4 = _decay(ls)
    kdec_rows = np.ascontiguousarray(np.broadcast_to(np.tile(kdec4.reshape(per), te // per)[:, None], (te, DK)))
    xs = jnp.swapaxes(x_sample, 0, 1).reshape(n, D_MODEL)
    pre = _sample_pre(xs, tabs_s, kdec_rows, p, sg_ws[0][:, :ls, :ls].reshape(-1), sg_bs[0][:, :ls].reshape(-1))
    qe, kre, kde, ve, ge, xqe, osg, vrows, gates = pre

    rb = SAMPLE_BB * per
    blk16 = np.einsum("hk,hls->hlks", np.eye(HEADS, dtype=np.float32), dmat4).reshape(per, per)
    dec_s = dict(gc=gc4,
                 dmat=np.kron(np.eye(SAMPLE_BB, dtype=np.float32), blk16),
                 qdec=np.ascontiguousarray(
                     np.broadcast_to(np.tile(qdec4.reshape(per), SAMPLE_BB)[:, None], (rb, DV))),
                 gn=jnp.tile(jnp.repeat(ret_gn_g[0], ls, axis=0), (SAMPLE_BB, 1)))
    orete, oxe, s_sample = _sample_state(
        (qe, kre, kde, ve, ge, xqe), state_ret[0].reshape(bs, HEADS * DK, DV),
        cache_mem_k[0].reshape(bs, MEM_LEN * HEADS, DK), cache_mem_v[0].reshape(bs, MEM_LEN * HEADS, DK), dec_s)
    xs_mid = _sample_merge(xs, orete, oxe, osg, gates, p)
    y_sample, c2a, c2b, c3a, c3b = _sample_ffn(xs_mid, jnp.swapaxes(state_conv[0], 0, 1), p)
    conv_s = jnp.stack([jnp.concatenate([c2a, c2b], -1), jnp.concatenate([c3a, c3b], -1)], axis=1)
    unpos = lambda a: jnp.swapaxes(a.reshape(ls, bs, a.shape[-1]), 0, 1)

    return (y_prompt, unpos(y_sample),
            s_prompt[None], tail[None, :, 6:8],
            mk.reshape(1, bp, MEM_LEN, HEADS, DK), mv.reshape(1, bp, MEM_LEN, HEADS, DK),
            s_sample.reshape(1, bs, HEADS, DK, DV), conv_s[None],
            unpos(vrows)[None])
```

```python
import functools

import numpy as np
import jax
import jax.numpy as jnp
from jax import lax
from jax.experimental import pallas as pl
from jax.experimental.pallas import tpu as pltpu

F32 = jnp.float32
BF16 = jnp.bfloat16

D_MODEL = 1024
HEADS = 4
DK = 128
DV = 256
QK_W = HEADS * DK
V_W = HEADS * DV
SG_W = 512
X_W = 512
MEM_LEN = 256
D_FF = 2816
CHUNK = 128
ROPE_BASE = 10000.0
EPS = 1e-6
PAST_LEN = 16384

C_Q, C_K, C_V, C_G, C_SU, C_SV, C_XQ, C_GT, C_END = 0, 512, 1024, 2048, 3072, 3584, 4096, 4608, 7680

PROMPT_TILE = 512
EXP = HEADS
FFN_CW = 1408
VMEM_BYTES_V7X = 64 * 1024 * 1024
VMEM_LIMIT = VMEM_BYTES_V7X - 3 * 1024 * 1024


def _rms(x, g):
    return x * lax.rsqrt(jnp.mean(x * x, axis=-1, keepdims=True) + EPS) * g


def _stdnorm(x):
    mu = jnp.mean(x, axis=-1, keepdims=True)
    xc = x - mu
    var = jnp.mean(xc * xc, axis=-1, keepdims=True)
    return xc * lax.rsqrt(var + EPS)


_GELU_C0 = np.float32(np.sqrt(2.0 / np.pi))
_GELU_C1 = np.float32(np.sqrt(2.0 / np.pi) * 0.044715)


def _gelu_tanh(x):
    return jnp.tanh(x * (_GELU_C0 + _GELU_C1 * (x * x)))


def _gelu(x):
    hx = 0.5 * x
    return hx + hx * _gelu_tanh(x)


def _sigmoid(x):
    return 0.5 + 0.5 * jnp.tanh(0.5 * x)


def _silu(x):
    hx = 0.5 * x
    return hx + hx * jnp.tanh(hx)


def _softmax(s):
    e = jnp.exp(s - jnp.max(s, axis=-1, keepdims=True))
    return e * (1.0 / jnp.sum(e, axis=-1, keepdims=True))


def _mm(a, b):
    return jnp.dot(a.astype(BF16), b.astype(BF16), preferred_element_type=F32)


def _mm_nt(a, b):
    return lax.dot_general(a.astype(BF16), b.astype(BF16), (((1,), (1,)), ((), ())),
                           preferred_element_type=F32)


def _mm_tn(a, b):
    return lax.dot_general(a.astype(BF16), b.astype(BF16), (((0,), (0,)), ((), ())),
                           preferred_element_type=F32)


def _rope(x, cos, sin):
    return x * cos + pltpu.roll(x, DK // 2, 1) * sin


def _tril(w):
    r = lax.broadcasted_iota(jnp.int32, w.shape, 0)
    c = lax.broadcasted_iota(jnp.int32, w.shape, 1)
    return jnp.where(r >= c, w, 0.0)


def _conv_gate(zs_ref, n, cols_a, cols_b, cw_a, cb_a, cw_b, cb_b):
    def conv(cols, cw, cb):
        zz = zs_ref[0:8 + n, cols]
        z1 = pltpu.roll(zz, 1, 0)[8:]
        z2 = pltpu.roll(zz, 2, 0)[8:]
        return cb + cw[0:1] * z2 + cw[1:2] * z1 + cw[2:3] * zz[8:]
    a = conv(cols_a, cw_a, cb_a)
    return (a + a * _gelu_tanh(a)) * conv(cols_b, 0.5 * cw_b, 0.5 * cb_b)


def _const_spec(shape):
    nd = len(shape)
    return pl.BlockSpec(shape, lambda *_: (0,) * nd, pipeline_mode=pl.Buffered(1))


def _smem_spec():
    return pl.BlockSpec(memory_space=pltpu.SMEM)


def _params(sem):
    return pltpu.CompilerParams(dimension_semantics=sem, vmem_limit_bytes=VMEM_LIMIT)


def _cast_blocks(refs):
    for src, dst in refs:
        dst[...] = src[...].astype(BF16)


def _cast_specs(weights, steps, flat_step):
    specs, shapes = [], []
    for w in weights:
        rows, cols = w.shape
        nblk = max(n for n in range(1, steps + 1) if steps % n == 0 and rows % (16 * n) == 0)
        specs.append(pl.BlockSpec((rows // nblk, cols), lambda *g, r=steps // nblk: (flat_step(*g) // r, 0)))
        shapes.append(jax.ShapeDtypeStruct(w.shape, BF16))
    return specs, shapes


def _memkv_kernel(mem_ref, g_ref, w_ref, *refs):
    ncast = (len(refs) - 4) // 2
    casts, (k_ref, v_ref, kb_ref, vb_ref) = refs[:ncast], refs[ncast:ncast + 4]
    kv = _mm(_rms(mem_ref[...], g_ref[...]), w_ref[...])
    k, v = kv[:, :X_W], kv[:, X_W:]
    for hd in range(HEADS):
        cols = slice(hd * DK, (hd + 1) * DK)
        k_ref[pl.ds(hd, MEM_LEN, stride=HEADS), :] = k[:, cols]
        v_ref[pl.ds(hd, MEM_LEN, stride=HEADS), :] = v[:, cols]
    kb_ref[...] = k.astype(BF16)
    vb_ref[...] = v.astype(BF16)
    _cast_blocks(zip(casts, refs[ncast + 4:]))


def _memkv(mem, g, w, cast_weights):
    b = mem.shape[0]
    blk = pl.BlockSpec((None, MEM_LEN, X_W), lambda i: (i, 0, 0))
    flat = pl.BlockSpec((None, MEM_LEN * HEADS, DK), lambda i: (i, 0, 0))
    cspec, cshape = _cast_specs(cast_weights, b, lambda i: i)
    return pl.pallas_call(
        _memkv_kernel,
        grid=(b,),
        in_specs=[pl.BlockSpec((None, MEM_LEN, D_MODEL), lambda i: (i, 0, 0)),
                  _const_spec((1, D_MODEL)), _const_spec((D_MODEL, 2 * X_W))] + cspec,
        out_specs=[flat, flat, blk, blk] + cspec,
        out_shape=[jax.ShapeDtypeStruct((b, MEM_LEN * HEADS, DK), F32)] * 2
        + [jax.ShapeDtypeStruct((b, MEM_LEN, X_W), BF16)] * 2 + cshape,
        compiler_params=_params(("arbitrary",)),
        name="mem_kv",
    )(mem, g, w, *cast_weights)


def _mixer_kernel(gc_ref, x_ref, cq_ref, sq_ref, ck_ref, sk_ref, mk_ref, mv_ref, ng_ref, win_ref, bg_ref,
                  gn_ref, lng_ref, ws_ref, sgb_ref, dmat_ref, qdec_ref, kdec_ref,
                  wr_ref, wsg_ref, wx_ref, wo_ref, wup_ref, wdn_ref,
                  xo_ref, s_ref, wup_o_ref, wdn_o_ref, oret_ref, osg_ref, ox_ref):
    tile = x_ref.shape[0]
    _cast_blocks(((wup_ref, wup_o_ref), (wdn_ref, wdn_o_ref)))

    @pl.when(pl.program_id(1) == 0)
    def _():
        s_ref[...] = jnp.zeros_like(s_ref)

    x = x_ref[...]
    h = _rms(x, ng_ref[...]).astype(BF16)

    def proj(a, b):
        return jnp.dot(h, win_ref[:, a:b], preferred_element_type=F32)

    heads, chunks = range(HEADS), range(tile // CHUNK)
    rows = [slice(c * CHUNK, (c + 1) * CHUNK) for c in chunks]
    kcols = [slice(hd * DK, (hd + 1) * DK) for hd in heads]
    vcols = [slice(hd * DV, (hd + 1) * DV) for hd in heads]

    su = proj(C_SU, C_SV)
    sv = proj(C_SV, C_XQ)
    q = proj(C_Q, C_K)
    k = proj(C_K, C_V)
    u = _gelu(su)
    vn = (_stdnorm(_gelu(sv)) * lng_ref[...]).astype(BF16)
    wsg = [_tril(ws_ref[g]).astype(BF16) for g in heads]
    vb = proj(C_V, C_G).astype(BF16)
    mixed = [[jnp.dot(wsg[g], vn[rows[c], kcols[g]], preferred_element_type=F32) + sgb_ref[:, kcols[g]]
              for c in chunks] for g in heads]
    gsil = _silu(proj(C_G, C_SU))
    for g in heads:
        osg_ref[:, kcols[g]] = (u[:, kcols[g]] * jnp.concatenate(mixed[g], axis=0)).astype(BF16)

    cq, sq, ck, sk = cq_ref[...], sq_ref[...], ck_ref[...], sk_ref[...]
    qr = [_rope(q[:, kcols[hd]], cq, sq) for hd in heads]
    kr = [_rope(k[:, kcols[hd]], ck, sk) for hd in heads]
    sc = [[_mm_nt(qr[hd][rows[c]], kr[hd][rows[c]]) for c in chunks] for hd in heads]
    upd = [[_mm_tn(kr[hd][rows[c]] * kdec_ref[hd], vb[rows[c], vcols[hd]]) for c in chunks] for hd in heads]
    xq = proj(C_XQ, C_GT)
    gmid = C_GT + (C_END - C_GT) // 2
    gt0 = proj(C_GT, gmid)
    states = []
    for hd in heads:
        st, before = s_ref[hd], []
        for c in chunks:
            before.append(st.astype(BF16))
            st = gc_ref[hd] * st + upd[hd][c]
        s_ref[hd] = st
        states.append(before)
    o = [[jnp.dot(jnp.concatenate([(sc[hd][c] * dmat_ref[hd]).astype(BF16),
                                   (qr[hd][rows[c]] * qdec_ref[hd]).astype(BF16)], axis=1),
                  jnp.concatenate([vb[rows[c], vcols[hd]], states[hd][c]], axis=0),
                  preferred_element_type=F32)
          for c in chunks] for hd in heads]
    sx = [_mm_nt(xq[:, kcols[hd]], mk_ref[:, kcols[hd]]) * np.float32(DK ** -0.5) for hd in heads]
    for hd in heads:
        on = _stdnorm(jnp.concatenate(o[hd], axis=0)) * gn_ref[:, vcols[hd]]
        oret_ref[:, vcols[hd]] = (gsil[:, vcols[hd]] * on).astype(BF16)
    gt1 = proj(gmid, C_END)
    px = [_softmax(sx[hd]) for hd in heads]
    for hd in heads:
        ox_ref[:, kcols[hd]] = _mm(px[hd], mv_ref[:, kcols[hd]]).astype(BF16)

    gates = _sigmoid(jnp.concatenate([gt0, gt1], axis=1) + bg_ref[...])
    merged = (gates[:, :D_MODEL] * jnp.dot(oret_ref[...], wr_ref[...], preferred_element_type=F32)
              + gates[:, D_MODEL:2 * D_MODEL] * jnp.dot(osg_ref[...], wsg_ref[...], preferred_element_type=F32)
              + gates[:, 2 * D_MODEL:] * jnp.dot(ox_ref[...], wx_ref[...], preferred_element_type=F32))
    xo_ref[...] = x + _mm(merged, wo_ref[...])


def _prompt_mixer(x, tabs, mk_b, mv_b, dec, p, cast_weights):
    b, l, _ = x.shape
    t = PROMPT_TILE
    nt = l // t
    cspec, cshape = _cast_specs(cast_weights, b * nt, lambda i, j: i * nt + j)
    tok = lambda w: pl.BlockSpec((None, t, w), lambda i, j: (i, j, 0))
    tab = pl.BlockSpec((t, DK), lambda i, j: (j, 0))
    mem = pl.BlockSpec((None, MEM_LEN, X_W), lambda i, j: (i, 0, 0))
    consts = [p["norm_mix_g"], p["w_in"], p["b_gate"], p["gn_g"], p["sg_ln_g"], p["sg_ws"], p["sg_bias"],
              dec["dmat"], dec["qdec"], dec["kdec"], p["w_br_ret"], p["w_br_sg"], p["w_br_x"], p["w_o"]]
    return pl.pallas_call(
        _mixer_kernel,
        grid=(b, l // t),
        in_specs=[_smem_spec(), tok(D_MODEL), tab, tab, tab, tab, mem, mem]
        + [_const_spec(c.shape) for c in consts] + cspec,
        out_specs=[tok(D_MODEL), pl.BlockSpec((None, HEADS, DK, DV), lambda i, j: (i, 0, 0, 0))] + cspec,
        out_shape=[jax.ShapeDtypeStruct((b, l, D_MODEL), F32),
                   jax.ShapeDtypeStruct((b, HEADS, DK, DV), F32)] + cshape,
        scratch_shapes=[pltpu.VMEM((t, V_W), BF16), pltpu.VMEM((t, SG_W), BF16), pltpu.VMEM((t, X_W), BF16)],
        compiler_params=_params(("arbitrary", "arbitrary")),
        name="prompt_mixer",
    )(dec["gc"], x, *tabs, mk_b, mv_b, *consts, *cast_weights)


def _row_head(n):
    return (lax.broadcasted_iota(jnp.int32, (n, 1), 0) >> 2) & (HEADS - 1)


def _state_matmuls(gc_ref, qe_ref, kre_ref, kde_ref, ve_ref, xqe_ref, s_ref, mk_ref, dmat_ref, so_ref):
    rows_n = qe_ref.shape[0]
    per = HEADS * EXP
    batches = range(rows_n // per)
    rows = [slice(b * per, (b + 1) * per) for b in batches]
    q, ve, xq = qe_ref[...], ve_ref[...], xqe_ref[...]
    sc = _mm_nt(q, kre_ref[...]) * dmat_ref[...]
    kd = kde_ref[...]
    rowb = lax.broadcasted_iota(jnp.int32, (rows_n, 1), 0) >> 4
    cross = jnp.concatenate(
        [jnp.dot(q[rows[b]], s_ref[b].astype(BF16), preferred_element_type=F32) for b in batches], axis=0)
    upd = [_mm_tn(kd, jnp.where(rowb == b, ve, jnp.zeros_like(ve))) for b in batches]
    for b in batches:
        for hd in range(HEADS):
            hr = slice(hd * DK, (hd + 1) * DK)
            so_ref[b, hr, :] = gc_ref[hd] * s_ref[b, hr, :] + upd[b][hr]
    sx = jnp.concatenate([_mm_nt(xq[rows[b]], mk_ref[b]) for b in batches], axis=0) * np.float32(DK ** -0.5)
    return sc, cross, sx


def _state_outputs(sc, cross, sx, ve_ref, ge_ref, mv_ref, qdec_ref, gn_ref, oret_ref, ox_ref):
    rows_n = ve_ref.shape[0]
    per = HEADS * EXP
    batches = range(rows_n // per)
    rows = [slice(b * per, (b + 1) * per) for b in batches]
    inner = jnp.dot(sc.astype(BF16), ve_ref[...], preferred_element_type=F32)
    hh = _row_head(rows_n)
    own_col = (lax.broadcasted_iota(jnp.int32, (rows_n, MEM_LEN * HEADS), 1) & (HEADS - 1)) == hh
    p = _softmax(jnp.where(own_col, sx, np.float32(-1e30))).astype(BF16)
    oxs = jnp.concatenate([_mm(p[rows[b]], mv_ref[b]) for b in batches], axis=0)
    o = inner + cross * qdec_ref[...]
    og = ge_ref[...] * (_stdnorm(o) * gn_ref[...])
    for hd in range(HEADS):
        oret_ref[:, hd * DV:(hd + 1) * DV] = jnp.where(hh == hd, og, 0.0)
        ox_ref[:, hd * DK:(hd + 1) * DK] = jnp.where(hh == hd, oxs, 0.0)


def _ffn_kernel(gc_ref, x_ref, ng_ref, wup_ref, cw_ref, cb_ref, wdn_ref, nf_ref,
                qe_ref, kre_ref, kde_ref, ve_ref, ge_ref, xqe_ref, s_ref, mk_ref, mv_ref, dmat_ref, qdec_ref, gn_ref,
                y_ref, tail_ref, oret_ref, ox_ref, so_ref, zs_ref):
    tile = x_ref.shape[0]
    first = pl.program_id(1) == 0

    @pl.when(first)
    def _():
        zs_ref[0:8, :] = jnp.zeros((8, 2 * D_FF), F32)

    @pl.when(jnp.logical_not(first))
    def _():
        zs_ref[0:8, :] = zs_ref[tile:tile + 8, :]

    sc, cross, sx = _state_matmuls(gc_ref, qe_ref, kre_ref, kde_ref, ve_ref, xqe_ref, s_ref, mk_ref, dmat_ref,
                                   so_ref)
    x = x_ref[...]
    zs_ref[8:8 + tile, :] = _mm(_rms(x, ng_ref[...]), wup_ref[...])
    _state_outputs(sc, cross, sx, ve_ref, ge_ref, mv_ref, qdec_ref, gn_ref, oret_ref, ox_ref)
    ca, cb = slice(0, D_FF), slice(D_FF, 2 * D_FF)
    gate = _conv_gate(zs_ref, tile, ca, cb, cw_ref[:, ca], cb_ref[:, ca], cw_ref[:, cb], cb_ref[:, cb])
    y = x + _mm(gate, wdn_ref[...])
    y_ref[...] = _rms(y, nf_ref[...])
    tail_ref[...] = zs_ref[tile:tile + 8, :]


def _prompt_ffn(x, p, pre, state, mk, mv, dec):
    b, l, _ = x.shape
    t = PROMPT_TILE
    nt = l // t
    qe, kre, kde, ve, ge, xqe = pre
    nb = state.shape[0]
    bb = nb // (b * nt)
    rb = bb * HEADS * EXP
    ne = qe.shape[0]
    tok = pl.BlockSpec((None, t, D_MODEL), lambda i, j: (i, j, 0))
    rowblk = lambda w: pl.BlockSpec((rb, w), lambda i, j: (i * nt + j, 0))
    batblk = lambda a, c: pl.BlockSpec((bb, a, c), lambda i, j: (i * nt + j, 0, 0))
    consts = [p["norm_ffn_g"], p["w_up"], p["conv_w"], p["conv_b"], p["w_down"], p["norm_final_g"]]
    return pl.pallas_call(
        _ffn_kernel,
        grid=(b, nt),
        in_specs=[_smem_spec(), tok] + [_const_spec(c.shape) for c in consts]
        + [rowblk(QK_W), rowblk(QK_W), rowblk(QK_W), rowblk(DV), rowblk(DV), rowblk(DK),
           batblk(HEADS * DK, DV), batblk(MEM_LEN * HEADS, DK), batblk(MEM_LEN * HEADS, DK),
           _const_spec((rb, rb)), _const_spec((rb, DV)), _const_spec((rb, DV))],
        out_specs=[tok, pl.BlockSpec((None, 8, 2 * D_FF), lambda i, j: (i, 0, 0)),
                   rowblk(V_W), rowblk(X_W), batblk(HEADS * DK, DV)],
        out_shape=[jax.ShapeDtypeStruct((b, l, D_MODEL), F32), jax.ShapeDtypeStruct((b, 8, 2 * D_FF), F32),
                   jax.ShapeDtypeStruct((ne, V_W), F32), jax.ShapeDtypeStruct((ne, X_W), F32),
                   jax.ShapeDtypeStruct(state.shape, F32)],
        scratch_shapes=[pltpu.VMEM((t + 8, 2 * D_FF), F32)],
        compiler_params=_params(("arbitrary", "arbitrary")),
        name="prompt_ffn",
    )(dec["gc"], x, *consts, qe, kre, kde, ve, ge, xqe, state, mk, mv, dec["dmat"], dec["qdec"], dec["gn"])


def _sample_pre_kernel(ws_ref, bs_ref, x_ref, cq_ref, sq_ref, ck_ref, sk_ref, kdec_ref, ng_ref, win_ref, bg_ref,
                       lng_ref,
                       qe_ref, kre_ref, kde_ref, ve_ref, ge_ref, xqe_ref, osg_ref, vrows_ref, gates_ref,
                       h_ref, vn_ref):
    n = x_ref.shape[0]
    nb = n // EXP
    ne = qe_ref.shape[0]
    i = pl.program_id(0)

    @pl.when(i == 0)
    def _():
        h_ref[...] = _rms(x_ref[...], ng_ref[...]).astype(BF16)
        vn_ref[...] = jnp.zeros_like(vn_ref)

    r = lax.broadcasted_iota(jnp.int32, (ne, n), 0)
    c = lax.broadcasted_iota(jnp.int32, (ne, n), 1)
    rep = jnp.where(c == (r & (EXP - 1)) * nb + i * (ne // (HEADS * EXP)) + (r >> 4), 1.0, 0.0).astype(BF16)
    he = jnp.dot(rep, h_ref[...], preferred_element_type=F32).astype(BF16)
    hh = _row_head(ne)
    h = h_ref[pl.ds(pl.multiple_of(i * nb, nb), nb), :]

    def proj(hm, a, b):
        return jnp.dot(hm, win_ref[:, a:b], preferred_element_type=F32)

    q, k = proj(he, C_Q, C_K), proj(he, C_K, C_V)
    xq = proj(he, C_XQ, C_GT)
    cq, sq, ck, sk, kdec = cq_ref[...], sq_ref[...], ck_ref[...], sk_ref[...], kdec_ref[...]
    for hd in range(HEADS):
        cols = slice(hd * DK, (hd + 1) * DK)
        own = hh == hd
        kr = _rope(k[:, cols], ck, sk)
        qe_ref[:, cols] = jnp.where(own, _rope(q[:, cols], cq, sq), 0.0).astype(BF16)
        kre_ref[:, cols] = jnp.where(own, kr, 0.0).astype(BF16)
        kde_ref[:, cols] = jnp.where(own, kr * kdec, 0.0).astype(BF16)
    v = proj(he, C_V, C_G)
    g = proj(he, C_G, C_SU)
    ve = jnp.zeros((ne, DV), F32)
    ge = jnp.zeros((ne, DV), F32)
    xqc = jnp.zeros((ne, DK), F32)
    for hd in range(HEADS):
        cols = slice(hd * DV, (hd + 1) * DV)
        own = hh == hd
        ve = jnp.where(own, v[:, cols], ve)
        ge = jnp.where(own, g[:, cols], ge)
        xqc = jnp.where(own, xq[:, hd * DK:(hd + 1) * DK], xqc)
    xqe_ref[...] = xqc.astype(BF16)
    ve_ref[...] = ve.astype(BF16)
    ge_ref[...] = _silu(ge)

    u = _gelu(proj(h, C_SU, C_SV))
    vn = _stdnorm(_gelu(proj(h, C_SV, C_XQ))) * lng_ref[...]
    vrows_ref[...] = vn
    vn_ref[i] = vn
    for g in range(HEADS):
        cols = slice(g * DK, (g + 1) * DK)
        mixed = jnp.full((nb, DK), bs_ref[g * EXP + i], F32)
        for s in range(EXP):
            w = jnp.where(s <= i, ws_ref[(g * EXP + i) * EXP + s], 0.0)
            mixed = mixed + w * vn_ref[s, :, cols]
        osg_ref[:, cols] = (u[:, cols] * mixed).astype(BF16)
    gates_ref[...] = _sigmoid(proj(h, C_GT, C_END) + bg_ref[...])


def _sample_pre(x, tabs, kdec_rows, p, ws4, bs4):
    n = x.shape[0]
    t = n // EXP
    te = (t // EXP) * HEADS * EXP
    consts = [p["norm_mix_g"], p["w_in"], p["b_gate"], p["sg_ln_g"]]
    rowblk = lambda rows, w: pl.BlockSpec((rows, w), lambda i: (i, 0))
    outs = [(te, QK_W, BF16), (te, QK_W, BF16), (te, QK_W, BF16), (te, DV, BF16), (te, DV, F32),
            (te, DK, BF16), (t, SG_W, BF16), (t, SG_W, F32), (t, 3 * D_MODEL, F32)]
    return pl.pallas_call(
        _sample_pre_kernel,
        grid=(EXP,),
        in_specs=[_smem_spec(), _smem_spec(), _const_spec(x.shape)] + [_const_spec((te, DK))] * 5
        + [_const_spec(c.shape) for c in consts],
        out_specs=[rowblk(r, w) for r, w, _ in outs],
        out_shape=[jax.ShapeDtypeStruct((r * EXP, w), d) for r, w, d in outs],
        scratch_shapes=[pltpu.VMEM((n, D_MODEL), BF16), pltpu.VMEM((EXP, t, SG_W), F32)],
        compiler_params=_params(("arbitrary",)),
        name="sample_pre",
    )(ws4, bs4, x, *tabs, kdec_rows, *consts)


def _sample_merge_kernel(x_ref, orete_ref, oxe_ref, osg_ref, gates_ref, wr_ref, wsg_ref, wx_ref, wo_ref, xo_ref):
    n = x_ref.shape[0]
    ne = n * EXP
    r = lax.broadcasted_iota(jnp.int32, (n, ne), 0)
    c = lax.broadcasted_iota(jnp.int32, (n, ne), 1)
    col = jnp.where(r == (c & (EXP - 1)) * (n // EXP) + (c >> 4), 1.0, 0.0).astype(BF16)
    oret = jnp.dot(col, orete_ref[...].astype(BF16), preferred_element_type=F32)
    ox = jnp.dot(col, oxe_ref[...].astype(BF16), preferred_element_type=F32)
    gates = gates_ref[...]
    merged = (gates[:, :D_MODEL] * _mm(oret, wr_ref[...])
              + gates[:, D_MODEL:2 * D_MODEL] * jnp.dot(osg_ref[...], wsg_ref[...], preferred_element_type=F32)
              + gates[:, 2 * D_MODEL:] * _mm(ox, wx_ref[...]))
    xo_ref[...] = x_ref[...] + _mm(merged, wo_ref[...])


def _sample_merge(x, orete, oxe, osg, gates, p):
    ins = [x, orete, oxe, osg, gates, p["w_br_ret"], p["w_br_sg"], p["w_br_x"], p["w_o"]]
    return pl.pallas_call(
        _sample_merge_kernel,
        grid=(1,),
        in_specs=[_const_spec(a.shape) for a in ins],
        out_specs=pl.BlockSpec(x.shape, lambda i: (0, 0)),
        out_shape=jax.ShapeDtypeStruct(x.shape, F32),
        compiler_params=_params(("arbitrary",)),
        name="sample_merge",
    )(*ins)


def _sample_ffn_kernel(x_ref, ng_ref, wa_ref, wb_ref, cwa_ref, cwb_ref, cba_ref, cbb_ref, s0a_ref, s0b_ref,
                       s1a_ref, s1b_ref, wdn_ref, nf_ref,
                       y_ref, c2a_ref, c2b_ref, c3a_ref, c3b_ref, h_ref, acc_ref):
    nb = x_ref.shape[0] // EXP
    j = pl.program_id(0)

    @pl.when(j == 0)
    def _():
        h_ref[...] = _rms(x_ref[...], ng_ref[...]).astype(BF16)
        acc_ref[...] = jnp.zeros_like(acc_ref)

    h = h_ref[...]

    def conv(w_ref, cw_ref, cb_ref, s0_ref, s1_ref, c2_ref, c3_ref):
        z = jnp.dot(h, w_ref[...], preferred_element_type=F32)
        zp = [s0_ref[...], s1_ref[...]] + [z[l * nb:(l + 1) * nb] for l in range(EXP)]
        c2_ref[...] = zp[EXP]
        c3_ref[...] = zp[EXP + 1]
        cw, cb = cw_ref[...], cb_ref[...]
        return jnp.concatenate([cb + cw[0:1] * zp[l] + cw[1:2] * zp[l + 1] + cw[2:3] * zp[l + 2]
                                for l in range(EXP)], axis=0)

    a = conv(wa_ref, cwa_ref, cba_ref, s0a_ref, s1a_ref, c2a_ref, c3a_ref)
    b = conv(wb_ref, cwb_ref, cbb_ref, s0b_ref, s1b_ref, c2b_ref, c3b_ref)
    acc_ref[...] += _mm(_gelu(a) * b, wdn_ref[...])

    @pl.when(j == pl.num_programs(0) - 1)
    def _():
        y_ref[...] = _rms(x_ref[...] + acc_ref[...], nf_ref[...])


def _sample_ffn(x, sc, p):
    n = x.shape[0]
    nb = n // EXP
    cw = FFN_CW
    nch = D_FF // cw
    ca = lambda rows: pl.BlockSpec((rows, cw), lambda j: (0, j))
    cb = lambda rows: pl.BlockSpec((rows, cw), lambda j: (0, nch + j))
    sa = lambda k: pl.BlockSpec((None, nb, cw), lambda j: (k, 0, j))
    sb = lambda k: pl.BlockSpec((None, nb, cw), lambda j: (k, 0, nch + j))
    full = pl.BlockSpec((n, D_MODEL), lambda j: (0, 0))
    vec = pl.BlockSpec((1, D_MODEL), lambda j: (0, 0))
    return pl.pallas_call(
        _sample_ffn_kernel,
        grid=(nch,),
        in_specs=[full, vec, ca(D_MODEL), cb(D_MODEL), ca(3), cb(3), ca(1), cb(1), sa(0), sb(0), sa(1), sb(1),
                  pl.BlockSpec((cw, D_MODEL), lambda j: (j, 0)), vec],
        out_specs=[full] + [ca(nb)] * 4,
        out_shape=[jax.ShapeDtypeStruct((n, D_MODEL), F32)] + [jax.ShapeDtypeStruct((nb, D_FF), F32)] * 4,
        scratch_shapes=[pltpu.VMEM((n, D_MODEL), BF16), pltpu.VMEM((n, D_MODEL), F32)],
        compiler_params=_params(("arbitrary",)),
        name="sample_ffn",
    )(x, p["norm_ffn_g"], p["w_up"], p["w_up"], p["conv_w"], p["conv_w"], p["conv_b"], p["conv_b"],
      sc, sc, sc, sc, p["w_down"], p["norm_final_g"])


def _rope_tables(pos, scale):
    inv = ROPE_BASE ** (-np.arange(0, DK, 2, dtype=np.float64) / DK)
    ang = np.asarray(pos, np.float64)[:, None] * inv[None, :]
    cos, sin = np.cos(ang), np.sin(ang)
    return (np.concatenate([cos, cos], -1) * scale).astype(np.float32), \
        (np.concatenate([-sin, sin], -1) * scale).astype(np.float32)


def _decay(chunk):
    lg = np.log1p(-np.exp2(-5.0 - np.arange(HEADS, dtype=np.float64)))
    n = np.arange(chunk, dtype=np.float64)
    diff = n[:, None] - n[None, :]
    dmat = np.where(diff >= 0, np.exp(np.maximum(diff, 0.0)[None] * lg[:, None, None]), 0.0)
    qdec = np.exp((n + 1.0)[None, :] * lg[:, None])
    kdec = np.exp((chunk - 1.0 - n)[None, :] * lg[:, None])
    f32 = lambda a: a.astype(np.float32)
    return f32(dmat), f32(qdec), f32(kdec), f32(np.exp(chunk * lg))


def kernel(x_prompt, x_sample, mem_prompt, state_ret, state_conv, cache_mem_k, cache_mem_v, norm_mix_g, w_in,
           b_gate, ret_gn_g, sg_ln_g, sg_ws, sg_bs, mem_norm_g, w_mem_kv, w_br_ret, w_br_sg, w_br_x, w_o,
           norm_ffn_g, w_up, conv_w, conv_b, w_down, norm_final_g):
    bp, lp, _ = x_prompt.shape
    bs, ls, _ = x_sample.shape
    assert state_ret.shape[0] == 1 and ls == EXP and lp % PROMPT_TILE == 0
    assert bs % (bp * (lp // PROMPT_TILE)) == 0
    assert bs == CHUNK
    row = lambda a: a.reshape(1, -1)
    scale = DK ** -0.5

    mk, mv, mk_b, mv_b, w_in_b, w_br_ret_b, w_br_sg_b, w_br_x_b, w_o_b = _memkv(
        mem_prompt, row(mem_norm_g[0]), w_mem_kv[0], [w_in[0], w_br_ret[0], w_br_sg[0], w_br_x[0], w_o[0]])
    p = dict(norm_mix_g=row(norm_mix_g[0]), w_in=w_in_b, b_gate=row(b_gate[0]),
             gn_g=row(ret_gn_g[0]), sg_ln_g=row(sg_ln_g[0]),
             w_br_ret=w_br_ret_b, w_br_sg=w_br_sg_b, w_br_x=w_br_x_b, w_o=w_o_b,
             norm_ffn_g=row(norm_ffn_g[0]), conv_w=conv_w[0], conv_b=row(conv_b[0]),
             norm_final_g=row(norm_final_g))
    pos_p = np.arange(lp)
    tabs_p = (*_rope_tables(pos_p, 1.0), *_rope_tables(pos_p, scale))
    dmat, qdec, kdec, gc = _decay(CHUNK)
    bcast = lambda a: np.ascontiguousarray(np.broadcast_to(a[:, :, None], (HEADS, CHUNK, DK)))
    dec_p = dict(dmat=dmat, qdec=bcast(qdec), kdec=bcast(kdec), gc=gc)
    pp = dict(p, sg_ws=sg_ws[0], sg_bias=jnp.repeat(sg_bs[0].T, DK, axis=1))
    x_mid, s_prompt, w_up_b, w_down_b = _prompt_mixer(x_prompt, tabs_p, mk_b, mv_b, dec_p, pp,
                                                      [w_up[0], w_down[0]])
    p = dict(p, w_up=w_up_b, w_down=w_down_b)

    n = bs * ls
    per = HEADS * EXP
    te = (bs // EXP) * per
    pos_s = PAST_LEN + (np.arange(te) & (ls - 1))
    tabs_s = (*_rope_tables(pos_s, 1.0), *_rope_tables(pos_s, scale))
    dmat4, qdec4, kdec4, gc4 = _decay(ls)
    kdec_rows = np.ascontiguousarray(np.broadcast_to(np.tile(kdec4.reshape(per), te // per)[:, None], (te, DK)))
    xs = jnp.swapaxes(x_sample, 0, 1).reshape(n, D_MODEL)
    pre = _sample_pre(xs, tabs_s, kdec_rows, p, sg_ws[0][:, :ls, :ls].reshape(-1), sg_bs[0][:, :ls].reshape(-1))
    qe, kre, kde, ve, ge, xqe, osg, vrows, gates = pre

    sbb = bs // (bp * (lp // PROMPT_TILE))
    rb = sbb * per
    blk16 = np.einsum("hk,hls->hlks", np.eye(HEADS, dtype=np.float32), dmat4).reshape(per, per)
    dec_s = dict(gc=gc4,
                 dmat=np.kron(np.eye(sbb, dtype=np.float32), blk16),
                 qdec=np.ascontiguousarray(np.broadcast_to(np.tile(qdec4.reshape(per), sbb)[:, None], (rb, DV))),
                 gn=jnp.tile(jnp.repeat(ret_gn_g[0], ls, axis=0), (sbb, 1)))
    y_prompt, tail, orete, oxe, s_sample = _prompt_ffn(
        x_mid, p, (qe, kre, kde, ve, ge, xqe), state_ret[0].reshape(bs, HEADS * DK, DV),
        cache_mem_k[0].reshape(bs, MEM_LEN * HEADS, DK), cache_mem_v[0].reshape(bs, MEM_LEN * HEADS, DK), dec_s)
    xs_mid = _sample_merge(xs, orete, oxe, osg, gates, p)
    y_sample, c2a, c2b, c3a, c3b = _sample_ffn(xs_mid, jnp.swapaxes(state_conv[0], 0, 1), p)
    conv_s = jnp.stack([jnp.concatenate([c2a, c2b], -1), jnp.concatenate([c3a, c3b], -1)], axis=1)
    unpos = lambda a: jnp.swapaxes(a.reshape(ls, bs, a.shape[-1]), 0, 1)

    return (y_prompt, unpos(y_sample),
            s_prompt[None], tail[None, :, 6:8],
            mk.reshape(1, bp, MEM_LEN, HEADS, DK), mv.reshape(1, bp, MEM_LEN, HEADS, DK),
            s_sample.reshape(1, bs, HEADS, DK, DV), conv_s[None],
            unpos(vrows)[None])
```

```python
import functools

import numpy as np
import jax
import jax.numpy as jnp
from jax import lax
from jax.experimental import pallas as pl
from jax.experimental.pallas import tpu as pltpu

F32 = jnp.float32
BF16 = jnp.bfloat16

D_MODEL = 1024
HEADS = 4
DK = 128
DV = 256
QK_W = HEADS * DK
V_W = HEADS * DV
SG_W = 512
X_W = 512
MEM_LEN = 256
D_FF = 2816
CHUNK = 128
ROPE_BASE = 10000.0
EPS = 1e-6
PAST_LEN = 16384

C_Q, C_K, C_V, C_G, C_SU, C_SV, C_XQ, C_GT, C_END = 0, 512, 1024, 2048, 3072, 3584, 4096, 4608, 7680

PROMPT_TILE = 512
EXP = HEADS
FFN_CW = 1408
VMEM_BYTES_V7X = 64 * 1024 * 1024
VMEM_LIMIT = VMEM_BYTES_V7X - 3 * 1024 * 1024


def _rms(x, g):
    return x * lax.rsqrt(jnp.mean(x * x, axis=-1, keepdims=True) + EPS) * g


def _stdnorm(x):
    mu = jnp.mean(x, axis=-1, keepdims=True)
    xc = x - mu
    var = jnp.mean(xc * xc, axis=-1, keepdims=True)
    return xc * lax.rsqrt(var + EPS)


_GELU_C0 = np.float32(np.sqrt(2.0 / np.pi))
_GELU_C1 = np.float32(np.sqrt(2.0 / np.pi) * 0.044715)


def _gelu_tanh(x):
    return jnp.tanh(x * (_GELU_C0 + _GELU_C1 * (x * x)))


def _gelu(x):
    hx = 0.5 * x
    return hx + hx * _gelu_tanh(x)


def _sigmoid(x):
    return 0.5 + 0.5 * jnp.tanh(0.5 * x)


def _silu(x):
    hx = 0.5 * x
    return hx + hx * jnp.tanh(hx)


def _softmax(s):
    e = jnp.exp(s - jnp.max(s, axis=-1, keepdims=True))
    return e * (1.0 / jnp.sum(e, axis=-1, keepdims=True))


def _mm(a, b):
    return jnp.dot(a.astype(BF16), b.astype(BF16), preferred_element_type=F32)


def _mm_nt(a, b):
    return lax.dot_general(a.astype(BF16), b.astype(BF16), (((1,), (1,)), ((), ())),
                           preferred_element_type=F32)


def _mm_tn(a, b):
    return lax.dot_general(a.astype(BF16), b.astype(BF16), (((0,), (0,)), ((), ())),
                           preferred_element_type=F32)


def _rope(x, cos, sin):
    return x * cos + pltpu.roll(x, DK // 2, 1) * sin


def _tril(w):
    r = lax.broadcasted_iota(jnp.int32, w.shape, 0)
    c = lax.broadcasted_iota(jnp.int32, w.shape, 1)
    return jnp.where(r >= c, w, 0.0)


def _conv_gate(zs_ref, n, cols_a, cols_b, cw_a, cb_a, cw_b, cb_b):
    def conv(cols, cw, cb):
        zz = zs_ref[0:8 + n, cols]
        z1 = pltpu.roll(zz, 1, 0)[8:]
        z2 = pltpu.roll(zz, 2, 0)[8:]
        return cb + cw[0:1] * z2 + cw[1:2] * z1 + cw[2:3] * zz[8:]
    a = conv(cols_a, cw_a, cb_a)
    return (a + a * _gelu_tanh(a)) * conv(cols_b, 0.5 * cw_b, 0.5 * cb_b)


def _const_spec(shape):
    nd = len(shape)
    return pl.BlockSpec(shape, lambda *_: (0,) * nd, pipeline_mode=pl.Buffered(1))


def _smem_spec():
    return pl.BlockSpec(memory_space=pltpu.SMEM)


def _params(sem):
    return pltpu.CompilerParams(dimension_semantics=sem, vmem_limit_bytes=VMEM_LIMIT)


def _cast_blocks(refs):
    for src, dst in refs:
        dst[...] = src[...].astype(BF16)


def _cast_specs(weights, steps, flat_step):
    specs, shapes = [], []
    for w in weights:
        rows, cols = w.shape
        nblk = max(n for n in range(1, steps + 1) if steps % n == 0 and rows % (16 * n) == 0)
        specs.append(pl.BlockSpec((rows // nblk, cols), lambda *g, r=steps // nblk: (flat_step(*g) // r, 0)))
        shapes.append(jax.ShapeDtypeStruct(w.shape, BF16))
    return specs, shapes


def _memkv_kernel(mem_ref, g_ref, w_ref, *refs):
    ncast = (len(refs) - 4) // 2
    casts, (k_ref, v_ref, kb_ref, vb_ref) = refs[:ncast], refs[ncast:ncast + 4]
    kv = _mm(_rms(mem_ref[...], g_ref[...]), w_ref[...])
    k, v = kv[:, :X_W], kv[:, X_W:]
    for hd in range(HEADS):
        cols = slice(hd * DK, (hd + 1) * DK)
        k_ref[pl.ds(hd, MEM_LEN, stride=HEADS), :] = k[:, cols]
        v_ref[pl.ds(hd, MEM_LEN, stride=HEADS), :] = v[:, cols]
    kb_ref[...] = k.astype(BF16)
    vb_ref[...] = v.astype(BF16)
    _cast_blocks(zip(casts, refs[ncast + 4:]))


def _memkv(mem, g, w, cast_weights):
    b = mem.shape[0]
    blk = pl.BlockSpec((None, MEM_LEN, X_W), lambda i: (i, 0, 0))
    flat = pl.BlockSpec((None, MEM_LEN * HEADS, DK), lambda i: (i, 0, 0))
    cspec, cshape = _cast_specs(cast_weights, b, lambda i: i)
    return pl.pallas_call(
        _memkv_kernel,
        grid=(b,),
        in_specs=[pl.BlockSpec((None, MEM_LEN, D_MODEL), lambda i: (i, 0, 0)),
                  _const_spec((1, D_MODEL)), _const_spec((D_MODEL, 2 * X_W))] + cspec,
        out_specs=[flat, flat, blk, blk] + cspec,
        out_shape=[jax.ShapeDtypeStruct((b, MEM_LEN * HEADS, DK), F32)] * 2
        + [jax.ShapeDtypeStruct((b, MEM_LEN, X_W), BF16)] * 2 + cshape,
        compiler_params=_params(("arbitrary",)),
        name="mem_kv",
    )(mem, g, w, *cast_weights)


def _mixer_kernel(gc_ref, x_ref, cq_ref, sq_ref, ck_ref, sk_ref, mk_ref, mv_ref, ng_ref, win_ref, bg_ref,
                  gn_ref, lng_ref, ws_ref, sgb_ref, dmat_ref, qdec_ref, kdec_ref,
                  wr_ref, wsg_ref, wx_ref, wo_ref, wup_ref, wdn_ref,
                  xo_ref, s_ref, wup_o_ref, wdn_o_ref, oret_ref, osg_ref, ox_ref):
    tile = x_ref.shape[0]

    @pl.when(pl.program_id(1) == 0)
    def _():
        s_ref[...] = jnp.zeros_like(s_ref)

    x = x_ref[...]
    h = _rms(x, ng_ref[...]).astype(BF16)

    def proj(a, b):
        return jnp.dot(h, win_ref[:, a:b], preferred_element_type=F32)

    heads, chunks = range(HEADS), range(tile // CHUNK)
    rows = [slice(c * CHUNK, (c + 1) * CHUNK) for c in chunks]
    kcols = [slice(hd * DK, (hd + 1) * DK) for hd in heads]
    vcols = [slice(hd * DV, (hd + 1) * DV) for hd in heads]

    suv = proj(C_SU, C_XQ)
    su, sv = suv[:, :SG_W], suv[:, SG_W:]
    qk = proj(C_Q, C_V)
    q, k = qk[:, :QK_W], qk[:, QK_W:]
    u = _gelu(su)
    vn = (_stdnorm(_gelu(sv)) * lng_ref[...]).astype(BF16)
    wsg = [_tril(ws_ref[g]).astype(BF16) for g in heads]
    vb = proj(C_V, C_G).astype(BF16)
    mixed = [[jnp.dot(wsg[g], vn[rows[c], kcols[g]], preferred_element_type=F32) + sgb_ref[:, kcols[g]]
              for c in chunks] for g in heads]
    gsil = _silu(proj(C_G, C_SU))
    for g in heads:
        osg_ref[:, kcols[g]] = (u[:, kcols[g]] * jnp.concatenate(mixed[g], axis=0)).astype(BF16)

    cq, sq, ck, sk = cq_ref[...], sq_ref[...], ck_ref[...], sk_ref[...]
    qr = [_rope(q[:, kcols[hd]], cq, sq) for hd in heads]
    kr = [_rope(k[:, kcols[hd]], ck, sk) for hd in heads]
    sc = [[_mm_nt(qr[hd][rows[c]], kr[hd][rows[c]]) for c in chunks] for hd in heads]
    upd = [[_mm_tn(kr[hd][rows[c]] * kdec_ref[hd], vb[rows[c], vcols[hd]]) for c in chunks] for hd in heads]
    xq = proj(C_XQ, C_GT)
    gmid = C_GT + (C_END - C_GT) // 2
    gt0 = proj(C_GT, gmid)
    states = []
    for hd in heads:
        st, before = s_ref[hd], []
        for c in chunks:
            before.append(st.astype(BF16))
            st = gc_ref[hd] * st + upd[hd][c]
        s_ref[hd] = st
        states.append(before)
    o = [[jnp.dot(jnp.concatenate([(sc[hd][c] * dmat_ref[hd]).astype(BF16),
                                   (qr[hd][rows[c]] * qdec_ref[hd]).astype(BF16)], axis=1),
                  jnp.concatenate([vb[rows[c], vcols[hd]], states[hd][c]], axis=0),
                  preferred_element_type=F32)
          for c in chunks] for hd in heads]
    sx = [_mm_nt(xq[:, kcols[hd]], mk_ref[:, kcols[hd]]) * np.float32(DK ** -0.5) for hd in heads]
    for hd in heads:
        on = _stdnorm(jnp.concatenate(o[hd], axis=0)) * gn_ref[:, vcols[hd]]
        oret_ref[:, vcols[hd]] = (gsil[:, vcols[hd]] * on).astype(BF16)
    gt1 = proj(gmid, C_END)
    px = [_softmax(sx[hd]) for hd in heads]
    for hd in heads:
        ox_ref[:, kcols[hd]] = _mm(px[hd], mv_ref[:, kcols[hd]]).astype(BF16)

    gates = _sigmoid(jnp.concatenate([gt0, gt1], axis=1) + bg_ref[...])
    merged = (gates[:, :D_MODEL] * jnp.dot(oret_ref[...], wr_ref[...], preferred_element_type=F32)
              + gates[:, D_MODEL:2 * D_MODEL] * jnp.dot(osg_ref[...], wsg_ref[...], preferred_element_type=F32)
              + gates[:, 2 * D_MODEL:] * jnp.dot(ox_ref[...], wx_ref[...], preferred_element_type=F32))
    xo_ref[...] = x + _mm(merged, wo_ref[...])
    _cast_blocks(((wup_ref, wup_o_ref), (wdn_ref, wdn_o_ref)))


def _prompt_mixer(x, tabs, mk_b, mv_b, dec, p, cast_weights):
    b, l, _ = x.shape
    t = PROMPT_TILE
    nt = l // t
    cspec, cshape = _cast_specs(cast_weights, b * nt, lambda i, j: i * nt + j)
    tok = lambda w: pl.BlockSpec((None, t, w), lambda i, j: (i, j, 0))
    tab = pl.BlockSpec((t, DK), lambda i, j: (j, 0))
    mem = pl.BlockSpec((None, MEM_LEN, X_W), lambda i, j: (i, 0, 0))
    consts = [p["norm_mix_g"], p["w_in"], p["b_gate"], p["gn_g"], p["sg_ln_g"], p["sg_ws"], p["sg_bias"],
              dec["dmat"], dec["qdec"], dec["kdec"], p["w_br_ret"], p["w_br_sg"], p["w_br_x"], p["w_o"]]
    return pl.pallas_call(
        _mixer_kernel,
        grid=(b, l // t),
        in_specs=[_smem_spec(), tok(D_MODEL), tab, tab, tab, tab, mem, mem]
        + [_const_spec(c.shape) for c in consts] + cspec,
        out_specs=[tok(D_MODEL), pl.BlockSpec((None, HEADS, DK, DV), lambda i, j: (i, 0, 0, 0))] + cspec,
        out_shape=[jax.ShapeDtypeStruct((b, l, D_MODEL), F32),
                   jax.ShapeDtypeStruct((b, HEADS, DK, DV), F32)] + cshape,
        scratch_shapes=[pltpu.VMEM((t, V_W), BF16), pltpu.VMEM((t, SG_W), BF16), pltpu.VMEM((t, X_W), BF16)],
        compiler_params=_params(("arbitrary", "arbitrary")),
        name="prompt_mixer",
    )(dec["gc"], x, *tabs, mk_b, mv_b, *consts, *cast_weights)


def _row_head(n):
    return (lax.broadcasted_iota(jnp.int32, (n, 1), 0) >> 2) & (HEADS - 1)


def _state_matmuls(gc_ref, qe_ref, kre_ref, kde_ref, ve_ref, xqe_ref, s_ref, mk_ref, dmat_ref, so_ref):
    rows_n = qe_ref.shape[0]
    per = HEADS * EXP
    batches = range(rows_n // per)
    rows = [slice(b * per, (b + 1) * per) for b in batches]
    q, ve, xq = qe_ref[...], ve_ref[...], xqe_ref[...]
    sc = _mm_nt(q, kre_ref[...]) * dmat_ref[...]
    kd = kde_ref[...]
    rowb = lax.broadcasted_iota(jnp.int32, (rows_n, 1), 0) >> 4
    cross = jnp.concatenate(
        [jnp.dot(q[rows[b]], s_ref[b].astype(BF16), preferred_element_type=F32) for b in batches], axis=0)
    upd = [_mm_tn(kd, jnp.where(rowb == b, ve, jnp.zeros_like(ve))) for b in batches]
    for b in batches:
        for hd in range(HEADS):
            hr = slice(hd * DK, (hd + 1) * DK)
            so_ref[b, hr, :] = gc_ref[hd] * s_ref[b, hr, :] + upd[b][hr]
    sx = jnp.concatenate([_mm_nt(xq[rows[b]], mk_ref[b]) for b in batches], axis=0) * np.float32(DK ** -0.5)
    return sc, cross, sx


def _state_outputs(sc, cross, sx, ve_ref, ge_ref, mv_ref, qdec_ref, gn_ref, oret_ref, ox_ref):
    rows_n = ve_ref.shape[0]
    per = HEADS * EXP
    batches = range(rows_n // per)
    rows = [slice(b * per, (b + 1) * per) for b in batches]
    inner = jnp.dot(sc.astype(BF16), ve_ref[...], preferred_element_type=F32)
    hh = _row_head(rows_n)
    own_col = (lax.broadcasted_iota(jnp.int32, (rows_n, MEM_LEN * HEADS), 1) & (HEADS - 1)) == hh
    p = _softmax(jnp.where(own_col, sx, np.float32(-1e30))).astype(BF16)
    oxs = jnp.concatenate([_mm(p[rows[b]], mv_ref[b]) for b in batches], axis=0)
    o = inner + cross * qdec_ref[...]
    og = ge_ref[...] * (_stdnorm(o) * gn_ref[...])
    for hd in range(HEADS):
        oret_ref[:, hd * DV:(hd + 1) * DV] = jnp.where(hh == hd, og, 0.0)
        ox_ref[:, hd * DK:(hd + 1) * DK] = jnp.where(hh == hd, oxs, 0.0)


def _ffn_kernel(gc_ref, x_ref, ng_ref, wup_ref, cw_ref, cb_ref, wdn_ref, nf_ref,
                qe_ref, kre_ref, kde_ref, ve_ref, ge_ref, xqe_ref, s_ref, mk_ref, mv_ref, dmat_ref, qdec_ref, gn_ref,
                y_ref, tail_ref, oret_ref, ox_ref, so_ref, zs_ref):
    tile = x_ref.shape[0]
    first = pl.program_id(1) == 0

    @pl.when(first)
    def _():
        zs_ref[0:8, :] = jnp.zeros((8, 2 * D_FF), F32)

    @pl.when(jnp.logical_not(first))
    def _():
        zs_ref[0:8, :] = zs_ref[tile:tile + 8, :]

    sc, cross, sx = _state_matmuls(gc_ref, qe_ref, kre_ref, kde_ref, ve_ref, xqe_ref, s_ref, mk_ref, dmat_ref,
                                   so_ref)
    x = x_ref[...]
    zs_ref[8:8 + tile, :] = _mm(_rms(x, ng_ref[...]), wup_ref[...])
    _state_outputs(sc, cross, sx, ve_ref, ge_ref, mv_ref, qdec_ref, gn_ref, oret_ref, ox_ref)
    ca, cb = slice(0, D_FF), slice(D_FF, 2 * D_FF)
    gate = _conv_gate(zs_ref, tile, ca, cb, cw_ref[:, ca], cb_ref[:, ca], cw_ref[:, cb], cb_ref[:, cb])
    y = x + _mm(gate, wdn_ref[...])
    y_ref[...] = _rms(y, nf_ref[...])
    tail_ref[...] = zs_ref[tile + 6:tile + 8, :]


def _prompt_ffn(x, p, pre, state, mk, mv, dec):
    b, l, _ = x.shape
    t = PROMPT_TILE
    nt = l // t
    qe, kre, kde, ve, ge, xqe = pre
    nb = state.shape[0]
    bb = nb // (b * nt)
    rb = bb * HEADS * EXP
    ne = qe.shape[0]
    tok = pl.BlockSpec((None, t, D_MODEL), lambda i, j: (i, j, 0))
    rowblk = lambda w: pl.BlockSpec((rb, w), lambda i, j: (i * nt + j, 0))
    batblk = lambda a, c: pl.BlockSpec((bb, a, c), lambda i, j: (i * nt + j, 0, 0))
    consts = [p["norm_ffn_g"], p["w_up"], p["conv_w"], p["conv_b"], p["w_down"], p["norm_final_g"]]
    return pl.pallas_call(
        _ffn_kernel,
        grid=(b, nt),
        in_specs=[_smem_spec(), tok] + [_const_spec(c.shape) for c in consts]
        + [rowblk(QK_W), rowblk(QK_W), rowblk(QK_W), rowblk(DV), rowblk(DV), rowblk(DK),
           batblk(HEADS * DK, DV), batblk(MEM_LEN * HEADS, DK), batblk(MEM_LEN * HEADS, DK),
           _const_spec((rb, rb)), _const_spec((rb, DV)), _const_spec((rb, DV))],
        out_specs=[tok, pl.BlockSpec((None, 2, 2 * D_FF), lambda i, j: (i, 0, 0)),
                   rowblk(V_W), rowblk(X_W), batblk(HEADS * DK, DV)],
        out_shape=[jax.ShapeDtypeStruct((b, l, D_MODEL), F32), jax.ShapeDtypeStruct((b, 2, 2 * D_FF), F32),
                   jax.ShapeDtypeStruct((ne, V_W), F32), jax.ShapeDtypeStruct((ne, X_W), F32),
                   jax.ShapeDtypeStruct(state.shape, F32)],
        scratch_shapes=[pltpu.VMEM((t + 8, 2 * D_FF), F32)],
        compiler_params=_params(("arbitrary", "arbitrary")),
        name="prompt_ffn",
    )(dec["gc"], x, *consts, qe, kre, kde, ve, ge, xqe, state, mk, mv, dec["dmat"], dec["qdec"], dec["gn"])


def _sample_pre_kernel(ws_ref, bs_ref, x_ref, cq_ref, sq_ref, ck_ref, sk_ref, kdec_ref, ng_ref, win_ref, bg_ref,
                       lng_ref, wr_ref, wsg_ref, wx_ref, wo_ref,
                       qe_ref, kre_ref, kde_ref, ve_ref, ge_ref, xqe_ref, osg_ref, vrows_ref, gates_ref,
                       wr_o_ref, wsg_o_ref, wx_o_ref, wo_o_ref, h_ref, vn_ref):
    n = x_ref.shape[0]
    nb = n // EXP
    ne = qe_ref.shape[0]
    i = pl.program_id(0)

    @pl.when(i == 0)
    def _():
        h_ref[...] = _rms(x_ref[...], ng_ref[...]).astype(BF16)
        vn_ref[...] = jnp.zeros_like(vn_ref)

    r = lax.broadcasted_iota(jnp.int32, (ne, n), 0)
    c = lax.broadcasted_iota(jnp.int32, (ne, n), 1)
    rep = jnp.where(c == (r & (EXP - 1)) * nb + i * (ne // (HEADS * EXP)) + (r >> 4), 1.0, 0.0).astype(BF16)
    he = jnp.dot(rep, h_ref[...], preferred_element_type=F32).astype(BF16)
    hh = _row_head(ne)
    h = h_ref[pl.ds(pl.multiple_of(i * nb, nb), nb), :]

    def proj(hm, a, b):
        return jnp.dot(hm, win_ref[:, a:b], preferred_element_type=F32)

    q, k = proj(he, C_Q, C_K), proj(he, C_K, C_V)
    xq = proj(he, C_XQ, C_GT)
    cq, sq, ck, sk, kdec = cq_ref[...], sq_ref[...], ck_ref[...], sk_ref[...], kdec_ref[...]
    for hd in range(HEADS):
        cols = slice(hd * DK, (hd + 1) * DK)
        own = hh == hd
        kr = _rope(k[:, cols], ck, sk)
        qe_ref[:, cols] = jnp.where(own, _rope(q[:, cols], cq, sq), 0.0).astype(BF16)
        kre_ref[:, cols] = jnp.where(own, kr, 0.0).astype(BF16)
        kde_ref[:, cols] = jnp.where(own, kr * kdec, 0.0).astype(BF16)
    v = proj(he, C_V, C_G)
    g = proj(he, C_G, C_SU)
    ve = jnp.zeros((ne, DV), F32)
    ge = jnp.zeros((ne, DV), F32)
    xqc = jnp.zeros((ne, DK), F32)
    for hd in range(HEADS):
        cols = slice(hd * DV, (hd + 1) * DV)
        own = hh == hd
        ve = jnp.where(own, v[:, cols], ve)
        ge = jnp.where(own, g[:, cols], ge)
        xqc = jnp.where(own, xq[:, hd * DK:(hd + 1) * DK], xqc)
    xqe_ref[...] = xqc.astype(BF16)
    ve_ref[...] = ve.astype(BF16)
    ge_ref[...] = _silu(ge)

    u = _gelu(proj(h, C_SU, C_SV))
    vn = _stdnorm(_gelu(proj(h, C_SV, C_XQ))) * lng_ref[...]
    vrows_ref[...] = vn
    vn_ref[i] = vn
    for g in range(HEADS):
        cols = slice(g * DK, (g + 1) * DK)
        mixed = jnp.full((nb, DK), bs_ref[g * EXP + i], F32)
        for s in range(EXP):
            w = jnp.where(s <= i, ws_ref[(g * EXP + i) * EXP + s], 0.0)
            mixed = mixed + w * vn_ref[s, :, cols]
        osg_ref[:, cols] = (u[:, cols] * mixed).astype(BF16)
    gates_ref[...] = _sigmoid(proj(h, C_GT, C_END) + bg_ref[...])
    _cast_blocks(((wr_ref, wr_o_ref), (wsg_ref, wsg_o_ref), (wx_ref, wx_o_ref), (wo_ref, wo_o_ref)))


def _sample_pre(x, tabs, kdec_rows, p, ws4, bs4, cast_weights):
    n = x.shape[0]
    cspec, cshape = _cast_specs(cast_weights, EXP, lambda i: i)
    t = n // EXP
    te = (t // EXP) * HEADS * EXP
    consts = [p["norm_mix_g"], p["w_in"], p["b_gate"], p["sg_ln_g"]]
    rowblk = lambda rows, w: pl.BlockSpec((rows, w), lambda i: (i, 0))
    outs = [(te, QK_W, BF16), (te, QK_W, BF16), (te, QK_W, BF16), (te, DV, BF16), (te, DV, F32),
            (te, DK, BF16), (t, SG_W, BF16), (t, SG_W, F32), (t, 3 * D_MODEL, F32)]
    return pl.pallas_call(
        _sample_pre_kernel,
        grid=(EXP,),
        in_specs=[_smem_spec(), _smem_spec(), _const_spec(x.shape)] + [_const_spec((te, DK))] * 5
        + [_const_spec(c.shape) for c in consts] + cspec,
        out_specs=[rowblk(r, w) for r, w, _ in outs] + cspec,
        out_shape=[jax.ShapeDtypeStruct((r * EXP, w), d) for r, w, d in outs] + cshape,
        scratch_shapes=[pltpu.VMEM((n, D_MODEL), BF16), pltpu.VMEM((EXP, t, SG_W), F32)],
        compiler_params=_params(("arbitrary",)),
        name="sample_pre",
    )(ws4, bs4, x, *tabs, kdec_rows, *consts, *cast_weights)


def _sample_merge_kernel(x_ref, orete_ref, oxe_ref, osg_ref, gates_ref, wr_ref, wsg_ref, wx_ref, wo_ref, xo_ref):
    n = x_ref.shape[0]
    ne = n * EXP
    r = lax.broadcasted_iota(jnp.int32, (n, ne), 0)
    c = lax.broadcasted_iota(jnp.int32, (n, ne), 1)
    col = jnp.where(r == (c & (EXP - 1)) * (n // EXP) + (c >> 4), 1.0, 0.0).astype(BF16)
    oret = jnp.dot(col, orete_ref[...].astype(BF16), preferred_element_type=F32)
    ox = jnp.dot(col, oxe_ref[...].astype(BF16), preferred_element_type=F32)
    gates = gates_ref[...]
    merged = (gates[:, :D_MODEL] * _mm(oret, wr_ref[...])
              + gates[:, D_MODEL:2 * D_MODEL] * jnp.dot(osg_ref[...], wsg_ref[...], preferred_element_type=F32)
              + gates[:, 2 * D_MODEL:] * _mm(ox, wx_ref[...]))
    xo_ref[...] = x_ref[...] + _mm(merged, wo_ref[...])


def _sample_merge(x, orete, oxe, osg, gates, p):
    ins = [x, orete, oxe, osg, gates, p["w_br_ret"], p["w_br_sg"], p["w_br_x"], p["w_o"]]
    return pl.pallas_call(
        _sample_merge_kernel,
        grid=(1,),
        in_specs=[_const_spec(a.shape) for a in ins],
        out_specs=pl.BlockSpec(x.shape, lambda i: (0, 0)),
        out_shape=jax.ShapeDtypeStruct(x.shape, F32),
        compiler_params=_params(("arbitrary",)),
        name="sample_merge",
    )(*ins)


def _sample_ffn_kernel(x_ref, ng_ref, wa_ref, wb_ref, cwa_ref, cwb_ref, cba_ref, cbb_ref, s0a_ref, s0b_ref,
                       s1a_ref, s1b_ref, wdn_ref, nf_ref,
                       y_ref, c2a_ref, c2b_ref, c3a_ref, c3b_ref, h_ref, acc_ref):
    nb = x_ref.shape[0] // EXP
    j = pl.program_id(0)

    @pl.when(j == 0)
    def _():
        h_ref[...] = _rms(x_ref[...], ng_ref[...]).astype(BF16)
        acc_ref[...] = jnp.zeros_like(acc_ref)

    h = h_ref[...]

    def conv(w_ref, cw_ref, cb_ref, s0_ref, s1_ref, c2_ref, c3_ref):
        z = jnp.dot(h, w_ref[...], preferred_element_type=F32)
        zp = [s0_ref[...], s1_ref[...]] + [z[l * nb:(l + 1) * nb] for l in range(EXP)]
        c2_ref[...] = zp[EXP]
        c3_ref[...] = zp[EXP + 1]
        cw, cb = cw_ref[...], cb_ref[...]
        return jnp.concatenate([cb + cw[0:1] * zp[l] + cw[1:2] * zp[l + 1] + cw[2:3] * zp[l + 2]
                                for l in range(EXP)], axis=0)

    a = conv(wa_ref, cwa_ref, cba_ref, s0a_ref, s1a_ref, c2a_ref, c3a_ref)
    b = conv(wb_ref, cwb_ref, cbb_ref, s0b_ref, s1b_ref, c2b_ref, c3b_ref)
    acc_ref[...] += _mm(_gelu(a) * b, wdn_ref[...])

    @pl.when(j == pl.num_programs(0) - 1)
    def _():
        y_ref[...] = _rms(x_ref[...] + acc_ref[...], nf_ref[...])


def _sample_ffn(x, sc, p):
    n = x.shape[0]
    nb = n // EXP
    cw = FFN_CW
    nch = D_FF // cw
    ca = lambda rows: pl.BlockSpec((rows, cw), lambda j: (0, j))
    cb = lambda rows: pl.BlockSpec((rows, cw), lambda j: (0, nch + j))
    sa = lambda k: pl.BlockSpec((None, nb, cw), lambda j: (k, 0, j))
    sb = lambda k: pl.BlockSpec((None, nb, cw), lambda j: (k, 0, nch + j))
    full = pl.BlockSpec((n, D_MODEL), lambda j: (0, 0))
    vec = pl.BlockSpec((1, D_MODEL), lambda j: (0, 0))
    return pl.pallas_call(
        _sample_ffn_kernel,
        grid=(nch,),
        in_specs=[full, vec, ca(D_MODEL), cb(D_MODEL), ca(3), cb(3), ca(1), cb(1), sa(0), sb(0), sa(1), sb(1),
                  pl.BlockSpec((cw, D_MODEL), lambda j: (j, 0)), vec],
        out_specs=[full] + [ca(nb)] * 4,
        out_shape=[jax.ShapeDtypeStruct((n, D_MODEL), F32)] + [jax.ShapeDtypeStruct((nb, D_FF), F32)] * 4,
        scratch_shapes=[pltpu.VMEM((n, D_MODEL), BF16), pltpu.VMEM((n, D_MODEL), F32)],
        compiler_params=_params(("arbitrary",)),
        name="sample_ffn",
    )(x, p["norm_ffn_g"], p["w_up"], p["w_up"], p["conv_w"], p["conv_w"], p["conv_b"], p["conv_b"],
      sc, sc, sc, sc, p["w_down"], p["norm_final_g"])


def _rope_tables(pos, scale):
    inv = ROPE_BASE ** (-np.arange(0, DK, 2, dtype=np.float64) / DK)
    ang = np.asarray(pos, np.float64)[:, None] * inv[None, :]
    cos, sin = np.cos(ang), np.sin(ang)
    return (np.concatenate([cos, cos], -1) * scale).astype(np.float32), \
        (np.concatenate([-sin, sin], -1) * scale).astype(np.float32)


def _decay(chunk):
    lg = np.log1p(-np.exp2(-5.0 - np.arange(HEADS, dtype=np.float64)))
    n = np.arange(chunk, dtype=np.float64)
    diff = n[:, None] - n[None, :]
    dmat = np.where(diff >= 0, np.exp(np.maximum(diff, 0.0)[None] * lg[:, None, None]), 0.0)
    qdec = np.exp((n + 1.0)[None, :] * lg[:, None])
    kdec = np.exp((chunk - 1.0 - n)[None, :] * lg[:, None])
    f32 = lambda a: a.astype(np.float32)
    return f32(dmat), f32(qdec), f32(kdec), f32(np.exp(chunk * lg))


def kernel(x_prompt, x_sample, mem_prompt, state_ret, state_conv, cache_mem_k, cache_mem_v, norm_mix_g, w_in,
           b_gate, ret_gn_g, sg_ln_g, sg_ws, sg_bs, mem_norm_g, w_mem_kv, w_br_ret, w_br_sg, w_br_x, w_o,
           norm_ffn_g, w_up, conv_w, conv_b, w_down, norm_final_g):
    bp, lp, _ = x_prompt.shape
    bs, ls, _ = x_sample.shape
    assert state_ret.shape[0] == 1 and ls == EXP and lp % PROMPT_TILE == 0
    assert bs % (bp * (lp // PROMPT_TILE)) == 0
    assert bs == CHUNK
    row = lambda a: a.reshape(1, -1)
    scale = DK ** -0.5

    mk, mv, mk_b, mv_b, w_in_b = _memkv(mem_prompt, row(mem_norm_g[0]), w_mem_kv[0], [w_in[0]])
    p = dict(norm_mix_g=row(norm_mix_g[0]), w_in=w_in_b, b_gate=row(b_gate[0]),
             gn_g=row(ret_gn_g[0]), sg_ln_g=row(sg_ln_g[0]),
             norm_ffn_g=row(norm_ffn_g[0]), conv_w=conv_w[0], conv_b=row(conv_b[0]),
             norm_final_g=row(norm_final_g))

    n = bs * ls
    per = HEADS * EXP
    te = (bs // EXP) * per
    pos_s = PAST_LEN + (np.arange(te) & (ls - 1))
    tabs_s = (*_rope_tables(pos_s, 1.0), *_rope_tables(pos_s, scale))
    dmat4, qdec4, kdec4, gc4 = _decay(ls)
    kdec_rows = np.ascontiguousarray(np.broadcast_to(np.tile(kdec4.reshape(per), te // per)[:, None], (te, DK)))
    xs = jnp.swapaxes(x_sample, 0, 1).reshape(n, D_MODEL)
    pre = _sample_pre(xs, tabs_s, kdec_rows, p, sg_ws[0][:, :ls, :ls].reshape(-1), sg_bs[0][:, :ls].reshape(-1),
                      [w_br_ret[0], w_br_sg[0], w_br_x[0], w_o[0]])
    qe, kre, kde, ve, ge, xqe, osg, vrows, gates, w_br_ret_b, w_br_sg_b, w_br_x_b, w_o_b = pre
    p = dict(p, w_br_ret=w_br_ret_b, w_br_sg=w_br_sg_b, w_br_x=w_br_x_b, w_o=w_o_b)

    pos_p = np.arange(lp)
    tabs_p = (*_rope_tables(pos_p, 1.0), *_rope_tables(pos_p, scale))
    dmat, qdec, kdec, gc = _decay(CHUNK)
    bcast = lambda a: np.ascontiguousarray(np.broadcast_to(a[:, :, None], (HEADS, CHUNK, DK)))
    dec_p = dict(dmat=dmat, qdec=bcast(qdec), kdec=bcast(kdec), gc=gc)
    pp = dict(p, sg_ws=sg_ws[0], sg_bias=jnp.repeat(sg_bs[0].T, DK, axis=1))
    x_mid, s_prompt, w_up_b, w_down_b = _prompt_mixer(x_prompt, tabs_p, mk_b, mv_b, dec_p, pp,
                                                      [w_up[0], w_down[0]])
    p = dict(p, w_up=w_up_b, w_down=w_down_b)

    sbb = bs // (bp * (lp // PROMPT_TILE))
    rb = sbb * per
    blk16 = np.einsum("hk,hls->hlks", np.eye(HEADS, dtype=np.float32), dmat4).reshape(per, per)
    dec_s = dict(gc=gc4,
                 dmat=np.kron(np.eye(sbb, dtype=np.float32), blk16),
                 qdec=np.ascontiguousarray(np.broadcast_to(np.tile(qdec4.reshape(per), sbb)[:, None], (rb, DV))),
                 gn=jnp.tile(jnp.repeat(ret_gn_g[0], ls, axis=0), (sbb, 1)))
    y_prompt, tail, orete, oxe, s_sample = _prompt_ffn(
        x_mid, p, (qe, kre, kde, ve, ge, xqe), state_ret[0].reshape(bs, HEADS * DK, DV),
        cache_mem_k[0].reshape(bs, MEM_LEN * HEADS, DK), cache_mem_v[0].reshape(bs, MEM_LEN * HEADS, DK), dec_s)
    xs_mid = _sample_merge(xs, orete, oxe, osg, gates, p)
    y_sample, c2a, c2b, c3a, c3b = _sample_ffn(xs_mid, jnp.swapaxes(state_conv[0], 0, 1), p)
    conv_s = jnp.stack([jnp.concatenate([c2a, c2b], -1), jnp.concatenate([c3a, c3b], -1)], axis=1)
    unpos = lambda a: jnp.swapaxes(a.reshape(ls, bs, a.shape[-1]), 0, 1)

    return (y_prompt, unpos(y_sample),
            s_prompt[None], tail[None],
            mk.reshape(1, bp, MEM_LEN, HEADS, DK), mv.reshape(1, bp, MEM_LEN, HEADS, DK),
            s_sample.reshape(1, bs, HEADS, DK, DV), conv_s[None],
            unpos(vrows)[None])
```

```python
import numpy as np
import jax
import jax.numpy as jnp
from jax import lax
from jax.experimental import pallas as pl
from jax.experimental.pallas import tpu as pltpu

F32 = jnp.float32
BF16 = jnp.bfloat16

D_MODEL = 1024
HEADS = 4
DK = 128
DV = 256
QK_W = HEADS * DK
V_W = HEADS * DV
SG_W = 512
X_W = 512
MEM_LEN = 256
D_FF = 2816
CHUNK = 128
ROPE_BASE = 10000.0
EPS = 1e-6
PAST_LEN = 16384

C_Q, C_K, C_V, C_G, C_SU, C_SV, C_XQ, C_GT, C_END = 0, 512, 1024, 2048, 3072, 3584, 4096, 4608, 7680

PROMPT_TILE = 512
EXP = HEADS
FFN_CW = 1408
VMEM_BYTES_V7X = 64 * 1024 * 1024
VMEM_LIMIT = VMEM_BYTES_V7X - 3 * 1024 * 1024


def _rms(x, g):
    return x * lax.rsqrt(jnp.mean(x * x, axis=-1, keepdims=True) + EPS) * g


def _stdnorm(x):
    mu = jnp.mean(x, axis=-1, keepdims=True)
    xc = x - mu
    var = jnp.mean(xc * xc, axis=-1, keepdims=True)
    return xc * lax.rsqrt(var + EPS)


_GELU_C0 = np.float32(np.sqrt(2.0 / np.pi))
_GELU_C1 = np.float32(np.sqrt(2.0 / np.pi) * 0.044715)


def _gelu_tanh(x):
    return jnp.tanh(x * (_GELU_C0 + _GELU_C1 * (x * x)))


def _gelu(x):
    hx = 0.5 * x
    return hx + hx * _gelu_tanh(x)


def _sigmoid(x):
    return 0.5 + 0.5 * jnp.tanh(0.5 * x)


def _silu(x):
    hx = 0.5 * x
    return hx + hx * jnp.tanh(hx)


def _softmax(s):
    e = jnp.exp(s - jnp.max(s, axis=-1, keepdims=True))
    return e * (1.0 / jnp.sum(e, axis=-1, keepdims=True))


def _mm(a, b):
    return jnp.dot(a.astype(BF16), b.astype(BF16), preferred_element_type=F32)


def _mm_nt(a, b):
    return lax.dot_general(a.astype(BF16), b.astype(BF16), (((1,), (1,)), ((), ())),
                           preferred_element_type=F32)


def _mm_tn(a, b):
    return lax.dot_general(a.astype(BF16), b.astype(BF16), (((0,), (0,)), ((), ())),
                           preferred_element_type=F32)


def _rope(x, cos, sin):
    return x * cos + pltpu.roll(x, DK // 2, 1) * sin


def _tril(w):
    r = lax.broadcasted_iota(jnp.int32, w.shape, 0)
    c = lax.broadcasted_iota(jnp.int32, w.shape, 1)
    return jnp.where(r >= c, w, 0.0)


def _conv_gate(zs_ref, n, cols_a, cols_b, cw_a, cb_a, cw_b, cb_b):
    def conv(cols, cw, cb):
        zz = zs_ref[0:8 + n, cols]
        z1 = pltpu.roll(zz, 1, 0)[8:]
        z2 = pltpu.roll(zz, 2, 0)[8:]
        return cb + cw[0:1] * z2 + cw[1:2] * z1 + cw[2:3] * zz[8:]
    a = conv(cols_a, cw_a, cb_a)
    return (a + a * _gelu_tanh(a)) * conv(cols_b, 0.5 * cw_b, 0.5 * cb_b)


def _const_spec(shape):
    nd = len(shape)
    return pl.BlockSpec(shape, lambda *_: (0,) * nd, pipeline_mode=pl.Buffered(1))


def _smem_spec():
    return pl.BlockSpec(memory_space=pltpu.SMEM)


def _params(sem):
    return pltpu.CompilerParams(dimension_semantics=sem, vmem_limit_bytes=VMEM_LIMIT)


def _cast_blocks(refs):
    for src, dst in refs:
        dst[...] = src[...].astype(BF16)


def _cast_specs(weights, steps, flat_step):
    specs, shapes = [], []
    for w in weights:
        rows, cols = w.shape
        nblk = max(n for n in range(1, steps + 1) if steps % n == 0 and rows % (16 * n) == 0)
        specs.append(pl.BlockSpec((rows // nblk, cols), lambda *g, r=steps // nblk: (flat_step(*g) // r, 0)))
        shapes.append(jax.ShapeDtypeStruct(w.shape, BF16))
    return specs, shapes


def _memkv_kernel(mem_ref, g_ref, w_ref, *refs):
    ncast = (len(refs) - 4) // 2
    casts, (k_ref, v_ref, kb_ref, vb_ref) = refs[:ncast], refs[ncast:ncast + 4]
    kv = _mm(_rms(mem_ref[...], g_ref[...]), w_ref[...])
    k, v = kv[:, :X_W], kv[:, X_W:]
    for hd in range(HEADS):
        cols = slice(hd * DK, (hd + 1) * DK)
        k_ref[pl.ds(hd, MEM_LEN, stride=HEADS), :] = k[:, cols]
        v_ref[pl.ds(hd, MEM_LEN, stride=HEADS), :] = v[:, cols]
    kb_ref[...] = k.astype(BF16)
    vb_ref[...] = v.astype(BF16)
    _cast_blocks(zip(casts, refs[ncast + 4:]))


def _memkv(mem, g, w, cast_weights):
    b = mem.shape[0]
    blk = pl.BlockSpec((None, MEM_LEN, X_W), lambda i: (i, 0, 0))
    flat = pl.BlockSpec((None, MEM_LEN * HEADS, DK), lambda i: (i, 0, 0))
    cspec, cshape = _cast_specs(cast_weights, b, lambda i: i)
    return pl.pallas_call(
        _memkv_kernel,
        grid=(b,),
        in_specs=[pl.BlockSpec((None, MEM_LEN, D_MODEL), lambda i: (i, 0, 0)),
                  _const_spec((1, D_MODEL)), _const_spec((D_MODEL, 2 * X_W))] + cspec,
        out_specs=[flat, flat, blk, blk] + cspec,
        out_shape=[jax.ShapeDtypeStruct((b, MEM_LEN * HEADS, DK), F32)] * 2
        + [jax.ShapeDtypeStruct((b, MEM_LEN, X_W), BF16)] * 2 + cshape,
        compiler_params=_params(("arbitrary",)),
        name="mem_kv",
    )(mem, g, w, *cast_weights)


def _mixer_kernel(gc_ref, x_ref, tab_ref, mk_ref, mv_ref, ng_ref, win_ref, bg_ref,
                  gn_ref, lng_ref, ws_ref, sgb_ref, dmat_ref, qdec_ref, kdec_ref,
                  wr_ref, wsg_ref, wx_ref, wo_ref, wup_ref, wdn_ref,
                  xo_ref, s_ref, wup_o_ref, wdn_o_ref, oret_ref, osg_ref, ox_ref):
    tile = x_ref.shape[0]

    @pl.when(pl.program_id(1) == 0)
    def _():
        s_ref[...] = jnp.zeros_like(s_ref)

    x = x_ref[...]
    h = _rms(x, ng_ref[...]).astype(BF16)

    def proj(a, b):
        return jnp.dot(h, win_ref[:, a:b], preferred_element_type=F32)

    heads, chunks = range(HEADS), range(tile // CHUNK)
    rows = [slice(c * CHUNK, (c + 1) * CHUNK) for c in chunks]
    kcols = [slice(hd * DK, (hd + 1) * DK) for hd in heads]
    vcols = [slice(hd * DV, (hd + 1) * DV) for hd in heads]

    suv = proj(C_SU, C_XQ)
    su, sv = suv[:, :SG_W], suv[:, SG_W:]
    qk = proj(C_Q, C_V)
    q, k = qk[:, :QK_W], qk[:, QK_W:]
    u = _gelu(su)
    vn = (_stdnorm(_gelu(sv)) * lng_ref[...]).astype(BF16)
    wsg = [_tril(ws_ref[g]).astype(BF16) for g in heads]
    vb = proj(C_V, C_G).astype(BF16)
    mixed = [[jnp.dot(wsg[g], vn[rows[c], kcols[g]], preferred_element_type=F32) + sgb_ref[:, kcols[g]]
              for c in chunks] for g in heads]
    gsil = _silu(proj(C_G, C_SU))
    for g in heads:
        osg_ref[:, kcols[g]] = (u[:, kcols[g]] * jnp.concatenate(mixed[g], axis=0)).astype(BF16)

    cq, sq, ck, sk = (tab_ref[:, i * DK:(i + 1) * DK] for i in range(4))
    qr = [_rope(q[:, kcols[hd]], cq, sq) for hd in heads]
    kr = [_rope(k[:, kcols[hd]], ck, sk) for hd in heads]
    sc = [[_mm_nt(qr[hd][rows[c]], kr[hd][rows[c]]) for c in chunks] for hd in heads]
    upd = [[_mm_tn(kr[hd][rows[c]] * kdec_ref[hd], vb[rows[c], vcols[hd]]) for c in chunks] for hd in heads]
    xq = proj(C_XQ, C_GT)
    gmid = C_GT + (C_END - C_GT) // 2
    gt0 = proj(C_GT, gmid)
    states = []
    for hd in heads:
        st, before = s_ref[hd], []
        for c in chunks:
            before.append(st.astype(BF16))
            st = gc_ref[hd] * st + upd[hd][c]
        s_ref[hd] = st
        states.append(before)
    o = [[jnp.dot(jnp.concatenate([(sc[hd][c] * dmat_ref[hd]).astype(BF16),
                                   (qr[hd][rows[c]] * qdec_ref[hd]).astype(BF16)], axis=1),
                  jnp.concatenate([vb[rows[c], vcols[hd]], states[hd][c]], axis=0),
                  preferred_element_type=F32)
          for c in chunks] for hd in heads]
    sx = [_mm_nt(xq[:, kcols[hd]], mk_ref[:, kcols[hd]]) * np.float32(DK ** -0.5) for hd in heads]
    for hd in heads:
        on = _stdnorm(jnp.concatenate(o[hd], axis=0)) * gn_ref[:, vcols[hd]]
        oret_ref[:, vcols[hd]] = (gsil[:, vcols[hd]] * on).astype(BF16)
    gt1 = proj(gmid, C_END)
    px = [_softmax(sx[hd]) for hd in heads]
    for hd in heads:
        ox_ref[:, kcols[hd]] = _mm(px[hd], mv_ref[:, kcols[hd]]).astype(BF16)

    gates = _sigmoid(jnp.concatenate([gt0, gt1], axis=1) + bg_ref[...])
    merged = (gates[:, :D_MODEL] * jnp.dot(oret_ref[...], wr_ref[...], preferred_element_type=F32)
              + gates[:, D_MODEL:2 * D_MODEL] * jnp.dot(osg_ref[...], wsg_ref[...], preferred_element_type=F32)
              + gates[:, 2 * D_MODEL:] * jnp.dot(ox_ref[...], wx_ref[...], preferred_element_type=F32))
    xo_ref[...] = x + _mm(merged, wo_ref[...])
    _cast_blocks(((wup_ref, wup_o_ref), (wdn_ref, wdn_o_ref)))


def _prompt_mixer(x, tabs, mk_b, mv_b, dec, p, cast_weights):
    b, l, _ = x.shape
    t = PROMPT_TILE
    nt = l // t
    cspec, cshape = _cast_specs(cast_weights, b * nt, lambda i, j: i * nt + j)
    tok = lambda w: pl.BlockSpec((None, t, w), lambda i, j: (i, j, 0))
    tab = pl.BlockSpec((t, 4 * DK), lambda i, j: (j, 0))
    mem = pl.BlockSpec((None, MEM_LEN, X_W), lambda i, j: (i, 0, 0))
    consts = [p["norm_mix_g"], p["w_in"], p["b_gate"], p["gn_g"], p["sg_ln_g"], p["sg_ws"], p["sg_bias"],
              dec["dmat"], dec["qdec"], dec["kdec"], p["w_br_ret"], p["w_br_sg"], p["w_br_x"], p["w_o"]]
    return pl.pallas_call(
        _mixer_kernel,
        grid=(b, l // t),
        in_specs=[_smem_spec(), tok(D_MODEL), tab, mem, mem]
        + [_const_spec(c.shape) for c in consts] + cspec,
        out_specs=[tok(D_MODEL), pl.BlockSpec((None, HEADS, DK, DV), lambda i, j: (i, 0, 0, 0))] + cspec,
        out_shape=[jax.ShapeDtypeStruct((b, l, D_MODEL), F32),
                   jax.ShapeDtypeStruct((b, HEADS, DK, DV), F32)] + cshape,
        scratch_shapes=[pltpu.VMEM((t, V_W), BF16), pltpu.VMEM((t, SG_W), BF16), pltpu.VMEM((t, X_W), BF16)],
        compiler_params=_params(("arbitrary", "arbitrary")),
        name="prompt_mixer",
    )(dec["gc"], x, np.concatenate(tabs, axis=1), mk_b, mv_b, *consts, *cast_weights)


def _row_head(n):
    return (lax.broadcasted_iota(jnp.int32, (n, 1), 0) >> 2) & (HEADS - 1)


def _state_matmuls(gc_ref, qe_ref, kre_ref, kde_ref, ve_ref, xqe_ref, s_ref, mk_ref, dmat_ref, so_ref):
    rows_n = qe_ref.shape[0]
    per = HEADS * EXP
    batches = range(rows_n // per)
    rows = [slice(b * per, (b + 1) * per) for b in batches]
    q, ve, xq = qe_ref[...], ve_ref[...], xqe_ref[...]
    sc = _mm_nt(q, kre_ref[...]) * dmat_ref[...]
    kd = kde_ref[...]
    rowb = lax.broadcasted_iota(jnp.int32, (rows_n, 1), 0) >> 4
    cross = jnp.concatenate(
        [jnp.dot(q[rows[b]], s_ref[b].astype(BF16), preferred_element_type=F32) for b in batches], axis=0)
    upd = [_mm_tn(kd, jnp.where(rowb == b, ve, jnp.zeros_like(ve))) for b in batches]
    for b in batches:
        for hd in range(HEADS):
            hr = slice(hd * DK, (hd + 1) * DK)
            so_ref[b, hr, :] = gc_ref[hd] * s_ref[b, hr, :] + upd[b][hr]
    sx = jnp.concatenate([_mm_nt(xq[rows[b]], mk_ref[b]) for b in batches], axis=0) * np.float32(DK ** -0.5)
    return sc, cross, sx


def _state_outputs(sc, cross, sx, ve_ref, ge_ref, mv_ref, qdec_ref, gn_ref, oret_ref, ox_ref):
    rows_n = ve_ref.shape[0]
    per = HEADS * EXP
    batches = range(rows_n // per)
    rows = [slice(b * per, (b + 1) * per) for b in batches]
    inner = jnp.dot(sc.astype(BF16), ve_ref[...], preferred_element_type=F32)
    hh = _row_head(rows_n)
    own_col = (lax.broadcasted_iota(jnp.int32, (rows_n, MEM_LEN * HEADS), 1) & (HEADS - 1)) == hh
    p = _softmax(jnp.where(own_col, sx, np.float32(-1e30))).astype(BF16)
    oxs = jnp.concatenate([_mm(p[rows[b]], mv_ref[b]) for b in batches], axis=0)
    o = inner + cross * qdec_ref[...]
    og = ge_ref[...] * (_stdnorm(o) * gn_ref[...])
    for hd in range(HEADS):
        oret_ref[:, hd * DV:(hd + 1) * DV] = jnp.where(hh == hd, og, 0.0).astype(BF16)
        ox_ref[:, hd * DK:(hd + 1) * DK] = jnp.where(hh == hd, oxs, 0.0).astype(BF16)


def _ffn_kernel(gc_ref, x_ref, ng_ref, wup_ref, cw_ref, cb_ref, wdn_ref, nf_ref,
                qe_ref, kre_ref, kde_ref, ve_ref, ge_ref, xqe_ref, s_ref, mk_ref, mv_ref, dmat_ref, qdec_ref, gn_ref,
                y_ref, tail_ref, oret_ref, ox_ref, so_ref, zs_ref):
    tile = x_ref.shape[0]
    first = pl.program_id(1) == 0

    @pl.when(first)
    def _():
        zs_ref[0:8, :] = jnp.zeros((8, 2 * D_FF), F32)

    @pl.when(jnp.logical_not(first))
    def _():
        zs_ref[0:8, :] = zs_ref[tile:tile + 8, :]

    sc, cross, sx = _state_matmuls(gc_ref, qe_ref, kre_ref, kde_ref, ve_ref, xqe_ref, s_ref, mk_ref, dmat_ref,
                                   so_ref)
    x = x_ref[...]
    zs_ref[8:8 + tile, :] = _mm(_rms(x, ng_ref[...]), wup_ref[...])
    _state_outputs(sc, cross, sx, ve_ref, ge_ref, mv_ref, qdec_ref, gn_ref, oret_ref, ox_ref)
    ca, cb = slice(0, D_FF), slice(D_FF, 2 * D_FF)
    gate = _conv_gate(zs_ref, tile, ca, cb, cw_ref[:, ca], cb_ref[:, ca], cw_ref[:, cb], cb_ref[:, cb])
    y = x + _mm(gate, wdn_ref[...])
    y_ref[...] = _rms(y, nf_ref[...])
    tail_ref[...] = zs_ref[tile + 6:tile + 8, :]


def _prompt_ffn(x, p, pre, state, mk, mv, dec):
    b, l, _ = x.shape
    t = PROMPT_TILE
    nt = l // t
    qe, kre, kde, ve, ge, xqe = pre
    nb = state.shape[0]
    bb = nb // (b * nt)
    rb = bb * HEADS * EXP
    ne = qe.shape[0]
    tok = pl.BlockSpec((None, t, D_MODEL), lambda i, j: (i, j, 0))
    rowblk = lambda w: pl.BlockSpec((rb, w), lambda i, j: (i * nt + j, 0))
    batblk = lambda a, c: pl.BlockSpec((bb, a, c), lambda i, j: (i * nt + j, 0, 0))
    consts = [p["norm_ffn_g"], p["w_up"], p["conv_w"], p["conv_b"], p["w_down"], p["norm_final_g"]]
    return pl.pallas_call(
        _ffn_kernel,
        grid=(b, nt),
        in_specs=[_smem_spec(), tok] + [_const_spec(c.shape) for c in consts]
        + [rowblk(QK_W), rowblk(QK_W), rowblk(QK_W), rowblk(DV), rowblk(DV), rowblk(DK),
           batblk(HEADS * DK, DV), batblk(MEM_LEN * HEADS, DK), batblk(MEM_LEN * HEADS, DK),
           _const_spec((rb, rb)), _const_spec((rb, DV)), _const_spec((rb, DV))],
        out_specs=[tok, pl.BlockSpec((None, 2, 2 * D_FF), lambda i, j: (i, 0, 0)),
                   rowblk(V_W), rowblk(X_W), batblk(HEADS * DK, DV)],
        out_shape=[jax.ShapeDtypeStruct((b, l, D_MODEL), F32), jax.ShapeDtypeStruct((b, 2, 2 * D_FF), F32),
                   jax.ShapeDtypeStruct((ne, V_W), BF16), jax.ShapeDtypeStruct((ne, X_W), BF16),
                   jax.ShapeDtypeStruct(state.shape, F32)],
        scratch_shapes=[pltpu.VMEM((t + 8, 2 * D_FF), F32)],
        compiler_params=_params(("arbitrary", "arbitrary")),
        name="prompt_ffn",
    )(dec["gc"], x, *consts, qe, kre, kde, ve, ge, xqe, state, mk, mv, dec["dmat"], dec["qdec"], dec["gn"])


def _sample_pre_kernel(ws_ref, bs_ref, x_ref, cq_ref, sq_ref, ck_ref, sk_ref, kdec_ref, ng_ref, win_ref, bg_ref,
                       lng_ref, wr_ref, wsg_ref, wx_ref, wo_ref,
                       qe_ref, kre_ref, kde_ref, ve_ref, ge_ref, xqe_ref, osg_ref, vrows_ref, gates_ref,
                       wr_o_ref, wsg_o_ref, wx_o_ref, wo_o_ref, h_ref, vn_ref):
    n = x_ref.shape[0]
    nb = n // EXP
    ne = qe_ref.shape[0]
    i = pl.program_id(0)

    @pl.when(i == 0)
    def _():
        h_ref[...] = _rms(x_ref[...], ng_ref[...]).astype(BF16)
        vn_ref[...] = jnp.zeros_like(vn_ref)

    r = lax.broadcasted_iota(jnp.int32, (ne, n), 0)
    c = lax.broadcasted_iota(jnp.int32, (ne, n), 1)
    rep = jnp.where(c == (r & (EXP - 1)) * nb + i * (ne // (HEADS * EXP)) + (r >> 4), 1.0, 0.0).astype(BF16)
    he = jnp.dot(rep, h_ref[...], preferred_element_type=F32).astype(BF16)
    hh = _row_head(ne)
    h = h_ref[pl.ds(pl.multiple_of(i * nb, nb), nb), :]

    def proj(hm, a, b):
        return jnp.dot(hm, win_ref[:, a:b], preferred_element_type=F32)

    q, k = proj(he, C_Q, C_K), proj(he, C_K, C_V)
    xq = proj(he, C_XQ, C_GT)
    cq, sq, ck, sk, kdec = cq_ref[...], sq_ref[...], ck_ref[...], sk_ref[...], kdec_ref[...]
    for hd in range(HEADS):
        cols = slice(hd * DK, (hd + 1) * DK)
        own = hh == hd
        kr = _rope(k[:, cols], ck, sk)
        qe_ref[:, cols] = jnp.where(own, _rope(q[:, cols], cq, sq), 0.0).astype(BF16)
        kre_ref[:, cols] = jnp.where(own, kr, 0.0).astype(BF16)
        kde_ref[:, cols] = jnp.where(own, kr * kdec, 0.0).astype(BF16)
    v = proj(he, C_V, C_G)
    g = proj(he, C_G, C_SU)
    ve = jnp.zeros((ne, DV), F32)
    ge = jnp.zeros((ne, DV), F32)
    xqc = jnp.zeros((ne, DK), F32)
    for hd in range(HEADS):
        cols = slice(hd * DV, (hd + 1) * DV)
        own = hh == hd
        ve = jnp.where(own, v[:, cols], ve)
        ge = jnp.where(own, g[:, cols], ge)
        xqc = jnp.where(own, xq[:, hd * DK:(hd + 1) * DK], xqc)
    xqe_ref[...] = xqc.astype(BF16)
    ve_ref[...] = ve.astype(BF16)
    ge_ref[...] = _silu(ge)

    u = _gelu(proj(h, C_SU, C_SV))
    vn = _stdnorm(_gelu(proj(h, C_SV, C_XQ))) * lng_ref[...]
    vrows_ref[...] = vn
    vn_ref[i] = vn
    for g in range(HEADS):
        cols = slice(g * DK, (g + 1) * DK)
        mixed = jnp.full((nb, DK), bs_ref[g * EXP + i], F32)
        for s in range(EXP):
            w = jnp.where(s <= i, ws_ref[(g * EXP + i) * EXP + s], 0.0)
            mixed = mixed + w * vn_ref[s, :, cols]
        osg_ref[:, cols] = (u[:, cols] * mixed).astype(BF16)
    gates_ref[...] = _sigmoid(proj(h, C_GT, C_END) + bg_ref[...])
    _cast_blocks(((wr_ref, wr_o_ref), (wsg_ref, wsg_o_ref), (wx_ref, wx_o_ref), (wo_ref, wo_o_ref)))


def _sample_pre(x, tabs, kdec_rows, p, ws4, bs4, cast_weights):
    n = x.shape[0]
    cspec, cshape = _cast_specs(cast_weights, EXP, lambda i: i)
    t = n // EXP
    te = (t // EXP) * HEADS * EXP
    consts = [p["norm_mix_g"], p["w_in"], p["b_gate"], p["sg_ln_g"]]
    rowblk = lambda rows, w: pl.BlockSpec((rows, w), lambda i: (i, 0))
    outs = [(te, QK_W, BF16), (te, QK_W, BF16), (te, QK_W, BF16), (te, DV, BF16), (te, DV, F32),
            (te, DK, BF16), (t, SG_W, BF16), (t, SG_W, F32), (t, 3 * D_MODEL, F32)]
    return pl.pallas_call(
        _sample_pre_kernel,
        grid=(EXP,),
        in_specs=[_smem_spec(), _smem_spec(), _const_spec(x.shape)] + [_const_spec((te, DK))] * 5
        + [_const_spec(c.shape) for c in consts] + cspec,
        out_specs=[rowblk(r, w) for r, w, _ in outs] + cspec,
        out_shape=[jax.ShapeDtypeStruct((r * EXP, w), d) for r, w, d in outs] + cshape,
        scratch_shapes=[pltpu.VMEM((n, D_MODEL), BF16), pltpu.VMEM((EXP, t, SG_W), F32)],
        compiler_params=_params(("arbitrary",)),
        name="sample_pre",
    )(ws4, bs4, x, *tabs, kdec_rows, *consts, *cast_weights)


def _sample_merge_kernel(x_ref, orete_ref, oxe_ref, osg_ref, gates_ref, wr_ref, wsg_ref, wx_ref, wo_ref, xo_ref):
    n = x_ref.shape[0]
    ne = n * EXP
    r = lax.broadcasted_iota(jnp.int32, (n, ne), 0)
    c = lax.broadcasted_iota(jnp.int32, (n, ne), 1)
    col = jnp.where(r == (c & (EXP - 1)) * (n // EXP) + (c >> 4), 1.0, 0.0).astype(BF16)
    oret = jnp.dot(col, orete_ref[...], preferred_element_type=F32)
    ox = jnp.dot(col, oxe_ref[...], preferred_element_type=F32)
    gates = gates_ref[...]
    merged = (gates[:, :D_MODEL] * _mm(oret, wr_ref[...])
              + gates[:, D_MODEL:2 * D_MODEL] * jnp.dot(osg_ref[...], wsg_ref[...], preferred_element_type=F32)
              + gates[:, 2 * D_MODEL:] * _mm(ox, wx_ref[...]))
    xo_ref[...] = x_ref[...] + _mm(merged, wo_ref[...])


def _sample_merge(x, orete, oxe, osg, gates, p):
    ins = [x, orete, oxe, osg, gates, p["w_br_ret"], p["w_br_sg"], p["w_br_x"], p["w_o"]]
    return pl.pallas_call(
        _sample_merge_kernel,
        grid=(1,),
        in_specs=[_const_spec(a.shape) for a in ins],
        out_specs=pl.BlockSpec(x.shape, lambda i: (0, 0)),
        out_shape=jax.ShapeDtypeStruct(x.shape, F32),
        compiler_params=_params(("arbitrary",)),
        name="sample_merge",
    )(*ins)


def _sample_ffn_kernel(x_ref, ng_ref, wa_ref, wb_ref, cwa_ref, cwb_ref, cba_ref, cbb_ref, s0a_ref, s0b_ref,
                       s1a_ref, s1b_ref, wdn_ref, nf_ref,
                       y_ref, c2a_ref, c2b_ref, c3a_ref, c3b_ref, h_ref, acc_ref):
    nb = x_ref.shape[0] // EXP
    j = pl.program_id(0)

    @pl.when(j == 0)
    def _():
        h_ref[...] = _rms(x_ref[...], ng_ref[...]).astype(BF16)
        acc_ref[...] = jnp.zeros_like(acc_ref)

    h = h_ref[...]

    def conv(w_ref, cw_ref, cb_ref, s0_ref, s1_ref, c2_ref, c3_ref):
        z = jnp.dot(h, w_ref[...], preferred_element_type=F32)
        zp = [s0_ref[...], s1_ref[...]] + [z[l * nb:(l + 1) * nb] for l in range(EXP)]
        c2_ref[...] = zp[EXP]
        c3_ref[...] = zp[EXP + 1]
        cw, cb = cw_ref[...], cb_ref[...]
        return jnp.concatenate([cb + cw[0:1] * zp[l] + cw[1:2] * zp[l + 1] + cw[2:3] * zp[l + 2]
                                for l in range(EXP)], axis=0)

    a = conv(wa_ref, cwa_ref, cba_ref, s0a_ref, s1a_ref, c2a_ref, c3a_ref)
    b = conv(wb_ref, cwb_ref, cbb_ref, s0b_ref, s1b_ref, c2b_ref, c3b_ref)
    acc_ref[...] += _mm(_gelu(a) * b, wdn_ref[...])

    @pl.when(j == pl.num_programs(0) - 1)
    def _():
        y_ref[...] = _rms(x_ref[...] + acc_ref[...], nf_ref[...])


def _sample_ffn(x, sc, p):
    n = x.shape[0]
    nb = n // EXP
    cw = FFN_CW
    nch = D_FF // cw
    ca = lambda rows: pl.BlockSpec((rows, cw), lambda j: (0, j))
    cb = lambda rows: pl.BlockSpec((rows, cw), lambda j: (0, nch + j))
    sa = lambda k: pl.BlockSpec((None, nb, cw), lambda j: (k, 0, j))
    sb = lambda k: pl.BlockSpec((None, nb, cw), lambda j: (k, 0, nch + j))
    full = pl.BlockSpec((n, D_MODEL), lambda j: (0, 0))
    vec = pl.BlockSpec((1, D_MODEL), lambda j: (0, 0))
    return pl.pallas_call(
        _sample_ffn_kernel,
        grid=(nch,),
        in_specs=[full, vec, ca(D_MODEL), cb(D_MODEL), ca(3), cb(3), ca(1), cb(1), sa(0), sb(0), sa(1), sb(1),
                  pl.BlockSpec((cw, D_MODEL), lambda j: (j, 0)), vec],
        out_specs=[full] + [ca(nb)] * 4,
        out_shape=[jax.ShapeDtypeStruct((n, D_MODEL), F32)] + [jax.ShapeDtypeStruct((nb, D_FF), F32)] * 4,
        scratch_shapes=[pltpu.VMEM((n, D_MODEL), BF16), pltpu.VMEM((n, D_MODEL), F32)],
        compiler_params=_params(("arbitrary",)),
        name="sample_ffn",
    )(x, p["norm_ffn_g"], p["w_up"], p["w_up"], p["conv_w"], p["conv_w"], p["conv_b"], p["conv_b"],
      sc, sc, sc, sc, p["w_down"], p["norm_final_g"])


def _rope_tables(pos, scale):
    inv = ROPE_BASE ** (-np.arange(0, DK, 2, dtype=np.float64) / DK)
    ang = np.asarray(pos, np.float64)[:, None] * inv[None, :]
    cos, sin = np.cos(ang), np.sin(ang)
    return (np.concatenate([cos, cos], -1) * scale).astype(np.float32), \
        (np.concatenate([-sin, sin], -1) * scale).astype(np.float32)


def _decay(chunk):
    lg = np.log1p(-np.exp2(-5.0 - np.arange(HEADS, dtype=np.float64)))
    n = np.arange(chunk, dtype=np.float64)
    diff = n[:, None] - n[None, :]
    dmat = np.where(diff >= 0, np.exp(np.maximum(diff, 0.0)[None] * lg[:, None, None]), 0.0)
    qdec = np.exp((n + 1.0)[None, :] * lg[:, None])
    kdec = np.exp((chunk - 1.0 - n)[None, :] * lg[:, None])
    f32 = lambda a: a.astype(np.float32)
    return f32(dmat), f32(qdec), f32(kdec), f32(np.exp(chunk * lg))


def kernel(x_prompt, x_sample, mem_prompt, state_ret, state_conv, cache_mem_k, cache_mem_v, norm_mix_g, w_in,
           b_gate, ret_gn_g, sg_ln_g, sg_ws, sg_bs, mem_norm_g, w_mem_kv, w_br_ret, w_br_sg, w_br_x, w_o,
           norm_ffn_g, w_up, conv_w, conv_b, w_down, norm_final_g):
    bp, lp, _ = x_prompt.shape
    bs, ls, _ = x_sample.shape
    assert state_ret.shape[0] == 1 and ls == EXP and lp % PROMPT_TILE == 0
    assert bs % (bp * (lp // PROMPT_TILE)) == 0
    assert bs == CHUNK
    row = lambda a: a.reshape(1, -1)
    scale = DK ** -0.5

    mk, mv, mk_b, mv_b, w_in_b = _memkv(mem_prompt, row(mem_norm_g[0]), w_mem_kv[0], [w_in[0]])
    p = dict(norm_mix_g=row(norm_mix_g[0]), w_in=w_in_b, b_gate=row(b_gate[0]),
             gn_g=row(ret_gn_g[0]), sg_ln_g=row(sg_ln_g[0]),
             norm_ffn_g=row(norm_ffn_g[0]), conv_w=conv_w[0], conv_b=row(conv_b[0]),
             norm_final_g=row(norm_final_g))

    n = bs * ls
    per = HEADS * EXP
    te = (bs // EXP) * per
    pos_s = PAST_LEN + (np.arange(te) & (ls - 1))
    tabs_s = (*_rope_tables(pos_s, 1.0), *_rope_tables(pos_s, scale))
    dmat4, qdec4, kdec4, gc4 = _decay(ls)
    kdec_rows = np.ascontiguousarray(np.broadcast_to(np.tile(kdec4.reshape(per), te // per)[:, None], (te, DK)))
    xs = jnp.swapaxes(x_sample, 0, 1).reshape(n, D_MODEL)
    pre = _sample_pre(xs, tabs_s, kdec_rows, p, sg_ws[0][:, :ls, :ls].reshape(-1), sg_bs[0][:, :ls].reshape(-1),
                      [w_br_ret[0], w_br_sg[0], w_br_x[0], w_o[0]])
    qe, kre, kde, ve, ge, xqe, osg, vrows, gates, w_br_ret_b, w_br_sg_b, w_br_x_b, w_o_b = pre
    p = dict(p, w_br_ret=w_br_ret_b, w_br_sg=w_br_sg_b, w_br_x=w_br_x_b, w_o=w_o_b)

    pos_p = np.arange(lp)
    tabs_p = (*_rope_tables(pos_p, 1.0), *_rope_tables(pos_p, scale))
    dmat, qdec, kdec, gc = _decay(CHUNK)
    bcast = lambda a: np.ascontiguousarray(np.broadcast_to(a[:, :, None], (HEADS, CHUNK, DK)))
    dec_p = dict(dmat=dmat, qdec=bcast(qdec), kdec=bcast(kdec), gc=gc)
    pp = dict(p, sg_ws=sg_ws[0], sg_bias=jnp.repeat(sg_bs[0].T, DK, axis=1))
    x_mid, s_prompt, w_up_b, w_down_b = _prompt_mixer(x_prompt, tabs_p, mk_b, mv_b, dec_p, pp,
                                                      [w_up[0], w_down[0]])
    p = dict(p, w_up=w_up_b, w_down=w_down_b)

    sbb = bs // (bp * (lp // PROMPT_TILE))
    rb = sbb * per
    blk16 = np.einsum("hk,hls->hlks", np.eye(HEADS, dtype=np.float32), dmat4).reshape(per, per)
    dec_s = dict(gc=gc4,
                 dmat=np.kron(np.eye(sbb, dtype=np.float32), blk16),
                 qdec=np.ascontiguousarray(np.broadcast_to(np.tile(qdec4.reshape(per), sbb)[:, None], (rb, DV))),
                 gn=jnp.tile(jnp.repeat(ret_gn_g[0], ls, axis=0), (sbb, 1)))
    y_prompt, tail, orete, oxe, s_sample = _prompt_ffn(
        x_mid, p, (qe, kre, kde, ve, ge, xqe), state_ret[0].reshape(bs, HEADS * DK, DV),
        cache_mem_k[0].reshape(bs, MEM_LEN * HEADS, DK), cache_mem_v[0].reshape(bs, MEM_LEN * HEADS, DK), dec_s)
    xs_mid = _sample_merge(xs, orete, oxe, osg, gates, p)
    y_sample, c2a, c2b, c3a, c3b = _sample_ffn(xs_mid, jnp.swapaxes(state_conv[0], 0, 1), p)
    conv_s = jnp.stack([jnp.concatenate([c2a, c2b], -1), jnp.concatenate([c3a, c3b], -1)], axis=1)
    unpos = lambda a: jnp.swapaxes(a.reshape(ls, bs, a.shape[-1]), 0, 1)

    return (y_prompt, unpos(y_sample),
            s_prompt[None], tail[None],
            mk.reshape(1, bp, MEM_LEN, HEADS, DK), mv.reshape(1, bp, MEM_LEN, HEADS, DK),
            s_sample.reshape(1, bs, HEADS, DK, DV), conv_s[None],
            unpos(vrows)[None])
```

```python
import numpy as np
import jax
import jax.numpy as jnp
from jax import lax
from jax.experimental import pallas as pl
from jax.experimental.pallas import tpu as pltpu

F32 = jnp.float32
BF16 = jnp.bfloat16

D_MODEL = 1024
HEADS = 4
DK = 128
DV = 256
QK_W = HEADS * DK
V_W = HEADS * DV
SG_W = 512
X_W = 512
MEM_LEN = 256
D_FF = 2816
CHUNK = 128
ROPE_BASE = 10000.0
EPS = 1e-6
PAST_LEN = 16384

C_Q, C_K, C_V, C_G, C_SU, C_SV, C_XQ, C_GT, C_END = 0, 512, 1024, 2048, 3072, 3584, 4096, 4608, 7680

PROMPT_TILE = 512
EXP = HEADS
FFN_CW = 1408
VMEM_BYTES_V7X = 64 * 1024 * 1024
VMEM_LIMIT = VMEM_BYTES_V7X - 3 * 1024 * 1024


def _rms(x, g):
    return x * lax.rsqrt(jnp.mean(x * x, axis=-1, keepdims=True) + EPS) * g


def _stdnorm(x):
    mu = jnp.mean(x, axis=-1, keepdims=True)
    xc = x - mu
    var = jnp.mean(xc * xc, axis=-1, keepdims=True)
    return xc * lax.rsqrt(var + EPS)


_GELU_C0 = np.float32(np.sqrt(2.0 / np.pi))
_GELU_C1 = np.float32(np.sqrt(2.0 / np.pi) * 0.044715)


def _gelu_tanh(x):
    return jnp.tanh(x * (_GELU_C0 + _GELU_C1 * (x * x)))


def _gelu(x):
    hx = 0.5 * x
    return hx + hx * _gelu_tanh(x)


def _sigmoid(x):
    return 0.5 + 0.5 * jnp.tanh(0.5 * x)


def _silu(x):
    hx = 0.5 * x
    return hx + hx * jnp.tanh(hx)


def _softmax(s):
    e = jnp.exp(s - jnp.max(s, axis=-1, keepdims=True))
    return e * (1.0 / jnp.sum(e, axis=-1, keepdims=True))


def _mm(a, b):
    return jnp.dot(a.astype(BF16), b.astype(BF16), preferred_element_type=F32)


def _mm_nt(a, b):
    return lax.dot_general(a.astype(BF16), b.astype(BF16), (((1,), (1,)), ((), ())),
                           preferred_element_type=F32)


def _mm_tn(a, b):
    return lax.dot_general(a.astype(BF16), b.astype(BF16), (((0,), (0,)), ((), ())),
                           preferred_element_type=F32)


def _rope(x, cos, sin):
    return x * cos + pltpu.roll(x, DK // 2, 1) * sin


def _tril(w):
    r = lax.broadcasted_iota(jnp.int32, w.shape, 0)
    c = lax.broadcasted_iota(jnp.int32, w.shape, 1)
    return jnp.where(r >= c, w, 0.0)


def _conv_gate(zs_ref, n, cols_a, cols_b, cw_a, cb_a, cw_b, cb_b):
    def conv(cols, cw, cb):
        zz = zs_ref[0:8 + n, cols]
        z1 = pltpu.roll(zz, 1, 0)[8:]
        z2 = pltpu.roll(zz, 2, 0)[8:]
        return cb + cw[0:1] * z2 + cw[1:2] * z1 + cw[2:3] * zz[8:]
    a = conv(cols_a, cw_a, cb_a)
    return (a + a * _gelu_tanh(a)) * conv(cols_b, 0.5 * cw_b, 0.5 * cb_b)


def _const_spec(shape):
    nd = len(shape)
    return pl.BlockSpec(shape, lambda *_: (0,) * nd, pipeline_mode=pl.Buffered(1))


def _smem_spec():
    return pl.BlockSpec(memory_space=pltpu.SMEM)


def _params(sem):
    return pltpu.CompilerParams(dimension_semantics=sem, vmem_limit_bytes=VMEM_LIMIT)


def _cast_blocks(refs):
    for src, dst in refs:
        dst[...] = src[...].astype(BF16)


def _cast_specs(weights, steps, flat_step):
    specs, shapes = [], []
    for w in weights:
        rows, cols = w.shape
        nblk = max(n for n in range(1, steps + 1) if steps % n == 0 and rows % (16 * n) == 0)
        specs.append(pl.BlockSpec((rows // nblk, cols), lambda *g, r=steps // nblk: (flat_step(*g) // r, 0)))
        shapes.append(jax.ShapeDtypeStruct(w.shape, BF16))
    return specs, shapes


def _memkv_kernel(mem_ref, g_ref, w_ref, *refs):
    ncast = (len(refs) - 4) // 2
    casts, (k_ref, v_ref, kb_ref, vb_ref) = refs[:ncast], refs[ncast:ncast + 4]
    kv = _mm(_rms(mem_ref[...], g_ref[...]), w_ref[...])
    k, v = kv[:, :X_W], kv[:, X_W:]
    for hd in range(HEADS):
        cols = slice(hd * DK, (hd + 1) * DK)
        k_ref[pl.ds(hd, MEM_LEN, stride=HEADS), :] = k[:, cols]
        v_ref[pl.ds(hd, MEM_LEN, stride=HEADS), :] = v[:, cols]
    kb_ref[...] = k.astype(BF16)
    vb_ref[...] = v.astype(BF16)
    _cast_blocks(zip(casts, refs[ncast + 4:]))


def _memkv(mem, g, w, cast_weights):
    b = mem.shape[0]
    blk = pl.BlockSpec((None, MEM_LEN, X_W), lambda i: (i, 0, 0))
    flat = pl.BlockSpec((None, MEM_LEN * HEADS, DK), lambda i: (i, 0, 0))
    cspec, cshape = _cast_specs(cast_weights, b, lambda i: i)
    return pl.pallas_call(
        _memkv_kernel,
        grid=(b,),
        in_specs=[pl.BlockSpec((None, MEM_LEN, D_MODEL), lambda i: (i, 0, 0)),
                  _const_spec((1, D_MODEL)), _const_spec((D_MODEL, 2 * X_W))] + cspec,
        out_specs=[flat, flat, blk, blk] + cspec,
        out_shape=[jax.ShapeDtypeStruct((b, MEM_LEN * HEADS, DK), F32)] * 2
        + [jax.ShapeDtypeStruct((b, MEM_LEN, X_W), BF16)] * 2 + cshape,
        compiler_params=_params(("arbitrary",)),
        name="mem_kv",
    )(mem, g, w, *cast_weights)


def _mixer_kernel(gc_ref, x_ref, xnext_ref, tab_ref, mk_ref, mv_ref, ng_ref, win_ref, bg_ref,
                  gn_ref, lng_ref, ws_ref, sgb_ref, dmat_ref, qdec_ref, kdec_ref,
                  wr_ref, wsg_ref, wx_ref, wo_ref, wup_ref, wdn_ref,
                  xo_ref, s_ref, wup_o_ref, wdn_o_ref, oret_ref, osg_ref, ox_ref, h_ref, lead_ref):
    tile = x_ref.shape[0]

    def start_tile(x):
        hn = _rms(x, ng_ref[...]).astype(BF16)
        h_ref[...] = hn
        lead_ref[...] = jnp.dot(hn, win_ref[:, C_SU:C_XQ], preferred_element_type=F32)

    @pl.when(jnp.logical_and(pl.program_id(0) == 0, pl.program_id(1) == 0))
    def _():
        start_tile(x_ref[...])

    @pl.when(pl.program_id(1) == 0)
    def _():
        s_ref[...] = jnp.zeros_like(s_ref)

    def proj(a, b):
        return jnp.dot(h_ref[...], win_ref[:, a:b], preferred_element_type=F32)

    heads, chunks = range(HEADS), range(tile // CHUNK)
    rows = [slice(c * CHUNK, (c + 1) * CHUNK) for c in chunks]
    kcols = [slice(hd * DK, (hd + 1) * DK) for hd in heads]
    vcols = [slice(hd * DV, (hd + 1) * DV) for hd in heads]

    su, sv = lead_ref[:, :SG_W], lead_ref[:, SG_W:]
    qk = proj(C_Q, C_V)
    q, k = qk[:, :QK_W], qk[:, QK_W:]
    u = _gelu(su)
    vn = (_stdnorm(_gelu(sv)) * lng_ref[...]).astype(BF16)
    wsg = [_tril(ws_ref[g]).astype(BF16) for g in heads]
    vb = proj(C_V, C_G).astype(BF16)
    mixed = [[jnp.dot(wsg[g], vn[rows[c], kcols[g]], preferred_element_type=F32) + sgb_ref[:, kcols[g]]
              for c in chunks] for g in heads]
    gsil = _silu(proj(C_G, C_SU))
    for g in heads:
        osg_ref[:, kcols[g]] = (u[:, kcols[g]] * jnp.concatenate(mixed[g], axis=0)).astype(BF16)

    cq, sq, ck, sk = (tab_ref[:, i * DK:(i + 1) * DK] for i in range(4))
    qr = [_rope(q[:, kcols[hd]], cq, sq) for hd in heads]
    kr = [_rope(k[:, kcols[hd]], ck, sk) for hd in heads]
    sc = [[_mm_nt(qr[hd][rows[c]], kr[hd][rows[c]]) for c in chunks] for hd in heads]
    upd = [[_mm_tn(kr[hd][rows[c]] * kdec_ref[hd], vb[rows[c], vcols[hd]]) for c in chunks] for hd in heads]
    xq = proj(C_XQ, C_GT)
    gmid = C_GT + (C_END - C_GT) // 2
    gt0 = proj(C_GT, gmid)
    states = []
    for hd in heads:
        st, before = s_ref[hd], []
        for c in chunks:
            before.append(st.astype(BF16))
            st = gc_ref[hd] * st + upd[hd][c]
        s_ref[hd] = st
        states.append(before)
    o = [[jnp.dot(jnp.concatenate([(sc[hd][c] * dmat_ref[hd]).astype(BF16),
                                   (qr[hd][rows[c]] * qdec_ref[hd]).astype(BF16)], axis=1),
                  jnp.concatenate([vb[rows[c], vcols[hd]], states[hd][c]], axis=0),
                  preferred_element_type=F32)
          for c in chunks] for hd in heads]
    sx = [_mm_nt(xq[:, kcols[hd]], mk_ref[:, kcols[hd]]) * np.float32(DK ** -0.5) for hd in heads]
    for hd in heads:
        on = _stdnorm(jnp.concatenate(o[hd], axis=0)) * gn_ref[:, vcols[hd]]
        oret_ref[:, vcols[hd]] = (gsil[:, vcols[hd]] * on).astype(BF16)
    gt1 = proj(gmid, C_END)
    px = [_softmax(sx[hd]) for hd in heads]
    for hd in heads:
        ox_ref[:, kcols[hd]] = _mm(px[hd], mv_ref[:, kcols[hd]]).astype(BF16)

    gates = _sigmoid(jnp.concatenate([gt0, gt1], axis=1) + bg_ref[...])
    merged = (gates[:, :D_MODEL] * jnp.dot(oret_ref[...], wr_ref[...], preferred_element_type=F32)
              + gates[:, D_MODEL:2 * D_MODEL] * jnp.dot(osg_ref[...], wsg_ref[...], preferred_element_type=F32)
              + gates[:, 2 * D_MODEL:] * jnp.dot(ox_ref[...], wx_ref[...], preferred_element_type=F32))
    out = _mm(merged, wo_ref[...])
    start_tile(xnext_ref[...])
    xo_ref[...] = x_ref[...] + out
    _cast_blocks(((wup_ref, wup_o_ref), (wdn_ref, wdn_o_ref)))


def _prompt_mixer(x, tabs, mk_b, mv_b, dec, p, cast_weights):
    b, l, _ = x.shape
    t = PROMPT_TILE
    nt = l // t
    cspec, cshape = _cast_specs(cast_weights, b * nt, lambda i, j: i * nt + j)
    tok = lambda w: pl.BlockSpec((None, t, w), lambda i, j: (i, j, 0))

    def next_tile(i, j):
        step = jnp.minimum(i * nt + j + 1, b * nt - 1)
        return step // nt, step % nt, 0

    tab = pl.BlockSpec((t, 4 * DK), lambda i, j: (j, 0))
    mem = pl.BlockSpec((None, MEM_LEN, X_W), lambda i, j: (i, 0, 0))
    consts = [p["norm_mix_g"], p["w_in"], p["b_gate"], p["gn_g"], p["sg_ln_g"], p["sg_ws"], p["sg_bias"],
              dec["dmat"], dec["qdec"], dec["kdec"], p["w_br_ret"], p["w_br_sg"], p["w_br_x"], p["w_o"]]
    return pl.pallas_call(
        _mixer_kernel,
        grid=(b, l // t),
        in_specs=[_smem_spec(), tok(D_MODEL), pl.BlockSpec((None, t, D_MODEL), next_tile), tab, mem, mem]
        + [_const_spec(c.shape) for c in consts] + cspec,
        out_specs=[tok(D_MODEL), pl.BlockSpec((None, HEADS, DK, DV), lambda i, j: (i, 0, 0, 0))] + cspec,
        out_shape=[jax.ShapeDtypeStruct((b, l, D_MODEL), F32),
                   jax.ShapeDtypeStruct((b, HEADS, DK, DV), F32)] + cshape,
        scratch_shapes=[pltpu.VMEM((t, V_W), BF16), pltpu.VMEM((t, SG_W), BF16), pltpu.VMEM((t, X_W), BF16),
                        pltpu.VMEM((t, D_MODEL), BF16), pltpu.VMEM((t, C_XQ - C_SU), F32)],
        compiler_params=_params(("arbitrary", "arbitrary")),
        name="prompt_mixer",
    )(dec["gc"], x, x, np.concatenate(tabs, axis=1), mk_b, mv_b, *consts, *cast_weights)


def _row_head(n):
    return (lax.broadcasted_iota(jnp.int32, (n, 1), 0) >> 2) & (HEADS - 1)


def _state_matmuls(gc_ref, qe_ref, kre_ref, kde_ref, ve_ref, xqe_ref, s_ref, mk_ref, dmat_ref, so_ref):
    rows_n = qe_ref.shape[0]
    per = HEADS * EXP
    batches = range(rows_n // per)
    rows = [slice(b * per, (b + 1) * per) for b in batches]
    q, ve, xq = qe_ref[...], ve_ref[...], xqe_ref[...]
    sc = _mm_nt(q, kre_ref[...]) * dmat_ref[...]
    kd = kde_ref[...]
    rowb = lax.broadcasted_iota(jnp.int32, (rows_n, 1), 0) >> 4
    cross = jnp.concatenate(
        [jnp.dot(q[rows[b]], s_ref[b].astype(BF16), preferred_element_type=F32) for b in batches], axis=0)
    upd = [_mm_tn(kd, jnp.where(rowb == b, ve, jnp.zeros_like(ve))) for b in batches]
    for b in batches:
        for hd in range(HEADS):
            hr = slice(hd * DK, (hd + 1) * DK)
            so_ref[b, hr, :] = gc_ref[hd] * s_ref[b, hr, :] + upd[b][hr]
    sx = jnp.concatenate([_mm_nt(xq[rows[b]], mk_ref[b]) for b in batches], axis=0) * np.float32(DK ** -0.5)
    return sc, cross, sx


def _state_outputs(sc, cross, sx, ve_ref, ge_ref, mv_ref, qdec_ref, gn_ref, oret_ref, ox_ref):
    rows_n = ve_ref.shape[0]
    per = HEADS * EXP
    batches = range(rows_n // per)
    rows = [slice(b * per, (b + 1) * per) for b in batches]
    inner = jnp.dot(sc.astype(BF16), ve_ref[...], preferred_element_type=F32)
    hh = _row_head(rows_n)
    own_col = (lax.broadcasted_iota(jnp.int32, (rows_n, MEM_LEN * HEADS), 1) & (HEADS - 1)) == hh
    p = _softmax(jnp.where(own_col, sx, np.float32(-1e30))).astype(BF16)
    oxs = jnp.concatenate([_mm(p[rows[b]], mv_ref[b]) for b in batches], axis=0)
    o = inner + cross * qdec_ref[...]
    og = ge_ref[...] * (_stdnorm(o) * gn_ref[...])
    for hd in range(HEADS):
        oret_ref[:, hd * DV:(hd + 1) * DV] = jnp.where(hh == hd, og, 0.0).astype(BF16)
        ox_ref[:, hd * DK:(hd + 1) * DK] = jnp.where(hh == hd, oxs, 0.0).astype(BF16)


def _ffn_kernel(gc_ref, x_ref, ng_ref, wup_ref, cw_ref, cb_ref, wdn_ref, nf_ref,
                qe_ref, kre_ref, kde_ref, ve_ref, ge_ref, xqe_ref, s_ref, mk_ref, mv_ref, dmat_ref, qdec_ref, gn_ref,
                y_ref, tail_ref, oret_ref, ox_ref, so_ref, zs_ref):
    tile = x_ref.shape[0]
    first = pl.program_id(1) == 0

    @pl.when(first)
    def _():
        zs_ref[0:8, :] = jnp.zeros((8, 2 * D_FF), F32)

    @pl.when(jnp.logical_not(first))
    def _():
        zs_ref[0:8, :] = zs_ref[tile:tile + 8, :]

    sc, cross, sx = _state_matmuls(gc_ref, qe_ref, kre_ref, kde_ref, ve_ref, xqe_ref, s_ref, mk_ref, dmat_ref,
                                   so_ref)
    x = x_ref[...]
    zs_ref[8:8 + tile, :] = _mm(_rms(x, ng_ref[...]), wup_ref[...])
    _state_outputs(sc, cross, sx, ve_ref, ge_ref, mv_ref, qdec_ref, gn_ref, oret_ref, ox_ref)
    ca, cb = slice(0, D_FF), slice(D_FF, 2 * D_FF)
    gate = _conv_gate(zs_ref, tile, ca, cb, cw_ref[:, ca], cb_ref[:, ca], cw_ref[:, cb], cb_ref[:, cb])
    y = x + _mm(gate, wdn_ref[...])
    y_ref[...] = _rms(y, nf_ref[...])
    tail_ref[...] = zs_ref[tile + 6:tile + 8, :]


def _prompt_ffn(x, p, pre, state, mk, mv, dec):
    b, l, _ = x.shape
    t = PROMPT_TILE
    nt = l // t
    qe, kre, kde, ve, ge, xqe = pre
    nb = state.shape[0]
    bb = nb // (b * nt)
    rb = bb * HEADS * EXP
    ne = qe.shape[0]
    tok = pl.BlockSpec((None, t, D_MODEL), lambda i, j: (i, j, 0))
    rowblk = lambda w: pl.BlockSpec((rb, w), lambda i, j: (i * nt + j, 0))
    batblk = lambda a, c: pl.BlockSpec((bb, a, c), lambda i, j: (i * nt + j, 0, 0))
    consts = [p["norm_ffn_g"], p["w_up"], p["conv_w"], p["conv_b"], p["w_down"], p["norm_final_g"]]
    return pl.pallas_call(
        _ffn_kernel,
        grid=(b, nt),
        in_specs=[_smem_spec(), tok] + [_const_spec(c.shape) for c in consts]
        + [rowblk(QK_W), rowblk(QK_W), rowblk(QK_W), rowblk(DV), rowblk(DV), rowblk(DK),
           batblk(HEADS * DK, DV), batblk(MEM_LEN * HEADS, DK), batblk(MEM_LEN * HEADS, DK),
           _const_spec((rb, rb)), _const_spec((rb, DV)), _const_spec((rb, DV))],
        out_specs=[tok, pl.BlockSpec((None, 2, 2 * D_FF), lambda i, j: (i, 0, 0)),
                   rowblk(V_W), rowblk(X_W), batblk(HEADS * DK, DV)],
        out_shape=[jax.ShapeDtypeStruct((b, l, D_MODEL), F32), jax.ShapeDtypeStruct((b, 2, 2 * D_FF), F32),
                   jax.ShapeDtypeStruct((ne, V_W), BF16), jax.ShapeDtypeStruct((ne, X_W), BF16),
                   jax.ShapeDtypeStruct(state.shape, F32)],
        scratch_shapes=[pltpu.VMEM((t + 8, 2 * D_FF), F32)],
        compiler_params=_params(("arbitrary", "arbitrary")),
        name="prompt_ffn",
    )(dec["gc"], x, *consts, qe, kre, kde, ve, ge, xqe, state, mk, mv, dec["dmat"], dec["qdec"], dec["gn"])


def _sample_pre_kernel(ws_ref, bs_ref, x_ref, cq_ref, sq_ref, ck_ref, sk_ref, kdec_ref, ng_ref, win_ref, bg_ref,
                       lng_ref, wr_ref, wsg_ref, wx_ref, wo_ref,
                       qe_ref, kre_ref, kde_ref, ve_ref, ge_ref, xqe_ref, osg_ref, vrows_ref, gates_ref,
                       wr_o_ref, wsg_o_ref, wx_o_ref, wo_o_ref, h_ref, vn_ref):
    n = x_ref.shape[0]
    nb = n // EXP
    ne = qe_ref.shape[0]
    i = pl.program_id(0)

    @pl.when(i == 0)
    def _():
        h_ref[...] = _rms(x_ref[...], ng_ref[...]).astype(BF16)
        vn_ref[...] = jnp.zeros_like(vn_ref)

    r = lax.broadcasted_iota(jnp.int32, (ne, n), 0)
    c = lax.broadcasted_iota(jnp.int32, (ne, n), 1)
    rep = jnp.where(c == (r & (EXP - 1)) * nb + i * (ne // (HEADS * EXP)) + (r >> 4), 1.0, 0.0).astype(BF16)
    he = jnp.dot(rep, h_ref[...], preferred_element_type=F32).astype(BF16)
    hh = _row_head(ne)
    h = h_ref[pl.ds(pl.multiple_of(i * nb, nb), nb), :]

    def proj(hm, a, b):
        return jnp.dot(hm, win_ref[:, a:b], preferred_element_type=F32)

    q, k = proj(he, C_Q, C_K), proj(he, C_K, C_V)
    xq = proj(he, C_XQ, C_GT)
    cq, sq, ck, sk, kdec = cq_ref[...], sq_ref[...], ck_ref[...], sk_ref[...], kdec_ref[...]
    for hd in range(HEADS):
        cols = slice(hd * DK, (hd + 1) * DK)
        own = hh == hd
        kr = _rope(k[:, cols], ck, sk)
        qe_ref[:, cols] = jnp.where(own, _rope(q[:, cols], cq, sq), 0.0).astype(BF16)
        kre_ref[:, cols] = jnp.where(own, kr, 0.0).astype(BF16)
        kde_ref[:, cols] = jnp.where(own, kr * kdec, 0.0).astype(BF16)
    v = proj(he, C_V, C_G)
    g = proj(he, C_G, C_SU)
    ve = jnp.zeros((ne, DV), F32)
    ge = jnp.zeros((ne, DV), F32)
    xqc = jnp.zeros((ne, DK), F32)
    for hd in range(HEADS):
        cols = slice(hd * DV, (hd + 1) * DV)
        own = hh == hd
        ve = jnp.where(own, v[:, cols], ve)
        ge = jnp.where(own, g[:, cols], ge)
        xqc = jnp.where(own, xq[:, hd * DK:(hd + 1) * DK], xqc)
    xqe_ref[...] = xqc.astype(BF16)
    ve_ref[...] = ve.astype(BF16)
    ge_ref[...] = _silu(ge)

    u = _gelu(proj(h, C_SU, C_SV))
    vn = _stdnorm(_gelu(proj(h, C_SV, C_XQ))) * lng_ref[...]
    vrows_ref[...] = vn
    vn_ref[i] = vn
    for g in range(HEADS):
        cols = slice(g * DK, (g + 1) * DK)
        mixed = jnp.full((nb, DK), bs_ref[g * EXP + i], F32)
        for s in range(EXP):
            w = jnp.where(s <= i, ws_ref[(g * EXP + i) * EXP + s], 0.0)
            mixed = mixed + w * vn_ref[s, :, cols]
        osg_ref[:, cols] = (u[:, cols] * mixed).astype(BF16)
    gates_ref[...] = _sigmoid(proj(h, C_GT, C_END) + bg_ref[...])
    _cast_blocks(((wr_ref, wr_o_ref), (wsg_ref, wsg_o_ref), (wx_ref, wx_o_ref), (wo_ref, wo_o_ref)))


def _sample_pre(x, tabs, kdec_rows, p, ws4, bs4, cast_weights):
    n = x.shape[0]
    cspec, cshape = _cast_specs(cast_weights, EXP, lambda i: i)
    t = n // EXP
    te = (t // EXP) * HEADS * EXP
    consts = [p["norm_mix_g"], p["w_in"], p["b_gate"], p["sg_ln_g"]]
    rowblk = lambda rows, w: pl.BlockSpec((rows, w), lambda i: (i, 0))
    outs = [(te, QK_W, BF16), (te, QK_W, BF16), (te, QK_W, BF16), (te, DV, BF16), (te, DV, F32),
            (te, DK, BF16), (t, SG_W, BF16), (t, SG_W, F32), (t, 3 * D_MODEL, F32)]
    return pl.pallas_call(
        _sample_pre_kernel,
        grid=(EXP,),
        in_specs=[_smem_spec(), _smem_spec(), _const_spec(x.shape)] + [_const_spec((te, DK))] * 5
        + [_const_spec(c.shape) for c in consts] + cspec,
        out_specs=[rowblk(r, w) for r, w, _ in outs] + cspec,
        out_shape=[jax.ShapeDtypeStruct((r * EXP, w), d) for r, w, d in outs] + cshape,
        scratch_shapes=[pltpu.VMEM((n, D_MODEL), BF16), pltpu.VMEM((EXP, t, SG_W), F32)],
        compiler_params=_params(("arbitrary",)),
        name="sample_pre",
    )(ws4, bs4, x, *tabs, kdec_rows, *consts, *cast_weights)


def _sample_merge_kernel(x_ref, orete_ref, oxe_ref, osg_ref, gates_ref, wr_ref, wsg_ref, wx_ref, wo_ref, xo_ref):
    n = x_ref.shape[0]
    ne = n * EXP
    r = lax.broadcasted_iota(jnp.int32, (n, ne), 0)
    c = lax.broadcasted_iota(jnp.int32, (n, ne), 1)
    col = jnp.where(r == (c & (EXP - 1)) * (n // EXP) + (c >> 4), 1.0, 0.0).astype(BF16)
    oret = jnp.dot(col, orete_ref[...], preferred_element_type=F32)
    ox = jnp.dot(col, oxe_ref[...], preferred_element_type=F32)
    gates = gates_ref[...]
    merged = (gates[:, :D_MODEL] * _mm(oret, wr_ref[...])
              + gates[:, D_MODEL:2 * D_MODEL] * jnp.dot(osg_ref[...], wsg_ref[...], preferred_element_type=F32)
              + gates[:, 2 * D_MODEL:] * _mm(ox, wx_ref[...]))
    xo_ref[...] = x_ref[...] + _mm(merged, wo_ref[...])


def _sample_merge(x, orete, oxe, osg, gates, p):
    ins = [x, orete, oxe, osg, gates, p["w_br_ret"], p["w_br_sg"], p["w_br_x"], p["w_o"]]
    return pl.pallas_call(
        _sample_merge_kernel,
        grid=(1,),
        in_specs=[_const_spec(a.shape) for a in ins],
        out_specs=pl.BlockSpec(x.shape, lambda i: (0, 0)),
        out_shape=jax.ShapeDtypeStruct(x.shape, F32),
        compiler_params=_params(("arbitrary",)),
        name="sample_merge",
    )(*ins)


def _sample_ffn_kernel(x_ref, ng_ref, wa_ref, wb_ref, cwa_ref, cwb_ref, cba_ref, cbb_ref, s0a_ref, s0b_ref,
                       s1a_ref, s1b_ref, wdn_ref, nf_ref,
                       y_ref, c2a_ref, c2b_ref, c3a_ref, c3b_ref, h_ref, acc_ref):
    nb = x_ref.shape[0] // EXP
    j = pl.program_id(0)

    @pl.when(j == 0)
    def _():
        h_ref[...] = _rms(x_ref[...], ng_ref[...]).astype(BF16)
        acc_ref[...] = jnp.zeros_like(acc_ref)

    h = h_ref[...]

    def conv(w_ref, cw_ref, cb_ref, s0_ref, s1_ref, c2_ref, c3_ref):
        z = jnp.dot(h, w_ref[...], preferred_element_type=F32)
        zp = [s0_ref[...], s1_ref[...]] + [z[l * nb:(l + 1) * nb] for l in range(EXP)]
        c2_ref[...] = zp[EXP]
        c3_ref[...] = zp[EXP + 1]
        cw, cb = cw_ref[...], cb_ref[...]
        return jnp.concatenate([cb + cw[0:1] * zp[l] + cw[1:2] * zp[l + 1] + cw[2:3] * zp[l + 2]
                                for l in range(EXP)], axis=0)

    a = conv(wa_ref, cwa_ref, cba_ref, s0a_ref, s1a_ref, c2a_ref, c3a_ref)
    b = conv(wb_ref, cwb_ref, cbb_ref, s0b_ref, s1b_ref, c2b_ref, c3b_ref)
    acc_ref[...] += _mm(_gelu(a) * b, wdn_ref[...])

    @pl.when(j == pl.num_programs(0) - 1)
    def _():
        y_ref[...] = _rms(x_ref[...] + acc_ref[...], nf_ref[...])


def _sample_ffn(x, sc, p):
    n = x.shape[0]
    nb = n // EXP
    cw = FFN_CW
    nch = D_FF // cw
    ca = lambda rows: pl.BlockSpec((rows, cw), lambda j: (0, j))
    cb = lambda rows: pl.BlockSpec((rows, cw), lambda j: (0, nch + j))
    sa = lambda k: pl.BlockSpec((None, nb, cw), lambda j: (k, 0, j))
    sb = lambda k: pl.BlockSpec((None, nb, cw), lambda j: (k, 0, nch + j))
    full = pl.BlockSpec((n, D_MODEL), lambda j: (0, 0))
    vec = pl.BlockSpec((1, D_MODEL), lambda j: (0, 0))
    return pl.pallas_call(
        _sample_ffn_kernel,
        grid=(nch,),
        in_specs=[full, vec, ca(D_MODEL), cb(D_MODEL), ca(3), cb(3), ca(1), cb(1), sa(0), sb(0), sa(1), sb(1),
                  pl.BlockSpec((cw, D_MODEL), lambda j: (j, 0)), vec],
        out_specs=[full] + [ca(nb)] * 4,
        out_shape=[jax.ShapeDtypeStruct((n, D_MODEL), F32)] + [jax.ShapeDtypeStruct((nb, D_FF), F32)] * 4,
        scratch_shapes=[pltpu.VMEM((n, D_MODEL), BF16), pltpu.VMEM((n, D_MODEL), F32)],
        compiler_params=_params(("arbitrary",)),
        name="sample_ffn",
    )(x, p["norm_ffn_g"], p["w_up"], p["w_up"], p["conv_w"], p["conv_w"], p["conv_b"], p["conv_b"],
      sc, sc, sc, sc, p["w_down"], p["norm_final_g"])


def _rope_tables(pos, scale):
    inv = ROPE_BASE ** (-np.arange(0, DK, 2, dtype=np.float64) / DK)
    ang = np.asarray(pos, np.float64)[:, None] * inv[None, :]
    cos, sin = np.cos(ang), np.sin(ang)
    return (np.concatenate([cos, cos], -1) * scale).astype(np.float32), \
        (np.concatenate([-sin, sin], -1) * scale).astype(np.float32)


def _decay(chunk):
    lg = np.log1p(-np.exp2(-5.0 - np.arange(HEADS, dtype=np.float64)))
    n = np.arange(chunk, dtype=np.float64)
    diff = n[:, None] - n[None, :]
    dmat = np.where(diff >= 0, np.exp(np.maximum(diff, 0.0)[None] * lg[:, None, None]), 0.0)
    qdec = np.exp((n + 1.0)[None, :] * lg[:, None])
    kdec = np.exp((chunk - 1.0 - n)[None, :] * lg[:, None])
    f32 = lambda a: a.astype(np.float32)
    return f32(dmat), f32(qdec), f32(kdec), f32(np.exp(chunk * lg))


def kernel(x_prompt, x_sample, mem_prompt, state_ret, state_conv, cache_mem_k, cache_mem_v, norm_mix_g, w_in,
           b_gate, ret_gn_g, sg_ln_g, sg_ws, sg_bs, mem_norm_g, w_mem_kv, w_br_ret, w_br_sg, w_br_x, w_o,
           norm_ffn_g, w_up, conv_w, conv_b, w_down, norm_final_g):
    bp, lp, _ = x_prompt.shape
    bs, ls, _ = x_sample.shape
    assert state_ret.shape[0] == 1 and ls == EXP and lp % PROMPT_TILE == 0
    assert bs % (bp * (lp // PROMPT_TILE)) == 0
    assert bs == CHUNK
    row = lambda a: a.reshape(1, -1)
    scale = DK ** -0.5

    mk, mv, mk_b, mv_b, w_in_b = _memkv(mem_prompt, row(mem_norm_g[0]), w_mem_kv[0], [w_in[0]])
    p = dict(norm_mix_g=row(norm_mix_g[0]), w_in=w_in_b, b_gate=row(b_gate[0]),
             gn_g=row(ret_gn_g[0]), sg_ln_g=row(sg_ln_g[0]),
             norm_ffn_g=row(norm_ffn_g[0]), conv_w=conv_w[0], conv_b=row(conv_b[0]),
             norm_final_g=row(norm_final_g))

    n = bs * ls
    per = HEADS * EXP
    te = (bs // EXP) * per
    pos_s = PAST_LEN + (np.arange(te) & (ls - 1))
    tabs_s = (*_rope_tables(pos_s, 1.0), *_rope_tables(pos_s, scale))
    dmat4, qdec4, kdec4, gc4 = _decay(ls)
    kdec_rows = np.ascontiguousarray(np.broadcast_to(np.tile(kdec4.reshape(per), te // per)[:, None], (te, DK)))
    xs = jnp.swapaxes(x_sample, 0, 1).reshape(n, D_MODEL)
    pre = _sample_pre(xs, tabs_s, kdec_rows, p, sg_ws[0][:, :ls, :ls].reshape(-1), sg_bs[0][:, :ls].reshape(-1),
                      [w_br_ret[0], w_br_sg[0], w_br_x[0], w_o[0]])
    qe, kre, kde, ve, ge, xqe, osg, vrows, gates, w_br_ret_b, w_br_sg_b, w_br_x_b, w_o_b = pre
    p = dict(p, w_br_ret=w_br_ret_b, w_br_sg=w_br_sg_b, w_br_x=w_br_x_b, w_o=w_o_b)

    pos_p = np.arange(lp)
    tabs_p = (*_rope_tables(pos_p, 1.0), *_rope_tables(pos_p, scale))
    dmat, qdec, kdec, gc = _decay(CHUNK)
    bcast = lambda a: np.ascontiguousarray(np.broadcast_to(a[:, :, None], (HEADS, CHUNK, DK)))
    dec_p = dict(dmat=dmat, qdec=bcast(qdec), kdec=bcast(kdec), gc=gc)
    pp = dict(p, sg_ws=sg_ws[0], sg_bias=jnp.repeat(sg_bs[0].T, DK, axis=1))
    x_mid, s_prompt, w_up_b, w_down_b = _prompt_mixer(x_prompt, tabs_p, mk_b, mv_b, dec_p, pp,
                                                      [w_up[0], w_down[0]])
    p = dict(p, w_up=w_up_b, w_down=w_down_b)

    sbb = bs // (bp * (lp // PROMPT_TILE))
    rb = sbb * per
    blk16 = np.einsum("hk,hls->hlks", np.eye(HEADS, dtype=np.float32), dmat4).reshape(per, per)
    dec_s = dict(gc=gc4,
                 dmat=np.kron(np.eye(sbb, dtype=np.float32), blk16),
                 qdec=np.ascontiguousarray(np.broadcast_to(np.tile(qdec4.reshape(per), sbb)[:, None], (rb, DV))),
                 gn=jnp.tile(jnp.repeat(ret_gn_g[0], ls, axis=0), (sbb, 1)))
    y_prompt, tail, orete, oxe, s_sample = _prompt_ffn(
        x_mid, p, (qe, kre, kde, ve, ge, xqe), state_ret[0].reshape(bs, HEADS * DK, DV),
        cache_mem_k[0].reshape(bs, MEM_LEN * HEADS, DK), cache_mem_v[0].reshape(bs, MEM_LEN * HEADS, DK), dec_s)
    xs_mid = _sample_merge(xs, orete, oxe, osg, gates, p)
    y_sample, c2a, c2b, c3a, c3b = _sample_ffn(xs_mid, jnp.swapaxes(state_conv[0], 0, 1), p)
    conv_s = jnp.stack([jnp.concatenate([c2a, c2b], -1), jnp.concatenate([c3a, c3b], -1)], axis=1)
    unpos = lambda a: jnp.swapaxes(a.reshape(ls, bs, a.shape[-1]), 0, 1)

    return (y_prompt, unpos(y_sample),
            s_prompt[None], tail[None],
            mk.reshape(1, bp, MEM_LEN, HEADS, DK), mv.reshape(1, bp, MEM_LEN, HEADS, DK),
            s_sample.reshape(1, bs, HEADS, DK, DV), conv_s[None],
            unpos(vrows)[None])
```

```python
import numpy as np
import jax
import jax.numpy as jnp
from jax import lax
from jax.experimental import pallas as pl
from jax.experimental.pallas import tpu as pltpu

F32 = jnp.float32
BF16 = jnp.bfloat16

D_MODEL = 1024
HEADS = 4
DK = 128
DV = 256
QK_W = HEADS * DK
V_W = HEADS * DV
SG_W = 512
X_W = 512
MEM_LEN = 256
D_FF = 2816
CHUNK = 128
ROPE_BASE = 10000.0
EPS = 1e-6
PAST_LEN = 16384

C_Q, C_K, C_V, C_G, C_SU, C_SV, C_XQ, C_GT, C_END = 0, 512, 1024, 2048, 3072, 3584, 4096, 4608, 7680

PROMPT_TILE = 512
EXP = HEADS
FFN_CW = 1408
VMEM_BYTES_V7X = 64 * 1024 * 1024
VMEM_LIMIT = VMEM_BYTES_V7X - 3 * 1024 * 1024


def _rms(x, g):
    return x * lax.rsqrt(jnp.mean(x * x, axis=-1, keepdims=True) + EPS) * g


def _stdnorm(x):
    mu = jnp.mean(x, axis=-1, keepdims=True)
    xc = x - mu
    var = jnp.mean(xc * xc, axis=-1, keepdims=True)
    return xc * lax.rsqrt(var + EPS)


_GELU_C0 = np.float32(np.sqrt(2.0 / np.pi))
_GELU_C1 = np.float32(np.sqrt(2.0 / np.pi) * 0.044715)


def _gelu_tanh(x):
    return jnp.tanh(x * (_GELU_C0 + _GELU_C1 * (x * x)))


def _gelu(x):
    hx = 0.5 * x
    return hx + hx * _gelu_tanh(x)


def _sigmoid(x):
    return 0.5 + 0.5 * jnp.tanh(0.5 * x)


def _silu(x):
    hx = 0.5 * x
    return hx + hx * jnp.tanh(hx)


def _softmax(s):
    e = jnp.exp(s - jnp.max(s, axis=-1, keepdims=True))
    return e * (1.0 / jnp.sum(e, axis=-1, keepdims=True))


def _mm(a, b):
    return jnp.dot(a.astype(BF16), b.astype(BF16), preferred_element_type=F32)


def _mm_nt(a, b):
    return lax.dot_general(a.astype(BF16), b.astype(BF16), (((1,), (1,)), ((), ())),
                           preferred_element_type=F32)


def _mm_tn(a, b):
    return lax.dot_general(a.astype(BF16), b.astype(BF16), (((0,), (0,)), ((), ())),
                           preferred_element_type=F32)


def _rope(x, cos, sin):
    return x * cos + pltpu.roll(x, DK // 2, 1) * sin


def _tril(w):
    r = lax.broadcasted_iota(jnp.int32, w.shape, 0)
    c = lax.broadcasted_iota(jnp.int32, w.shape, 1)
    return jnp.where(r >= c, w, 0.0)


def _conv_gate(zs_ref, n, cols_a, cols_b, cw_a, cb_a, cw_b, cb_b):
    def conv(cols, cw, cb):
        zz = zs_ref[0:8 + n, cols]
        blocks = zz.reshape(n // 8 + 1, 8, zz.shape[-1])
        sub = lax.broadcasted_iota(jnp.int32, (1, 8, 1), 1)
        r1 = pltpu.roll(blocks, 1, 1)
        r2 = pltpu.roll(r1, 1, 1)
        z1 = jnp.where(sub < 1, r1[:-1], r1[1:]).reshape(n, zz.shape[-1])
        z2 = jnp.where(sub < 2, r2[:-1], r2[1:]).reshape(n, zz.shape[-1])
        return cb + cw[0:1] * z2 + cw[1:2] * z1 + cw[2:3] * zz[8:]
    a = conv(cols_a, cw_a, cb_a)
    return (a + a * _gelu_tanh(a)) * conv(cols_b, 0.5 * cw_b, 0.5 * cb_b)


def _const_spec(shape):
    nd = len(shape)
    return pl.BlockSpec(shape, lambda *_: (0,) * nd, pipeline_mode=pl.Buffered(1))


def _smem_spec():
    return pl.BlockSpec(memory_space=pltpu.SMEM)


def _params(sem):
    return pltpu.CompilerParams(dimension_semantics=sem, vmem_limit_bytes=VMEM_LIMIT)


def _cast_blocks(refs):
    for src, dst in refs:
        dst[...] = src[...].astype(BF16)


def _cast_specs(weights, steps, flat_step):
    specs, shapes = [], []
    for w in weights:
        rows, cols = w.shape
        nblk = max(n for n in range(1, steps + 1) if steps % n == 0 and rows % (16 * n) == 0)
        specs.append(pl.BlockSpec((rows // nblk, cols), lambda *g, r=steps // nblk: (flat_step(*g) // r, 0)))
        shapes.append(jax.ShapeDtypeStruct(w.shape, BF16))
    return specs, shapes


def _memkv_kernel(mem_ref, g_ref, w_ref, *refs):
    ncast = (len(refs) - 4) // 2
    casts, (k_ref, v_ref, kb_ref, vb_ref) = refs[:ncast], refs[ncast:ncast + 4]
    kv = _mm(_rms(mem_ref[...], g_ref[...]), w_ref[...])
    k, v = kv[:, :X_W], kv[:, X_W:]
    for hd in range(HEADS):
        cols = slice(hd * DK, (hd + 1) * DK)
        k_ref[pl.ds(hd, MEM_LEN, stride=HEADS), :] = k[:, cols]
        v_ref[pl.ds(hd, MEM_LEN, stride=HEADS), :] = v[:, cols]
    kb_ref[...] = k.astype(BF16)
    vb_ref[...] = v.astype(BF16)
    _cast_blocks(zip(casts, refs[ncast + 4:]))


def _memkv(mem, g, w, cast_weights):
    b = mem.shape[0]
    blk = pl.BlockSpec((None, MEM_LEN, X_W), lambda i: (i, 0, 0))
    flat = pl.BlockSpec((None, MEM_LEN * HEADS, DK), lambda i: (i, 0, 0))
    cspec, cshape = _cast_specs(cast_weights, b, lambda i: i)
    return pl.pallas_call(
        _memkv_kernel,
        grid=(b,),
        in_specs=[pl.BlockSpec((None, MEM_LEN, D_MODEL), lambda i: (i, 0, 0)),
                  _const_spec((1, D_MODEL)), _const_spec((D_MODEL, 2 * X_W))] + cspec,
        out_specs=[flat, flat, blk, blk] + cspec,
        out_shape=[jax.ShapeDtypeStruct((b, MEM_LEN * HEADS, DK), F32)] * 2
        + [jax.ShapeDtypeStruct((b, MEM_LEN, X_W), BF16)] * 2 + cshape,
        compiler_params=_params(("arbitrary",)),
        name="mem_kv",
    )(mem, g, w, *cast_weights)


def _mixer_kernel(gc_ref, x_ref, tab_ref, mk_ref, mv_ref, ng_ref, win_ref, bg_ref,
                  gn_ref, lng_ref, ws_ref, sgb_ref, dmat_ref, qdec_ref, kdec_ref,
                  wr_ref, wsg_ref, wx_ref, wo_ref, wup_ref, wdn_ref,
                  xo_ref, s_ref, wup_o_ref, wdn_o_ref, oret_ref, osg_ref, ox_ref):
    tile = x_ref.shape[0]

    @pl.when(pl.program_id(1) == 0)
    def _():
        s_ref[...] = jnp.zeros_like(s_ref)

    x = x_ref[...]
    h = _rms(x, ng_ref[...]).astype(BF16)

    def proj(a, b):
        return jnp.dot(h, win_ref[:, a:b], preferred_element_type=F32)

    heads, chunks = range(HEADS), range(tile // CHUNK)
    rows = [slice(c * CHUNK, (c + 1) * CHUNK) for c in chunks]
    kcols = [slice(hd * DK, (hd + 1) * DK) for hd in heads]
    vcols = [slice(hd * DV, (hd + 1) * DV) for hd in heads]

    suv = proj(C_SU, C_XQ)
    su, sv = suv[:, :SG_W], suv[:, SG_W:]
    qk = proj(C_Q, C_V)
    q, k = qk[:, :QK_W], qk[:, QK_W:]
    u = _gelu(su)
    vn = (_stdnorm(_gelu(sv)) * lng_ref[...]).astype(BF16)
    wsg = [_tril(ws_ref[g]).astype(BF16) for g in heads]
    vb = proj(C_V, C_G).astype(BF16)
    mixed = [[jnp.dot(wsg[g], vn[rows[c], kcols[g]], preferred_element_type=F32) + sgb_ref[:, kcols[g]]
              for c in chunks] for g in heads]
    gsil = _silu(proj(C_G, C_SU))
    for g in heads:
        osg_ref[:, kcols[g]] = (u[:, kcols[g]] * jnp.concatenate(mixed[g], axis=0)).astype(BF16)

    cq, sq, ck, sk = (tab_ref[:, i * DK:(i + 1) * DK] for i in range(4))
    qr = [_rope(q[:, kcols[hd]], cq, sq) for hd in heads]
    kr = [_rope(k[:, kcols[hd]], ck, sk) for hd in heads]
    sc = [[_mm_nt(qr[hd][rows[c]], kr[hd][rows[c]]) for c in chunks] for hd in heads]
    upd = [[_mm_tn(kr[hd][rows[c]] * kdec_ref[hd], vb[rows[c], vcols[hd]]) for c in chunks] for hd in heads]
    xq = proj(C_XQ, C_GT)
    gmid = C_GT + (C_END - C_GT) // 2
    gt0 = proj(C_GT, gmid)
    states = []
    for hd in heads:
        st, before = s_ref[hd], []
        for c in chunks:
            before.append(st.astype(BF16))
            st = gc_ref[hd] * st + upd[hd][c]
        s_ref[hd] = st
        states.append(before)
    o = [[jnp.dot(jnp.concatenate([(sc[hd][c] * dmat_ref[hd]).astype(BF16),
                                   (qr[hd][rows[c]] * qdec_ref[hd]).astype(BF16)], axis=1),
                  jnp.concatenate([vb[rows[c], vcols[hd]], states[hd][c]], axis=0),
                  preferred_element_type=F32)
          for c in chunks] for hd in heads]
    sx = [_mm_nt(xq[:, kcols[hd]], mk_ref[:, kcols[hd]]) * np.float32(DK ** -0.5) for hd in heads]
    for hd in heads:
        on = _stdnorm(jnp.concatenate(o[hd], axis=0)) * gn_ref[:, vcols[hd]]
        oret_ref[:, vcols[hd]] = (gsil[:, vcols[hd]] * on).astype(BF16)
    gt1 = proj(gmid, C_END)
    px = [_softmax(sx[hd]) for hd in heads]
    for hd in heads:
        ox_ref[:, kcols[hd]] = _mm(px[hd], mv_ref[:, kcols[hd]]).astype(BF16)

    gates = _sigmoid(jnp.concatenate([gt0, gt1], axis=1) + bg_ref[...])
    merged = (gates[:, :D_MODEL] * jnp.dot(oret_ref[...], wr_ref[...], preferred_element_type=F32)
              + gates[:, D_MODEL:2 * D_MODEL] * jnp.dot(osg_ref[...], wsg_ref[...], preferred_element_type=F32)
              + gates[:, 2 * D_MODEL:] * jnp.dot(ox_ref[...], wx_ref[...], preferred_element_type=F32))
    xo_ref[...] = x + _mm(merged, wo_ref[...])
    _cast_blocks(((wup_ref, wup_o_ref), (wdn_ref, wdn_o_ref)))


def _prompt_mixer(x, tabs, mk_b, mv_b, dec, p, cast_weights):
    b, l, _ = x.shape
    t = PROMPT_TILE
    nt = l // t
    cspec, cshape = _cast_specs(cast_weights, b * nt, lambda i, j: i * nt + j)
    tok = lambda w: pl.BlockSpec((None, t, w), lambda i, j: (i, j, 0))
    tab = pl.BlockSpec((t, 4 * DK), lambda i, j: (j, 0))
    mem = pl.BlockSpec((None, MEM_LEN, X_W), lambda i, j: (i, 0, 0))
    consts = [p["norm_mix_g"], p["w_in"], p["b_gate"], p["gn_g"], p["sg_ln_g"], p["sg_ws"], p["sg_bias"],
              dec["dmat"], dec["qdec"], dec["kdec"], p["w_br_ret"], p["w_br_sg"], p["w_br_x"], p["w_o"]]
    return pl.pallas_call(
        _mixer_kernel,
        grid=(b, l // t),
        in_specs=[_smem_spec(), tok(D_MODEL), tab, mem, mem]
        + [_const_spec(c.shape) for c in consts] + cspec,
        out_specs=[tok(D_MODEL), pl.BlockSpec((None, HEADS, DK, DV), lambda i, j: (i, 0, 0, 0))] + cspec,
        out_shape=[jax.ShapeDtypeStruct((b, l, D_MODEL), F32),
                   jax.ShapeDtypeStruct((b, HEADS, DK, DV), F32)] + cshape,
        scratch_shapes=[pltpu.VMEM((t, V_W), BF16), pltpu.VMEM((t, SG_W), BF16), pltpu.VMEM((t, X_W), BF16)],
        compiler_params=_params(("arbitrary", "arbitrary")),
        name="prompt_mixer",
    )(dec["gc"], x, np.concatenate(tabs, axis=1), mk_b, mv_b, *consts, *cast_weights)


def _row_head(n):
    return (lax.broadcasted_iota(jnp.int32, (n, 1), 0) >> 2) & (HEADS - 1)


def _state_matmuls(gc_ref, qe_ref, kre_ref, kde_ref, ve_ref, xqe_ref, s_ref, mk_ref, dmat_ref, so_ref):
    rows_n = qe_ref.shape[0]
    per = HEADS * EXP
    batches = range(rows_n // per)
    rows = [slice(b * per, (b + 1) * per) for b in batches]
    q, ve, xq = qe_ref[...], ve_ref[...], xqe_ref[...]
    sc = _mm_nt(q, kre_ref[...]) * dmat_ref[...]
    kd = kde_ref[...]
    rowb = lax.broadcasted_iota(jnp.int32, (rows_n, 1), 0) >> 4
    cross = jnp.concatenate(
        [jnp.dot(q[rows[b]], s_ref[b].astype(BF16), preferred_element_type=F32) for b in batches], axis=0)
    upd = [_mm_tn(kd, jnp.where(rowb == b, ve, jnp.zeros_like(ve))) for b in batches]
    for b in batches:
        for hd in range(HEADS):
            hr = slice(hd * DK, (hd + 1) * DK)
            so_ref[b, hr, :] = gc_ref[hd] * s_ref[b, hr, :] + upd[b][hr]
    sx = jnp.concatenate([_mm_nt(xq[rows[b]], mk_ref[b]) for b in batches], axis=0) * np.float32(DK ** -0.5)
    return sc, cross, sx


def _state_outputs(sc, cross, sx, ve_ref, ge_ref, mv_ref, qdec_ref, gn_ref, oret_ref, ox_ref):
    rows_n = ve_ref.shape[0]
    per = HEADS * EXP
    batches = range(rows_n // per)
    rows = [slice(b * per, (b + 1) * per) for b in batches]
    inner = jnp.dot(sc.astype(BF16), ve_ref[...], preferred_element_type=F32)
    hh = _row_head(rows_n)
    own_col = (lax.broadcasted_iota(jnp.int32, (rows_n, MEM_LEN * HEADS), 1) & (HEADS - 1)) == hh
    p = _softmax(jnp.where(own_col, sx, np.float32(-1e30))).astype(BF16)
    oxs = jnp.concatenate([_mm(p[rows[b]], mv_ref[b]) for b in batches], axis=0)
    o = inner + cross * qdec_ref[...]
    og = ge_ref[...] * (_stdnorm(o) * gn_ref[...])
    for hd in range(HEADS):
        oret_ref[:, hd * DV:(hd + 1) * DV] = jnp.where(hh == hd, og, 0.0).astype(BF16)
        ox_ref[:, hd * DK:(hd + 1) * DK] = jnp.where(hh == hd, oxs, 0.0).astype(BF16)


def _ffn_kernel(gc_ref, x_ref, ng_ref, wup_ref, cw_ref, cb_ref, wdn_ref, nf_ref,
                qe_ref, kre_ref, kde_ref, ve_ref, ge_ref, xqe_ref, s_ref, mk_ref, mv_ref, dmat_ref, qdec_ref, gn_ref,
                y_ref, tail_ref, oret_ref, ox_ref, so_ref, zs_ref):
    tile = x_ref.shape[0]
    first = pl.program_id(1) == 0

    @pl.when(first)
    def _():
        zs_ref[0:8, :] = jnp.zeros((8, 2 * D_FF), F32)

    @pl.when(jnp.logical_not(first))
    def _():
        zs_ref[0:8, :] = zs_ref[tile:tile + 8, :]

    sc, cross, sx = _state_matmuls(gc_ref, qe_ref, kre_ref, kde_ref, ve_ref, xqe_ref, s_ref, mk_ref, dmat_ref,
                                   so_ref)
    x = x_ref[...]
    zs_ref[8:8 + tile, :] = _mm(_rms(x, ng_ref[...]), wup_ref[...])
    _state_outputs(sc, cross, sx, ve_ref, ge_ref, mv_ref, qdec_ref, gn_ref, oret_ref, ox_ref)
    ca, cb = slice(0, D_FF), slice(D_FF, 2 * D_FF)
    gate = _conv_gate(zs_ref, tile, ca, cb, cw_ref[:, ca], cb_ref[:, ca], cw_ref[:, cb], cb_ref[:, cb])
    y = x + _mm(gate, wdn_ref[...])
    y_ref[...] = _rms(y, nf_ref[...])
    tail_ref[...] = zs_ref[tile + 6:tile + 8, :]


def _prompt_ffn(x, p, pre, state, mk, mv, dec):
    b, l, _ = x.shape
    t = PROMPT_TILE
    nt = l // t
    qe, kre, kde, ve, ge, xqe = pre
    nb = state.shape[0]
    bb = nb // (b * nt)
    rb = bb * HEADS * EXP
    ne = qe.shape[0]
    tok = pl.BlockSpec((None, t, D_MODEL), lambda i, j: (i, j, 0))
    rowblk = lambda w: pl.BlockSpec((rb, w), lambda i, j: (i * nt + j, 0))
    batblk = lambda a, c: pl.BlockSpec((bb, a, c), lambda i, j: (i * nt + j, 0, 0))
    consts = [p["norm_ffn_g"], p["w_up"], p["conv_w"], p["conv_b"], p["w_down"], p["norm_final_g"]]
    return pl.pallas_call(
        _ffn_kernel,
        grid=(b, nt),
        in_specs=[_smem_spec(), tok] + [_const_spec(c.shape) for c in consts]
        + [rowblk(QK_W), rowblk(QK_W), rowblk(QK_W), rowblk(DV), rowblk(DV), rowblk(DK),
           batblk(HEADS * DK, DV), batblk(MEM_LEN * HEADS, DK), batblk(MEM_LEN * HEADS, DK),
           _const_spec((rb, rb)), _const_spec((rb, DV)), _const_spec((rb, DV))],
        out_specs=[tok, pl.BlockSpec((None, 2, 2 * D_FF), lambda i, j: (i, 0, 0)),
                   rowblk(V_W), rowblk(X_W), batblk(HEADS * DK, DV)],
        out_shape=[jax.ShapeDtypeStruct((b, l, D_MODEL), F32), jax.ShapeDtypeStruct((b, 2, 2 * D_FF), F32),
                   jax.ShapeDtypeStruct((ne, V_W), BF16), jax.ShapeDtypeStruct((ne, X_W), BF16),
                   jax.ShapeDtypeStruct(state.shape, F32)],
        scratch_shapes=[pltpu.VMEM((t + 8, 2 * D_FF), F32)],
        compiler_params=_params(("arbitrary", "arbitrary")),
        name="prompt_ffn",
    )(dec["gc"], x, *consts, qe, kre, kde, ve, ge, xqe, state, mk, mv, dec["dmat"], dec["qdec"], dec["gn"])


def _sample_pre_kernel(ws_ref, bs_ref, x_ref, cq_ref, sq_ref, ck_ref, sk_ref, kdec_ref, ng_ref, win_ref, bg_ref,
                       lng_ref, wr_ref, wsg_ref, wx_ref, wo_ref,
                       qe_ref, kre_ref, kde_ref, ve_ref, ge_ref, xqe_ref, osg_ref, vrows_ref, gates_ref,
                       wr_o_ref, wsg_o_ref, wx_o_ref, wo_o_ref, h_ref, vn_ref):
    n = x_ref.shape[0]
    nb = n // EXP
    ne = qe_ref.shape[0]
    i = pl.program_id(0)

    @pl.when(i == 0)
    def _():
        h_ref[...] = _rms(x_ref[...], ng_ref[...]).astype(BF16)
        vn_ref[...] = jnp.zeros_like(vn_ref)

    r = lax.broadcasted_iota(jnp.int32, (ne, n), 0)
    c = lax.broadcasted_iota(jnp.int32, (ne, n), 1)
    rep = jnp.where(c == (r & (EXP - 1)) * nb + i * (ne // (HEADS * EXP)) + (r >> 4), 1.0, 0.0).astype(BF16)
    he = jnp.dot(rep, h_ref[...], preferred_element_type=F32).astype(BF16)
    hh = _row_head(ne)
    h = h_ref[pl.ds(pl.multiple_of(i * nb, nb), nb), :]

    def proj(hm, a, b):
        return jnp.dot(hm, win_ref[:, a:b], preferred_element_type=F32)

    q, k = proj(he, C_Q, C_K), proj(he, C_K, C_V)
    xq = proj(he, C_XQ, C_GT)
    cq, sq, ck, sk, kdec = cq_ref[...], sq_ref[...], ck_ref[...], sk_ref[...], kdec_ref[...]
    for hd in range(HEADS):
        cols = slice(hd * DK, (hd + 1) * DK)
        own = hh == hd
        kr = _rope(k[:, cols], ck, sk)
        qe_ref[:, cols] = jnp.where(own, _rope(q[:, cols], cq, sq), 0.0).astype(BF16)
        kre_ref[:, cols] = jnp.where(own, kr, 0.0).astype(BF16)
        kde_ref[:, cols] = jnp.where(own, kr * kdec, 0.0).astype(BF16)
    v = proj(he, C_V, C_G)
    g = proj(he, C_G, C_SU)
    ve = jnp.zeros((ne, DV), F32)
    ge = jnp.zeros((ne, DV), F32)
    xqc = jnp.zeros((ne, DK), F32)
    for hd in range(HEADS):
        cols = slice(hd * DV, (hd + 1) * DV)
        own = hh == hd
        ve = jnp.where(own, v[:, cols], ve)
        ge = jnp.where(own, g[:, cols], ge)
        xqc = jnp.where(own, xq[:, hd * DK:(hd + 1) * DK], xqc)
    xqe_ref[...] = xqc.astype(BF16)
    ve_ref[...] = ve.astype(BF16)
    ge_ref[...] = _silu(ge)

    u = _gelu(proj(h, C_SU, C_SV))
    vn = _stdnorm(_gelu(proj(h, C_SV, C_XQ))) * lng_ref[...]
    vrows_ref[...] = vn
    vn_ref[i] = vn
    for g in range(HEADS):
        cols = slice(g * DK, (g + 1) * DK)
        mixed = jnp.full((nb, DK), bs_ref[g * EXP + i], F32)
        for s in range(EXP):
            w = jnp.where(s <= i, ws_ref[(g * EXP + i) * EXP + s], 0.0)
            mixed = mixed + w * vn_ref[s, :, cols]
        osg_ref[:, cols] = (u[:, cols] * mixed).astype(BF16)
    gates_ref[...] = _sigmoid(proj(h, C_GT, C_END) + bg_ref[...])
    _cast_blocks(((wr_ref, wr_o_ref), (wsg_ref, wsg_o_ref), (wx_ref, wx_o_ref), (wo_ref, wo_o_ref)))


def _sample_pre(x, tabs, kdec_rows, p, ws4, bs4, cast_weights):
    n = x.shape[0]
    cspec, cshape = _cast_specs(cast_weights, EXP, lambda i: i)
    t = n // EXP
    te = (t // EXP) * HEADS * EXP
    consts = [p["norm_mix_g"], p["w_in"], p["b_gate"], p["sg_ln_g"]]
    rowblk = lambda rows, w: pl.BlockSpec((rows, w), lambda i: (i, 0))
    outs = [(te, QK_W, BF16), (te, QK_W, BF16), (te, QK_W, BF16), (te, DV, BF16), (te, DV, F32),
            (te, DK, BF16), (t, SG_W, BF16), (t, SG_W, F32), (t, 3 * D_MODEL, F32)]
    return pl.pallas_call(
        _sample_pre_kernel,
        grid=(EXP,),
        in_specs=[_smem_spec(), _smem_spec(), _const_spec(x.shape)] + [_const_spec((te, DK))] * 5
        + [_const_spec(c.shape) for c in consts] + cspec,
        out_specs=[rowblk(r, w) for r, w, _ in outs] + cspec,
        out_shape=[jax.ShapeDtypeStruct((r * EXP, w), d) for r, w, d in outs] + cshape,
        scratch_shapes=[pltpu.VMEM((n, D_MODEL), BF16), pltpu.VMEM((EXP, t, SG_W), F32)],
        compiler_params=_params(("arbitrary",)),
        name="sample_pre",
    )(ws4, bs4, x, *tabs, kdec_rows, *consts, *cast_weights)


def _sample_merge_kernel(x_ref, orete_ref, oxe_ref, osg_ref, gates_ref, wr_ref, wsg_ref, wx_ref, wo_ref, xo_ref):
    n = x_ref.shape[0]
    ne = n * EXP
    r = lax.broadcasted_iota(jnp.int32, (n, ne), 0)
    c = lax.broadcasted_iota(jnp.int32, (n, ne), 1)
    col = jnp.where(r == (c & (EXP - 1)) * (n // EXP) + (c >> 4), 1.0, 0.0).astype(BF16)
    oret = jnp.dot(col, orete_ref[...], preferred_element_type=F32)
    ox = jnp.dot(col, oxe_ref[...], preferred_element_type=F32)
    gates = gates_ref[...]
    merged = (gates[:, :D_MODEL] * _mm(oret, wr_ref[...])
              + gates[:, D_MODEL:2 * D_MODEL] * jnp.dot(osg_ref[...], wsg_ref[...], preferred_element_type=F32)
              + gates[:, 2 * D_MODEL:] * _mm(ox, wx_ref[...]))
    xo_ref[...] = x_ref[...] + _mm(merged, wo_ref[...])


def _sample_merge(x, orete, oxe, osg, gates, p):
    ins = [x, orete, oxe, osg, gates, p["w_br_ret"], p["w_br_sg"], p["w_br_x"], p["w_o"]]
    return pl.pallas_call(
        _sample_merge_kernel,
        grid=(1,),
        in_specs=[_const_spec(a.shape) for a in ins],
        out_specs=pl.BlockSpec(x.shape, lambda i: (0, 0)),
        out_shape=jax.ShapeDtypeStruct(x.shape, F32),
        compiler_params=_params(("arbitrary",)),
        name="sample_merge",
    )(*ins)


def _sample_ffn_kernel(x_ref, ng_ref, wa_ref, wb_ref, cwa_ref, cwb_ref, cba_ref, cbb_ref, s0a_ref, s0b_ref,
                       s1a_ref, s1b_ref, wdn_ref, nf_ref,
                       y_ref, c2a_ref, c2b_ref, c3a_ref, c3b_ref, h_ref, acc_ref):
    nb = x_ref.shape[0] // EXP
    j = pl.program_id(0)

    @pl.when(j == 0)
    def _():
        h_ref[...] = _rms(x_ref[...], ng_ref[...]).astype(BF16)
        acc_ref[...] = jnp.zeros_like(acc_ref)

    h = h_ref[...]

    def conv(w_ref, cw_ref, cb_ref, s0_ref, s1_ref, c2_ref, c3_ref):
        z = jnp.dot(h, w_ref[...], preferred_element_type=F32)
        zp = [s0_ref[...], s1_ref[...]] + [z[l * nb:(l + 1) * nb] for l in range(EXP)]
        c2_ref[...] = zp[EXP]
        c3_ref[...] = zp[EXP + 1]
        cw, cb = cw_ref[...], cb_ref[...]
        return jnp.concatenate([cb + cw[0:1] * zp[l] + cw[1:2] * zp[l + 1] + cw[2:3] * zp[l + 2]
                                for l in range(EXP)], axis=0)

    a = conv(wa_ref, cwa_ref, cba_ref, s0a_ref, s1a_ref, c2a_ref, c3a_ref)
    b = conv(wb_ref, cwb_ref, cbb_ref, s0b_ref, s1b_ref, c2b_ref, c3b_ref)
    acc_ref[...] += _mm(_gelu(a) * b, wdn_ref[...])

    @pl.when(j == pl.num_programs(0) - 1)
    def _():
        y_ref[...] = _rms(x_ref[...] + acc_ref[...], nf_ref[...])


def _sample_ffn(x, sc, p):
    n = x.shape[0]
    nb = n // EXP
    cw = FFN_CW
    nch = D_FF // cw
    ca = lambda rows: pl.BlockSpec((rows, cw), lambda j: (0, j))
    cb = lambda rows: pl.BlockSpec((rows, cw), lambda j: (0, nch + j))
    sa = lambda k: pl.BlockSpec((None, nb, cw), lambda j: (k, 0, j))
    sb = lambda k: pl.BlockSpec((None, nb, cw), lambda j: (k, 0, nch + j))
    full = pl.BlockSpec((n, D_MODEL), lambda j: (0, 0))
    vec = pl.BlockSpec((1, D_MODEL), lambda j: (0, 0))
    return pl.pallas_call(
        _sample_ffn_kernel,
        grid=(nch,),
        in_specs=[full, vec, ca(D_MODEL), cb(D_MODEL), ca(3), cb(3), ca(1), cb(1), sa(0), sb(0), sa(1), sb(1),
                  pl.BlockSpec((cw, D_MODEL), lambda j: (j, 0)), vec],
        out_specs=[full] + [ca(nb)] * 4,
        out_shape=[jax.ShapeDtypeStruct((n, D_MODEL), F32)] + [jax.ShapeDtypeStruct((nb, D_FF), F32)] * 4,
        scratch_shapes=[pltpu.VMEM((n, D_MODEL), BF16), pltpu.VMEM((n, D_MODEL), F32)],
        compiler_params=_params(("arbitrary",)),
        name="sample_ffn",
    )(x, p["norm_ffn_g"], p["w_up"], p["w_up"], p["conv_w"], p["conv_w"], p["conv_b"], p["conv_b"],
      sc, sc, sc, sc, p["w_down"], p["norm_final_g"])


def _rope_tables(pos, scale):
    inv = ROPE_BASE ** (-np.arange(0, DK, 2, dtype=np.float64) / DK)
    ang = np.asarray(pos, np.float64)[:, None] * inv[None, :]
    cos, sin = np.cos(ang), np.sin(ang)
    return (np.concatenate([cos, cos], -1) * scale).astype(np.float32), \
        (np.concatenate([-sin, sin], -1) * scale).astype(np.float32)


def _decay(chunk):
    lg = np.log1p(-np.exp2(-5.0 - np.arange(HEADS, dtype=np.float64)))
    n = np.arange(chunk, dtype=np.float64)
    diff = n[:, None] - n[None, :]
    dmat = np.where(diff >= 0, np.exp(np.maximum(diff, 0.0)[None] * lg[:, None, None]), 0.0)
    qdec = np.exp((n + 1.0)[None, :] * lg[:, None])
    kdec = np.exp((chunk - 1.0 - n)[None, :] * lg[:, None])
    f32 = lambda a: a.astype(np.float32)
    return f32(dmat), f32(qdec), f32(kdec), f32(np.exp(chunk * lg))


def kernel(x_prompt, x_sample, mem_prompt, state_ret, state_conv, cache_mem_k, cache_mem_v, norm_mix_g, w_in,
           b_gate, ret_gn_g, sg_ln_g, sg_ws, sg_bs, mem_norm_g, w_mem_kv, w_br_ret, w_br_sg, w_br_x, w_o,
           norm_ffn_g, w_up, conv_w, conv_b, w_down, norm_final_g):
    bp, lp, _ = x_prompt.shape
    bs, ls, _ = x_sample.shape
    assert state_ret.shape[0] == 1 and ls == EXP and lp % PROMPT_TILE == 0
    assert bs % (bp * (lp // PROMPT_TILE)) == 0
    assert bs == CHUNK
    row = lambda a: a.reshape(1, -1)
    scale = DK ** -0.5

    mk, mv, mk_b, mv_b, w_in_b = _memkv(mem_prompt, row(mem_norm_g[0]), w_mem_kv[0], [w_in[0]])
    p = dict(norm_mix_g=row(norm_mix_g[0]), w_in=w_in_b, b_gate=row(b_gate[0]),
             gn_g=row(ret_gn_g[0]), sg_ln_g=row(sg_ln_g[0]),
             norm_ffn_g=row(norm_ffn_g[0]), conv_w=conv_w[0], conv_b=row(conv_b[0]),
             norm_final_g=row(norm_final_g))

    n = bs * ls
    per = HEADS * EXP
    te = (bs // EXP) * per
    pos_s = PAST_LEN + (np.arange(te) & (ls - 1))
    tabs_s = (*_rope_tables(pos_s, 1.0), *_rope_tables(pos_s, scale))
    dmat4, qdec4, kdec4, gc4 = _decay(ls)
    kdec_rows = np.ascontiguousarray(np.broadcast_to(np.tile(kdec4.reshape(per), te // per)[:, None], (te, DK)))
    xs = jnp.swapaxes(x_sample, 0, 1).reshape(n, D_MODEL)
    pre = _sample_pre(xs, tabs_s, kdec_rows, p, sg_ws[0][:, :ls, :ls].reshape(-1), sg_bs[0][:, :ls].reshape(-1),
                      [w_br_ret[0], w_br_sg[0], w_br_x[0], w_o[0]])
    qe, kre, kde, ve, ge, xqe, osg, vrows, gates, w_br_ret_b, w_br_sg_b, w_br_x_b, w_o_b = pre
    p = dict(p, w_br_ret=w_br_ret_b, w_br_sg=w_br_sg_b, w_br_x=w_br_x_b, w_o=w_o_b)

    pos_p = np.arange(lp)
    tabs_p = (*_rope_tables(pos_p, 1.0), *_rope_tables(pos_p, scale))
    dmat, qdec, kdec, gc = _decay(CHUNK)
    bcast = lambda a: np.ascontiguousarray(np.broadcast_to(a[:, :, None], (HEADS, CHUNK, DK)))
    dec_p = dict(dmat=dmat, qdec=bcast(qdec), kdec=bcast(kdec), gc=gc)
    pp = dict(p, sg_ws=sg_ws[0], sg_bias=jnp.repeat(sg_bs[0].T, DK, axis=1))
    x_mid, s_prompt, w_up_b, w_down_b = _prompt_mixer(x_prompt, tabs_p, mk_b, mv_b, dec_p, pp,
                                                      [w_up[0], w_down[0]])
    p = dict(p, w_up=w_up_b, w_down=w_down_b)

    sbb = bs // (bp * (lp // PROMPT_TILE))
    rb = sbb * per
    blk16 = np.einsum("hk,hls->hlks", np.eye(HEADS, dtype=np.float32), dmat4).reshape(per, per)
    dec_s = dict(gc=gc4,
                 dmat=np.kron(np.eye(sbb, dtype=np.float32), blk16),
                 qdec=np.ascontiguousarray(np.broadcast_to(np.tile(qdec4.reshape(per), sbb)[:, None], (rb, DV))),
                 gn=jnp.tile(jnp.repeat(ret_gn_g[0], ls, axis=0), (sbb, 1)))
    y_prompt, tail, orete, oxe, s_sample = _prompt_ffn(
        x_mid, p, (qe, kre, kde, ve, ge, xqe), state_ret[0].reshape(bs, HEADS * DK, DV),
        cache_mem_k[0].reshape(bs, MEM_LEN * HEADS, DK), cache_mem_v[0].reshape(bs, MEM_LEN * HEADS, DK), dec_s)
    xs_mid = _sample_merge(xs, orete, oxe, osg, gates, p)
    y_sample, c2a, c2b, c3a, c3b = _sample_ffn(xs_mid, jnp.swapaxes(state_conv[0], 0, 1), p)
    conv_s = jnp.stack([jnp.concatenate([c2a, c2b], -1), jnp.concatenate([c3a, c3b], -1)], axis=1)
    unpos = lambda a: jnp.swapaxes(a.reshape(ls, bs, a.shape[-1]), 0, 1)

    return (y_prompt, unpos(y_sample),
            s_prompt[None], tail[None],
            mk.reshape(1, bp, MEM_LEN, HEADS, DK), mv.reshape(1, bp, MEM_LEN, HEADS, DK),
            s_sample.reshape(1, bs, HEADS, DK, DV), conv_s[None],
            unpos(vrows)[None])
```

```python
import numpy as np
import jax
import jax.numpy as jnp
from jax import lax
from jax.experimental import pallas as pl
from jax.experimental.pallas import tpu as pltpu

F32 = jnp.float32
BF16 = jnp.bfloat16

D_MODEL = 1024
HEADS = 4
DK = 128
DV = 256
QK_W = HEADS * DK
V_W = HEADS * DV
SG_W = 512
X_W = 512
MEM_LEN = 256
D_FF = 2816
CHUNK = 128
ROPE_BASE = 10000.0
EPS = 1e-6
PAST_LEN = 16384

C_Q, C_K, C_V, C_G, C_SU, C_SV, C_XQ, C_GT, C_END = 0, 512, 1024, 2048, 3072, 3584, 4096, 4608, 7680

PROMPT_TILE = 512
EXP = HEADS
FFN_CW = 1408
VMEM_BYTES_V7X = 64 * 1024 * 1024
VMEM_LIMIT = VMEM_BYTES_V7X - 3 * 1024 * 1024


def _rms(x, g):
    return x * lax.rsqrt(jnp.mean(x * x, axis=-1, keepdims=True) + EPS) * g


def _stdnorm(x):
    mu = jnp.mean(x, axis=-1, keepdims=True)
    xc = x - mu
    var = jnp.mean(xc * xc, axis=-1, keepdims=True)
    return xc * lax.rsqrt(var + EPS)


_GELU_C0 = np.float32(np.sqrt(2.0 / np.pi))
_GELU_C1 = np.float32(np.sqrt(2.0 / np.pi) * 0.044715)


def _gelu_tanh(x):
    return jnp.tanh(x * (_GELU_C0 + _GELU_C1 * (x * x)))


def _gelu(x):
    hx = 0.5 * x
    return hx + hx * _gelu_tanh(x)


def _sigmoid(x):
    return 0.5 + 0.5 * jnp.tanh(0.5 * x)


def _silu(x):
    hx = 0.5 * x
    return hx + hx * jnp.tanh(hx)


def _softmax(s):
    e = jnp.exp(s - jnp.max(s, axis=-1, keepdims=True))
    return e * (1.0 / jnp.sum(e, axis=-1, keepdims=True))


def _mm(a, b):
    return jnp.dot(a.astype(BF16), b.astype(BF16), preferred_element_type=F32)


def _mm_nt(a, b):
    return lax.dot_general(a.astype(BF16), b.astype(BF16), (((1,), (1,)), ((), ())),
                           preferred_element_type=F32)


def _mm_tn(a, b):
    return lax.dot_general(a.astype(BF16), b.astype(BF16), (((0,), (0,)), ((), ())),
                           preferred_element_type=F32)


def _rope(x, cos, sin):
    return x * cos + pltpu.roll(x, DK // 2, 1) * sin


def _tril(w):
    r = lax.broadcasted_iota(jnp.int32, w.shape, 0)
    c = lax.broadcasted_iota(jnp.int32, w.shape, 1)
    return jnp.where(r >= c, w, 0.0)


def _conv_gate(zs_ref, n, cols_a, cols_b, cw_a, cb_a, cw_b, cb_b):
    def conv(cols, cw, cb):
        zz = zs_ref[0:8 + n, cols]
        blocks = zz.reshape(n // 8 + 1, 8, zz.shape[-1])
        prev, cur = blocks[:-1], blocks[1:]
        sub = lax.broadcasted_iota(jnp.int32, (1, 8, 1), 1)
        z1 = pltpu.roll(jnp.where(sub < 7, cur, prev), 1, 1).reshape(n, zz.shape[-1])
        z2 = pltpu.roll(jnp.where(sub < 6, cur, prev), 2, 1).reshape(n, zz.shape[-1])
        return cb + cw[0:1] * z2 + cw[1:2] * z1 + cw[2:3] * zz[8:]
    a = conv(cols_a, cw_a, cb_a)
    return (a + a * _gelu_tanh(a)) * conv(cols_b, 0.5 * cw_b, 0.5 * cb_b)


def _const_spec(shape):
    nd = len(shape)
    return pl.BlockSpec(shape, lambda *_: (0,) * nd, pipeline_mode=pl.Buffered(1))


def _smem_spec():
    return pl.BlockSpec(memory_space=pltpu.SMEM)


def _params(sem):
    return pltpu.CompilerParams(dimension_semantics=sem, vmem_limit_bytes=VMEM_LIMIT)


def _cast_blocks(refs):
    for src, dst in refs:
        dst[...] = src[...].astype(BF16)


def _cast_specs(weights, steps, flat_step):
    specs, shapes = [], []
    for w in weights:
        rows, cols = w.shape
        nblk = max(n for n in range(1, steps + 1) if steps % n == 0 and rows % (16 * n) == 0)
        specs.append(pl.BlockSpec((rows // nblk, cols), lambda *g, r=steps // nblk: (flat_step(*g) // r, 0)))
        shapes.append(jax.ShapeDtypeStruct(w.shape, BF16))
    return specs, shapes


def _memkv_kernel(mem_ref, g_ref, w_ref, *refs):
    ncast = (len(refs) - 4) // 2
    casts, (k_ref, v_ref, kb_ref, vb_ref) = refs[:ncast], refs[ncast:ncast + 4]
    kv = _mm(_rms(mem_ref[...], g_ref[...]), w_ref[...])
    k, v = kv[:, :X_W], kv[:, X_W:]
    for hd in range(HEADS):
        cols = slice(hd * DK, (hd + 1) * DK)
        k_ref[pl.ds(hd, MEM_LEN, stride=HEADS), :] = k[:, cols]
        v_ref[pl.ds(hd, MEM_LEN, stride=HEADS), :] = v[:, cols]
    kb_ref[...] = k.astype(BF16)
    vb_ref[...] = v.astype(BF16)
    _cast_blocks(zip(casts, refs[ncast + 4:]))


def _memkv(mem, g, w, cast_weights):
    b = mem.shape[0]
    blk = pl.BlockSpec((None, MEM_LEN, X_W), lambda i: (i, 0, 0))
    flat = pl.BlockSpec((None, MEM_LEN * HEADS, DK), lambda i: (i, 0, 0))
    cspec, cshape = _cast_specs(cast_weights, b, lambda i: i)
    return pl.pallas_call(
        _memkv_kernel,
        grid=(b,),
        in_specs=[pl.BlockSpec((None, MEM_LEN, D_MODEL), lambda i: (i, 0, 0)),
                  _const_spec((1, D_MODEL)), _const_spec((D_MODEL, 2 * X_W))] + cspec,
        out_specs=[flat, flat, blk, blk] + cspec,
        out_shape=[jax.ShapeDtypeStruct((b, MEM_LEN * HEADS, DK), F32)] * 2
        + [jax.ShapeDtypeStruct((b, MEM_LEN, X_W), BF16)] * 2 + cshape,
        compiler_params=_params(("arbitrary",)),
        name="mem_kv",
    )(mem, g, w, *cast_weights)


def _mixer_kernel(gc_ref, x_ref, tab_ref, mk_ref, mv_ref, ng_ref, win_ref, bg_ref,
                  gn_ref, lng_ref, ws_ref, sgb_ref, dmat_ref, qdec_ref, kdec_ref,
                  wr_ref, wsg_ref, wx_ref, wo_ref, wup_ref, wdn_ref,
                  xo_ref, s_ref, wup_o_ref, wdn_o_ref, oret_ref, osg_ref, ox_ref):
    tile = x_ref.shape[0]

    @pl.when(pl.program_id(1) == 0)
    def _():
        s_ref[...] = jnp.zeros_like(s_ref)

    x = x_ref[...]
    h = _rms(x, ng_ref[...]).astype(BF16)

    def proj(a, b):
        return jnp.dot(h, win_ref[:, a:b], preferred_element_type=F32)

    heads, chunks = range(HEADS), range(tile // CHUNK)
    rows = [slice(c * CHUNK, (c + 1) * CHUNK) for c in chunks]
    kcols = [slice(hd * DK, (hd + 1) * DK) for hd in heads]
    vcols = [slice(hd * DV, (hd + 1) * DV) for hd in heads]

    suv = proj(C_SU, C_XQ)
    su, sv = suv[:, :SG_W], suv[:, SG_W:]
    qk = proj(C_Q, C_V)
    q, k = qk[:, :QK_W], qk[:, QK_W:]
    u = _gelu(su)
    vn = (_stdnorm(_gelu(sv)) * lng_ref[...]).astype(BF16)
    wsg = [_tril(ws_ref[g]).astype(BF16) for g in heads]
    vb = proj(C_V, C_G).astype(BF16)
    mixed = [[jnp.dot(wsg[g], vn[rows[c], kcols[g]], preferred_element_type=F32) + sgb_ref[:, kcols[g]]
              for c in chunks] for g in heads]
    gsil = _silu(proj(C_G, C_SU))
    for g in heads:
        osg_ref[:, kcols[g]] = (u[:, kcols[g]] * jnp.concatenate(mixed[g], axis=0)).astype(BF16)

    cq, sq, ck, sk = (tab_ref[:, i * DK:(i + 1) * DK] for i in range(4))
    qr = [_rope(q[:, kcols[hd]], cq, sq) for hd in heads]
    kr = [_rope(k[:, kcols[hd]], ck, sk) for hd in heads]
    sc = [[_mm_nt(qr[hd][rows[c]], kr[hd][rows[c]]) for c in chunks] for hd in heads]
    upd = [[_mm_tn(kr[hd][rows[c]] * kdec_ref[hd], vb[rows[c], vcols[hd]]) for c in chunks] for hd in heads]
    xq = proj(C_XQ, C_GT)
    gmid = C_GT + (C_END - C_GT) // 2
    gt0 = proj(C_GT, gmid)
    states = []
    for hd in heads:
        st, before = s_ref[hd], []
        for c in chunks:
            before.append(st.astype(BF16))
            st = gc_ref[hd] * st + upd[hd][c]
        s_ref[hd] = st
        states.append(before)
    o = [[jnp.dot(jnp.concatenate([(sc[hd][c] * dmat_ref[hd]).astype(BF16),
                                   (qr[hd][rows[c]] * qdec_ref[hd]).astype(BF16)], axis=1),
                  jnp.concatenate([vb[rows[c], vcols[hd]], states[hd][c]], axis=0),
                  preferred_element_type=F32)
          for c in chunks] for hd in heads]
    sx = [_mm_nt(xq[:, kcols[hd]], mk_ref[:, kcols[hd]]) * np.float32(DK ** -0.5) for hd in heads]
    for hd in heads:
        on = _stdnorm(jnp.concatenate(o[hd], axis=0)) * gn_ref[:, vcols[hd]]
        oret_ref[:, vcols[hd]] = (gsil[:, vcols[hd]] * on).astype(BF16)
    gt1 = proj(gmid, C_END)
    px = [_softmax(sx[hd]) for hd in heads]
    for hd in heads:
        ox_ref[:, kcols[hd]] = _mm(px[hd], mv_ref[:, kcols[hd]]).astype(BF16)

    gates = _sigmoid(jnp.concatenate([gt0, gt1], axis=1) + bg_ref[...])
    merged = (gates[:, :D_MODEL] * jnp.dot(oret_ref[...], wr_ref[...], preferred_element_type=F32)
              + gates[:, D_MODEL:2 * D_MODEL] * jnp.dot(osg_ref[...], wsg_ref[...], preferred_element_type=F32)
              + gates[:, 2 * D_MODEL:] * jnp.dot(ox_ref[...], wx_ref[...], preferred_element_type=F32))
    xo_ref[...] = x + _mm(merged, wo_ref[...])
    _cast_blocks(((wup_ref, wup_o_ref), (wdn_ref, wdn_o_ref)))


def _prompt_mixer(x, tabs, mk_b, mv_b, dec, p, cast_weights):
    b, l, _ = x.shape
    t = PROMPT_TILE
    nt = l // t
    cspec, cshape = _cast_specs(cast_weights, b * nt, lambda i, j: i * nt + j)
    tok = lambda w: pl.BlockSpec((None, t, w), lambda i, j: (i, j, 0))
    tab = pl.BlockSpec((t, 4 * DK), lambda i, j: (j, 0))
    mem = pl.BlockSpec((None, MEM_LEN, X_W), lambda i, j: (i, 0, 0))
    consts = [p["norm_mix_g"], p["w_in"], p["b_gate"], p["gn_g"], p["sg_ln_g"], p["sg_ws"], p["sg_bias"],
              dec["dmat"], dec["qdec"], dec["kdec"], p["w_br_ret"], p["w_br_sg"], p["w_br_x"], p["w_o"]]
    return pl.pallas_call(
        _mixer_kernel,
        grid=(b, l // t),
        in_specs=[_smem_spec(), tok(D_MODEL), tab, mem, mem]
        + [_const_spec(c.shape) for c in consts] + cspec,
        out_specs=[tok(D_MODEL), pl.BlockSpec((None, HEADS, DK, DV), lambda i, j: (i, 0, 0, 0))] + cspec,
        out_shape=[jax.ShapeDtypeStruct((b, l, D_MODEL), F32),
                   jax.ShapeDtypeStruct((b, HEADS, DK, DV), F32)] + cshape,
        scratch_shapes=[pltpu.VMEM((t, V_W), BF16), pltpu.VMEM((t, SG_W), BF16), pltpu.VMEM((t, X_W), BF16)],
        compiler_params=_params(("arbitrary", "arbitrary")),
        name="prompt_mixer",
    )(dec["gc"], x, np.concatenate(tabs, axis=1), mk_b, mv_b, *consts, *cast_weights)


def _row_head(n):
    return (lax.broadcasted_iota(jnp.int32, (n, 1), 0) >> 2) & (HEADS - 1)


def _state_matmuls(gc_ref, qe_ref, kre_ref, kde_ref, ve_ref, xqe_ref, s_ref, mk_ref, dmat_ref, so_ref):
    rows_n = qe_ref.shape[0]
    per = HEADS * EXP
    batches = range(rows_n // per)
    rows = [slice(b * per, (b + 1) * per) for b in batches]
    q, ve, xq = qe_ref[...], ve_ref[...], xqe_ref[...]
    sc = _mm_nt(q, kre_ref[...]) * dmat_ref[...]
    kd = kde_ref[...]
    rowb = lax.broadcasted_iota(jnp.int32, (rows_n, 1), 0) >> 4
    cross = jnp.concatenate(
        [jnp.dot(q[rows[b]], s_ref[b].astype(BF16), preferred_element_type=F32) for b in batches], axis=0)
    upd = [_mm_tn(kd, jnp.where(rowb == b, ve, jnp.zeros_like(ve))) for b in batches]
    for b in batches:
        for hd in range(HEADS):
            hr = slice(hd * DK, (hd + 1) * DK)
            so_ref[b, hr, :] = gc_ref[hd] * s_ref[b, hr, :] + upd[b][hr]
    sx = jnp.concatenate([_mm_nt(xq[rows[b]], mk_ref[b]) for b in batches], axis=0) * np.float32(DK ** -0.5)
    return sc, cross, sx


def _state_outputs(sc, cross, sx, ve_ref, ge_ref, mv_ref, qdec_ref, gn_ref, oret_ref, ox_ref):
    rows_n = ve_ref.shape[0]
    per = HEADS * EXP
    batches = range(rows_n // per)
    rows = [slice(b * per, (b + 1) * per) for b in batches]
    inner = jnp.dot(sc.astype(BF16), ve_ref[...], preferred_element_type=F32)
    hh = _row_head(rows_n)
    own_col = (lax.broadcasted_iota(jnp.int32, (rows_n, MEM_LEN * HEADS), 1) & (HEADS - 1)) == hh
    p = _softmax(jnp.where(own_col, sx, np.float32(-1e30))).astype(BF16)
    oxs = jnp.concatenate([_mm(p[rows[b]], mv_ref[b]) for b in batches], axis=0)
    o = inner + cross * qdec_ref[...]
    og = ge_ref[...] * (_stdnorm(o) * gn_ref[...])
    for hd in range(HEADS):
        oret_ref[:, hd * DV:(hd + 1) * DV] = jnp.where(hh == hd, og, 0.0).astype(BF16)
        ox_ref[:, hd * DK:(hd + 1) * DK] = jnp.where(hh == hd, oxs, 0.0).astype(BF16)


def _ffn_kernel(gc_ref, x_ref, ng_ref, wup_ref, cw_ref, cb_ref, wdn_ref, nf_ref,
                qe_ref, kre_ref, kde_ref, ve_ref, ge_ref, xqe_ref, s_ref, mk_ref, mv_ref, dmat_ref, qdec_ref, gn_ref,
                y_ref, tail_ref, oret_ref, ox_ref, so_ref, zs_ref):
    tile = x_ref.shape[0]
    first = pl.program_id(1) == 0

    @pl.when(first)
    def _():
        zs_ref[0:8, :] = jnp.zeros((8, 2 * D_FF), F32)

    @pl.when(jnp.logical_not(first))
    def _():
        zs_ref[0:8, :] = zs_ref[tile:tile + 8, :]

    sc, cross, sx = _state_matmuls(gc_ref, qe_ref, kre_ref, kde_ref, ve_ref, xqe_ref, s_ref, mk_ref, dmat_ref,
                                   so_ref)
    x = x_ref[...]
    zs_ref[8:8 + tile, :] = _mm(_rms(x, ng_ref[...]), wup_ref[...])
    _state_outputs(sc, cross, sx, ve_ref, ge_ref, mv_ref, qdec_ref, gn_ref, oret_ref, ox_ref)
    ca, cb = slice(0, D_FF), slice(D_FF, 2 * D_FF)
    gate = _conv_gate(zs_ref, tile, ca, cb, cw_ref[:, ca], cb_ref[:, ca], cw_ref[:, cb], cb_ref[:, cb])
    y = x + _mm(gate, wdn_ref[...])
    y_ref[...] = _rms(y, nf_ref[...])
    tail_ref[...] = zs_ref[tile + 6:tile + 8, :]


def _prompt_ffn(x, p, pre, state, mk, mv, dec):
    b, l, _ = x.shape
    t = PROMPT_TILE
    nt = l // t
    qe, kre, kde, ve, ge, xqe = pre
    nb = state.shape[0]
    bb = nb // (b * nt)
    rb = bb * HEADS * EXP
    ne = qe.shape[0]
    tok = pl.BlockSpec((None, t, D_MODEL), lambda i, j: (i, j, 0))
    rowblk = lambda w: pl.BlockSpec((rb, w), lambda i, j: (i * nt + j, 0))
    batblk = lambda a, c: pl.BlockSpec((bb, a, c), lambda i, j: (i * nt + j, 0, 0))
    consts = [p["norm_ffn_g"], p["w_up"], p["conv_w"], p["conv_b"], p["w_down"], p["norm_final_g"]]
    return pl.pallas_call(
        _ffn_kernel,
        grid=(b, nt),
        in_specs=[_smem_spec(), tok] + [_const_spec(c.shape) for c in consts]
        + [rowblk(QK_W), rowblk(QK_W), rowblk(QK_W), rowblk(DV), rowblk(DV), rowblk(DK),
           batblk(HEADS * DK, DV), batblk(MEM_LEN * HEADS, DK), batblk(MEM_LEN * HEADS, DK),
           _const_spec((rb, rb)), _const_spec((rb, DV)), _const_spec((rb, DV))],
        out_specs=[tok, pl.BlockSpec((None, 2, 2 * D_FF), lambda i, j: (i, 0, 0)),
                   rowblk(V_W), rowblk(X_W), batblk(HEADS * DK, DV)],
        out_shape=[jax.ShapeDtypeStruct((b, l, D_MODEL), F32), jax.ShapeDtypeStruct((b, 2, 2 * D_FF), F32),
                   jax.ShapeDtypeStruct((ne, V_W), BF16), jax.ShapeDtypeStruct((ne, X_W), BF16),
                   jax.ShapeDtypeStruct(state.shape, F32)],
        scratch_shapes=[pltpu.VMEM((t + 8, 2 * D_FF), F32)],
        compiler_params=_params(("arbitrary", "arbitrary")),
        name="prompt_ffn",
    )(dec["gc"], x, *consts, qe, kre, kde, ve, ge, xqe, state, mk, mv, dec["dmat"], dec["qdec"], dec["gn"])


def _sample_pre_kernel(ws_ref, bs_ref, x_ref, cq_ref, sq_ref, ck_ref, sk_ref, kdec_ref, ng_ref, win_ref, bg_ref,
                       lng_ref, wr_ref, wsg_ref, wx_ref, wo_ref,
                       qe_ref, kre_ref, kde_ref, ve_ref, ge_ref, xqe_ref, osg_ref, vrows_ref, gates_ref,
                       wr_o_ref, wsg_o_ref, wx_o_ref, wo_o_ref, h_ref, vn_ref):
    n = x_ref.shape[0]
    nb = n // EXP
    ne = qe_ref.shape[0]
    i = pl.program_id(0)

    @pl.when(i == 0)
    def _():
        h_ref[...] = _rms(x_ref[...], ng_ref[...]).astype(BF16)
        vn_ref[...] = jnp.zeros_like(vn_ref)

    r = lax.broadcasted_iota(jnp.int32, (ne, n), 0)
    c = lax.broadcasted_iota(jnp.int32, (ne, n), 1)
    rep = jnp.where(c == (r & (EXP - 1)) * nb + i * (ne // (HEADS * EXP)) + (r >> 4), 1.0, 0.0).astype(BF16)
    he = jnp.dot(rep, h_ref[...], preferred_element_type=F32).astype(BF16)
    hh = _row_head(ne)
    h = h_ref[pl.ds(pl.multiple_of(i * nb, nb), nb), :]

    def proj(hm, a, b):
        return jnp.dot(hm, win_ref[:, a:b], preferred_element_type=F32)

    q, k = proj(he, C_Q, C_K), proj(he, C_K, C_V)
    xq = proj(he, C_XQ, C_GT)
    cq, sq, ck, sk, kdec = cq_ref[...], sq_ref[...], ck_ref[...], sk_ref[...], kdec_ref[...]
    for hd in range(HEADS):
        cols = slice(hd * DK, (hd + 1) * DK)
        own = hh == hd
        kr = _rope(k[:, cols], ck, sk)
        qe_ref[:, cols] = jnp.where(own, _rope(q[:, cols], cq, sq), 0.0).astype(BF16)
        kre_ref[:, cols] = jnp.where(own, kr, 0.0).astype(BF16)
        kde_ref[:, cols] = jnp.where(own, kr * kdec, 0.0).astype(BF16)
    v = proj(he, C_V, C_G)
    g = proj(he, C_G, C_SU)
    ve = jnp.zeros((ne, DV), F32)
    ge = jnp.zeros((ne, DV), F32)
    xqc = jnp.zeros((ne, DK), F32)
    for hd in range(HEADS):
        cols = slice(hd * DV, (hd + 1) * DV)
        own = hh == hd
        ve = jnp.where(own, v[:, cols], ve)
        ge = jnp.where(own, g[:, cols], ge)
        xqc = jnp.where(own, xq[:, hd * DK:(hd + 1) * DK], xqc)
    xqe_ref[...] = xqc.astype(BF16)
    ve_ref[...] = ve.astype(BF16)
    ge_ref[...] = _silu(ge)

    u = _gelu(proj(h, C_SU, C_SV))
    vn = _stdnorm(_gelu(proj(h, C_SV, C_XQ))) * lng_ref[...]
    vrows_ref[...] = vn
    vn_ref[i] = vn
    for g in range(HEADS):
        cols = slice(g * DK, (g + 1) * DK)
        mixed = jnp.full((nb, DK), bs_ref[g * EXP + i], F32)
        for s in range(EXP):
            w = jnp.where(s <= i, ws_ref[(g * EXP + i) * EXP + s], 0.0)
            mixed = mixed + w * vn_ref[s, :, cols]
        osg_ref[:, cols] = (u[:, cols] * mixed).astype(BF16)
    gates_ref[...] = _sigmoid(proj(h, C_GT, C_END) + bg_ref[...])
    _cast_blocks(((wr_ref, wr_o_ref), (wsg_ref, wsg_o_ref), (wx_ref, wx_o_ref), (wo_ref, wo_o_ref)))


def _sample_pre(x, tabs, kdec_rows, p, ws4, bs4, cast_weights):
    n = x.shape[0]
    cspec, cshape = _cast_specs(cast_weights, EXP, lambda i: i)
    t = n // EXP
    te = (t // EXP) * HEADS * EXP
    consts = [p["norm_mix_g"], p["w_in"], p["b_gate"], p["sg_ln_g"]]
    rowblk = lambda rows, w: pl.BlockSpec((rows, w), lambda i: (i, 0))
    outs = [(te, QK_W, BF16), (te, QK_W, BF16), (te, QK_W, BF16), (te, DV, BF16), (te, DV, F32),
            (te, DK, BF16), (t, SG_W, BF16), (t, SG_W, F32), (t, 3 * D_MODEL, F32)]
    return pl.pallas_call(
        _sample_pre_kernel,
        grid=(EXP,),
        in_specs=[_smem_spec(), _smem_spec(), _const_spec(x.shape)] + [_const_spec((te, DK))] * 5
        + [_const_spec(c.shape) for c in consts] + cspec,
        out_specs=[rowblk(r, w) for r, w, _ in outs] + cspec,
        out_shape=[jax.ShapeDtypeStruct((r * EXP, w), d) for r, w, d in outs] + cshape,
        scratch_shapes=[pltpu.VMEM((n, D_MODEL), BF16), pltpu.VMEM((EXP, t, SG_W), F32)],
        compiler_params=_params(("arbitrary",)),
        name="sample_pre",
    )(ws4, bs4, x, *tabs, kdec_rows, *consts, *cast_weights)


def _sample_merge_kernel(x_ref, orete_ref, oxe_ref, osg_ref, gates_ref, wr_ref, wsg_ref, wx_ref, wo_ref, xo_ref):
    n = x_ref.shape[0]
    ne = n * EXP
    r = lax.broadcasted_iota(jnp.int32, (n, ne), 0)
    c = lax.broadcasted_iota(jnp.int32, (n, ne), 1)
    col = jnp.where(r == (c & (EXP - 1)) * (n // EXP) + (c >> 4), 1.0, 0.0).astype(BF16)
    oret = jnp.dot(col, orete_ref[...], preferred_element_type=F32)
    ox = jnp.dot(col, oxe_ref[...], preferred_element_type=F32)
    gates = gates_ref[...]
    merged = (gates[:, :D_MODEL] * _mm(oret, wr_ref[...])
              + gates[:, D_MODEL:2 * D_MODEL] * jnp.dot(osg_ref[...], wsg_ref[...], preferred_element_type=F32)
              + gates[:, 2 * D_MODEL:] * _mm(ox, wx_ref[...]))
    xo_ref[...] = x_ref[...] + _mm(merged, wo_ref[...])


def _sample_merge(x, orete, oxe, osg, gates, p):
    ins = [x, orete, oxe, osg, gates, p["w_br_ret"], p["w_br_sg"], p["w_br_x"], p["w_o"]]
    return pl.pallas_call(
        _sample_merge_kernel,
        grid=(1,),
        in_specs=[_const_spec(a.shape) for a in ins],
        out_specs=pl.BlockSpec(x.shape, lambda i: (0, 0)),
        out_shape=jax.ShapeDtypeStruct(x.shape, F32),
        compiler_params=_params(("arbitrary",)),
        name="sample_merge",
    )(*ins)


def _sample_ffn_kernel(x_ref, ng_ref, wa_ref, wb_ref, cwa_ref, cwb_ref, cba_ref, cbb_ref, s0a_ref, s0b_ref,
                       s1a_ref, s1b_ref, wdn_ref, nf_ref,
                       y_ref, c2a_ref, c2b_ref, c3a_ref, c3b_ref, h_ref, acc_ref):
    nb = x_ref.shape[0] // EXP
    j = pl.program_id(0)

    @pl.when(j == 0)
    def _():
        h_ref[...] = _rms(x_ref[...], ng_ref[...]).astype(BF16)
        acc_ref[...] = jnp.zeros_like(acc_ref)

    h = h_ref[...]

    def conv(w_ref, cw_ref, cb_ref, s0_ref, s1_ref, c2_ref, c3_ref):
        z = jnp.dot(h, w_ref[...], preferred_element_type=F32)
        zp = [s0_ref[...], s1_ref[...]] + [z[l * nb:(l + 1) * nb] for l in range(EXP)]
        c2_ref[...] = zp[EXP]
        c3_ref[...] = zp[EXP + 1]
        cw, cb = cw_ref[...], cb_ref[...]
        return jnp.concatenate([cb + cw[0:1] * zp[l] + cw[1:2] * zp[l + 1] + cw[2:3] * zp[l + 2]
                                for l in range(EXP)], axis=0)

    a = conv(wa_ref, cwa_ref, cba_ref, s0a_ref, s1a_ref, c2a_ref, c3a_ref)
    b = conv(wb_ref, cwb_ref, cbb_ref, s0b_ref, s1b_ref, c2b_ref, c3b_ref)
    acc_ref[...] += _mm(_gelu(a) * b, wdn_ref[...])

    @pl.when(j == pl.num_programs(0) - 1)
    def _():
        y_ref[...] = _rms(x_ref[...] + acc_ref[...], nf_ref[...])


def _sample_ffn(x, sc, p):
    n = x.shape[0]
    nb = n // EXP
    cw = FFN_CW
    nch = D_FF // cw
    ca = lambda rows: pl.BlockSpec((rows, cw), lambda j: (0, j))
    cb = lambda rows: pl.BlockSpec((rows, cw), lambda j: (0, nch + j))
    sa = lambda k: pl.BlockSpec((None, nb, cw), lambda j: (k, 0, j))
    sb = lambda k: pl.BlockSpec((None, nb, cw), lambda j: (k, 0, nch + j))
    full = pl.BlockSpec((n, D_MODEL), lambda j: (0, 0))
    vec = pl.BlockSpec((1, D_MODEL), lambda j: (0, 0))
    return pl.pallas_call(
        _sample_ffn_kernel,
        grid=(nch,),
        in_specs=[full, vec, ca(D_MODEL), cb(D_MODEL), ca(3), cb(3), ca(1), cb(1), sa(0), sb(0), sa(1), sb(1),
                  pl.BlockSpec((cw, D_MODEL), lambda j: (j, 0)), vec],
        out_specs=[full] + [ca(nb)] * 4,
        out_shape=[jax.ShapeDtypeStruct((n, D_MODEL), F32)] + [jax.ShapeDtypeStruct((nb, D_FF), F32)] * 4,
        scratch_shapes=[pltpu.VMEM((n, D_MODEL), BF16), pltpu.VMEM((n, D_MODEL), F32)],
        compiler_params=_params(("arbitrary",)),
        name="sample_ffn",
    )(x, p["norm_ffn_g"], p["w_up"], p["w_up"], p["conv_w"], p["conv_w"], p["conv_b"], p["conv_b"],
      sc, sc, sc, sc, p["w_down"], p["norm_final_g"])


def _rope_tables(pos, scale):
    inv = ROPE_BASE ** (-np.arange(0, DK, 2, dtype=np.float64) / DK)
    ang = np.asarray(pos, np.float64)[:, None] * inv[None, :]
    cos, sin = np.cos(ang), np.sin(ang)
    return (np.concatenate([cos, cos], -1) * scale).astype(np.float32), \
        (np.concatenate([-sin, sin], -1) * scale).astype(np.float32)


def _decay(chunk):
    lg = np.log1p(-np.exp2(-5.0 - np.arange(HEADS, dtype=np.float64)))
    n = np.arange(chunk, dtype=np.float64)
    diff = n[:, None] - n[None, :]
    dmat = np.where(diff >= 0, np.exp(np.maximum(diff, 0.0)[None] * lg[:, None, None]), 0.0)
    qdec = np.exp((n + 1.0)[None, :] * lg[:, None])
    kdec = np.exp((chunk - 1.0 - n)[None, :] * lg[:, None])
    f32 = lambda a: a.astype(np.float32)
    return f32(dmat), f32(qdec), f32(kdec), f32(np.exp(chunk * lg))


def kernel(x_prompt, x_sample, mem_prompt, state_ret, state_conv, cache_mem_k, cache_mem_v, norm_mix_g, w_in,
           b_gate, ret_gn_g, sg_ln_g, sg_ws, sg_bs, mem_norm_g, w_mem_kv, w_br_ret, w_br_sg, w_br_x, w_o,
           norm_ffn_g, w_up, conv_w, conv_b, w_down, norm_final_g):
    bp, lp, _ = x_prompt.shape
    bs, ls, _ = x_sample.shape
    assert state_ret.shape[0] == 1 and ls == EXP and lp % PROMPT_TILE == 0
    assert bs % (bp * (lp // PROMPT_TILE)) == 0
    assert bs == CHUNK
    row = lambda a: a.reshape(1, -1)
    scale = DK ** -0.5

    mk, mv, mk_b, mv_b, w_in_b = _memkv(mem_prompt, row(mem_norm_g[0]), w_mem_kv[0], [w_in[0]])
    p = dict(norm_mix_g=row(norm_mix_g[0]), w_in=w_in_b, b_gate=row(b_gate[0]),
             gn_g=row(ret_gn_g[0]), sg_ln_g=row(sg_ln_g[0]),
             norm_ffn_g=row(norm_ffn_g[0]), conv_w=conv_w[0], conv_b=row(conv_b[0]),
             norm_final_g=row(norm_final_g))

    n = bs * ls
    per = HEADS * EXP
    te = (bs // EXP) * per
    pos_s = PAST_LEN + (np.arange(te) & (ls - 1))
    tabs_s = (*_rope_tables(pos_s, 1.0), *_rope_tables(pos_s, scale))
    dmat4, qdec4, kdec4, gc4 = _decay(ls)
    kdec_rows = np.ascontiguousarray(np.broadcast_to(np.tile(kdec4.reshape(per), te // per)[:, None], (te, DK)))
    xs = jnp.swapaxes(x_sample, 0, 1).reshape(n, D_MODEL)
    pre = _sample_pre(xs, tabs_s, kdec_rows, p, sg_ws[0][:, :ls, :ls].reshape(-1), sg_bs[0][:, :ls].reshape(-1),
                      [w_br_ret[0], w_br_sg[0], w_br_x[0], w_o[0]])
    qe, kre, kde, ve, ge, xqe, osg, vrows, gates, w_br_ret_b, w_br_sg_b, w_br_x_b, w_o_b = pre
    p = dict(p, w_br_ret=w_br_ret_b, w_br_sg=w_br_sg_b, w_br_x=w_br_x_b, w_o=w_o_b)

    pos_p = np.arange(lp)
    tabs_p = (*_rope_tables(pos_p, 1.0), *_rope_tables(pos_p, scale))
    dmat, qdec, kdec, gc = _decay(CHUNK)
    bcast = lambda a: np.ascontiguousarray(np.broadcast_to(a[:, :, None], (HEADS, CHUNK, DK)))
    dec_p = dict(dmat=dmat, qdec=bcast(qdec), kdec=bcast(kdec), gc=gc)
    pp = dict(p, sg_ws=sg_ws[0], sg_bias=jnp.repeat(sg_bs[0].T, DK, axis=1))
    x_mid, s_prompt, w_up_b, w_down_b = _prompt_mixer(x_prompt, tabs_p, mk_b, mv_b, dec_p, pp,
                                                      [w_up[0], w_down[0]])
    p = dict(p, w_up=w_up_b, w_down=w_down_b)

    sbb = bs // (bp * (lp // PROMPT_TILE))
    rb = sbb * per
    blk16 = np.einsum("hk,hls->hlks", np.eye(HEADS, dtype=np.float32), dmat4).reshape(per, per)
    dec_s = dict(gc=gc4,
                 dmat=np.kron(np.eye(sbb, dtype=np.float32), blk16),
                 qdec=np.ascontiguousarray(np.broadcast_to(np.tile(qdec4.reshape(per), sbb)[:, None], (rb, DV))),
                 gn=jnp.tile(jnp.repeat(ret_gn_g[0], ls, axis=0), (sbb, 1)))
    y_prompt, tail, orete, oxe, s_sample = _prompt_ffn(
        x_mid, p, (qe, kre, kde, ve, ge, xqe), state_ret[0].reshape(bs, HEADS * DK, DV),
        cache_mem_k[0].reshape(bs, MEM_LEN * HEADS, DK), cache_mem_v[0].reshape(bs, MEM_LEN * HEADS, DK), dec_s)
    xs_mid = _sample_merge(xs, orete, oxe, osg, gates, p)
    y_sample, c2a, c2b, c3a, c3b = _sample_ffn(xs_mid, jnp.swapaxes(state_conv[0], 0, 1), p)
    conv_s = jnp.stack([jnp.concatenate([c2a, c2b], -1), jnp.concatenate([c3a, c3b], -1)], axis=1)
    unpos = lambda a: jnp.swapaxes(a.reshape(ls, bs, a.shape[-1]), 0, 1)

    return (y_prompt, unpos(y_sample),
            s_prompt[None], tail[None],
            mk.reshape(1, bp, MEM_LEN, HEADS, DK), mv.reshape(1, bp, MEM_LEN, HEADS, DK),
            s_sample.reshape(1, bs, HEADS, DK, DV), conv_s[None],
            unpos(vrows)[None])
```

```python
import numpy as np
import jax
import jax.numpy as jnp
from jax import lax
from jax.experimental import pallas as pl
from jax.experimental.pallas import tpu as pltpu

F32 = jnp.float32
BF16 = jnp.bfloat16

D_MODEL = 1024
HEADS = 4
DK = 128
DV = 256
QK_W = HEADS * DK
V_W = HEADS * DV
SG_W = 512
X_W = 512
MEM_LEN = 256
D_FF = 2816
CHUNK = 128
ROPE_BASE = 10000.0
EPS = 1e-6
PAST_LEN = 16384

C_Q, C_K, C_V, C_G, C_SU, C_SV, C_XQ, C_GT, C_END = 0, 512, 1024, 2048, 3072, 3584, 4096, 4608, 7680

PROMPT_TILE = 512
EXP = HEADS
FFN_CW = 1408
VMEM_BYTES_V7X = 64 * 1024 * 1024
VMEM_LIMIT = VMEM_BYTES_V7X - 3 * 1024 * 1024


def _rms(x, g):
    return x * lax.rsqrt(jnp.mean(x * x, axis=-1, keepdims=True) + EPS) * g


def _stdnorm(x):
    mu = jnp.mean(x, axis=-1, keepdims=True)
    xc = x - mu
    var = jnp.mean(xc * xc, axis=-1, keepdims=True)
    return xc * lax.rsqrt(var + EPS)


_GELU_C0 = np.float32(np.sqrt(2.0 / np.pi))
_GELU_C1 = np.float32(np.sqrt(2.0 / np.pi) * 0.044715)


def _gelu_tanh(x):
    return jnp.tanh(x * (_GELU_C0 + _GELU_C1 * (x * x)))


def _gelu(x):
    hx = 0.5 * x
    return hx + hx * _gelu_tanh(x)


def _sigmoid(x):
    return 0.5 + 0.5 * jnp.tanh(0.5 * x)


def _silu(x):
    hx = 0.5 * x
    return hx + hx * jnp.tanh(hx)


def _softmax(s):
    e = jnp.exp(s - jnp.max(s, axis=-1, keepdims=True))
    return e * (1.0 / jnp.sum(e, axis=-1, keepdims=True))


def _mm(a, b):
    return jnp.dot(a.astype(BF16), b.astype(BF16), preferred_element_type=F32)


def _mm_nt(a, b):
    return lax.dot_general(a.astype(BF16), b.astype(BF16), (((1,), (1,)), ((), ())),
                           preferred_element_type=F32)


def _mm_tn(a, b):
    return lax.dot_general(a.astype(BF16), b.astype(BF16), (((0,), (0,)), ((), ())),
                           preferred_element_type=F32)


def _rope(x, cos, sin):
    return x * cos + pltpu.roll(x, DK // 2, 1) * sin


def _tril(w):
    r = lax.broadcasted_iota(jnp.int32, w.shape, 0)
    c = lax.broadcasted_iota(jnp.int32, w.shape, 1)
    return jnp.where(r >= c, w, 0.0)


def _conv_gate(zs_ref, n, cols_a, cols_b, cw_a, cb_a, cw_b, cb_b):
    def conv(cols, cw, cb):
        zz = zs_ref[0:8 + n, cols]
        blocks = zz.reshape(n // 8 + 1, 8, zz.shape[-1])
        prev, cur = blocks[:-1], blocks[1:]
        sub = lax.broadcasted_iota(jnp.int32, (1, 8, 1), 1)
        z1 = pltpu.roll(jnp.where(sub < 7, cur, prev), 1, 1).reshape(n, zz.shape[-1])
        z2 = pltpu.roll(jnp.where(sub < 6, cur, prev), 2, 1).reshape(n, zz.shape[-1])
        return cb + cw[0:1] * z2 + cw[1:2] * z1 + cw[2:3] * zz[8:]
    a = conv(cols_a, cw_a, cb_a)
    return (a + a * _gelu_tanh(a)) * conv(cols_b, 0.5 * cw_b, 0.5 * cb_b)


def _const_spec(shape):
    nd = len(shape)
    return pl.BlockSpec(shape, lambda *_: (0,) * nd, pipeline_mode=pl.Buffered(1))


def _smem_spec():
    return pl.BlockSpec(memory_space=pltpu.SMEM)


def _params(sem):
    return pltpu.CompilerParams(dimension_semantics=sem, vmem_limit_bytes=VMEM_LIMIT)


def _cast_blocks(refs):
    for src, dst in refs:
        dst[...] = src[...].astype(BF16)


def _cast_specs(weights, steps, flat_step):
    specs, shapes = [], []
    for w in weights:
        rows, cols = w.shape
        nblk = max(n for n in range(1, steps + 1) if steps % n == 0 and rows % (16 * n) == 0)
        specs.append(pl.BlockSpec((rows // nblk, cols), lambda *g, r=steps // nblk: (flat_step(*g) // r, 0)))
        shapes.append(jax.ShapeDtypeStruct(w.shape, BF16))
    return specs, shapes


def _memkv_kernel(mem_ref, g_ref, w_ref, *refs):
    ncast = (len(refs) - 4) // 2
    casts, (k_ref, v_ref, kb_ref, vb_ref) = refs[:ncast], refs[ncast:ncast + 4]
    kv = _mm(_rms(mem_ref[...], g_ref[...]), w_ref[...])
    k, v = kv[:, :X_W], kv[:, X_W:]
    for hd in range(HEADS):
        cols = slice(hd * DK, (hd + 1) * DK)
        k_ref[pl.ds(hd, MEM_LEN, stride=HEADS), :] = k[:, cols]
        v_ref[pl.ds(hd, MEM_LEN, stride=HEADS), :] = v[:, cols]
    kb_ref[...] = k.astype(BF16)
    vb_ref[...] = v.astype(BF16)
    _cast_blocks(zip(casts, refs[ncast + 4:]))


def _memkv(mem, g, w, cast_weights):
    b = mem.shape[0]
    blk = pl.BlockSpec((None, MEM_LEN, X_W), lambda i: (i, 0, 0))
    flat = pl.BlockSpec((None, MEM_LEN * HEADS, DK), lambda i: (i, 0, 0))
    cspec, cshape = _cast_specs(cast_weights, b, lambda i: i)
    return pl.pallas_call(
        _memkv_kernel,
        grid=(b,),
        in_specs=[pl.BlockSpec((None, MEM_LEN, D_MODEL), lambda i: (i, 0, 0)),
                  _const_spec((1, D_MODEL)), _const_spec((D_MODEL, 2 * X_W))] + cspec,
        out_specs=[flat, flat, blk, blk] + cspec,
        out_shape=[jax.ShapeDtypeStruct((b, MEM_LEN * HEADS, DK), F32)] * 2
        + [jax.ShapeDtypeStruct((b, MEM_LEN, X_W), BF16)] * 2 + cshape,
        compiler_params=_params(("arbitrary",)),
        name="mem_kv",
    )(mem, g, w, *cast_weights)


def _mixer_kernel(gc_ref, x_ref, tab_ref, mk_ref, mv_ref, ng_ref, win_ref, bg_ref,
                  gn_ref, lng_ref, ws_ref, sgb_ref, dmat_ref, qdec_ref, kdec_ref,
                  wr_ref, wsg_ref, wx_ref, wo_ref, wup_ref, wdn_ref,
                  xo_ref, s_ref, wup_o_ref, wdn_o_ref, oret_ref, osg_ref, ox_ref):
    tile = x_ref.shape[0]

    @pl.when(pl.program_id(1) == 0)
    def _():
        s_ref[...] = jnp.zeros_like(s_ref)

    x = x_ref[...]
    h = _rms(x, ng_ref[...]).astype(BF16)

    def proj(a, b):
        return jnp.dot(h, win_ref[:, a:b], preferred_element_type=F32)

    heads, chunks = range(HEADS), range(tile // CHUNK)
    rows = [slice(c * CHUNK, (c + 1) * CHUNK) for c in chunks]
    kcols = [slice(hd * DK, (hd + 1) * DK) for hd in heads]
    vcols = [slice(hd * DV, (hd + 1) * DV) for hd in heads]

    suv = proj(C_SU, C_XQ)
    su, sv = suv[:, :SG_W], suv[:, SG_W:]
    qk = proj(C_Q, C_V)
    q, k = qk[:, :QK_W], qk[:, QK_W:]
    u = _gelu(su)
    vn = (_stdnorm(_gelu(sv)) * lng_ref[...]).astype(BF16)
    wsg = [_tril(ws_ref[g]).astype(BF16) for g in heads]
    vb = proj(C_V, C_G).astype(BF16)
    mixed = [[jnp.dot(wsg[g], vn[rows[c], kcols[g]], preferred_element_type=F32) + sgb_ref[:, kcols[g]]
              for c in chunks] for g in heads]
    gsil = _silu(proj(C_G, C_SU))
    for g in heads:
        osg_ref[:, kcols[g]] = (u[:, kcols[g]] * jnp.concatenate(mixed[g], axis=0)).astype(BF16)

    cq, sq, ck, sk = (tab_ref[:, i * DK:(i + 1) * DK] for i in range(4))
    qr = [_rope(q[:, kcols[hd]], cq, sq) for hd in heads]
    kr = [_rope(k[:, kcols[hd]], ck, sk) for hd in heads]
    sc = [[_mm_nt(qr[hd][rows[c]], kr[hd][rows[c]]) for c in chunks] for hd in heads]
    upd = [[_mm_tn(kr[hd][rows[c]] * kdec_ref[hd], vb[rows[c], vcols[hd]]) for c in chunks] for hd in heads]
    xq = proj(C_XQ, C_GT)
    gmid = C_GT + (C_END - C_GT) // 2
    gt0 = proj(C_GT, gmid)
    states = []
    for hd in heads:
        st, before = s_ref[hd], []
        for c in chunks:
            before.append(st.astype(BF16))
            st = gc_ref[hd] * st + upd[hd][c]
        s_ref[hd] = st
        states.append(before)
    o = [[jnp.dot(jnp.concatenate([(sc[hd][c] * dmat_ref[hd]).astype(BF16),
                                   (qr[hd][rows[c]] * qdec_ref[hd]).astype(BF16)], axis=1),
                  jnp.concatenate([vb[rows[c], vcols[hd]], states[hd][c]], axis=0),
                  preferred_element_type=F32)
          for c in chunks] for hd in heads]
    sx = [_mm_nt(xq[:, kcols[hd]], mk_ref[:, kcols[hd]]) * np.float32(DK ** -0.5) for hd in heads]
    for hd in heads:
        on = _stdnorm(jnp.concatenate(o[hd], axis=0)) * gn_ref[:, vcols[hd]]
        oret_ref[:, vcols[hd]] = (gsil[:, vcols[hd]] * on).astype(BF16)
    gt1 = proj(gmid, C_END)
    px = [_softmax(sx[hd]) for hd in heads]
    for hd in heads:
        ox_ref[:, kcols[hd]] = _mm(px[hd], mv_ref[:, kcols[hd]]).astype(BF16)

    gates = _sigmoid(jnp.concatenate([gt0, gt1], axis=1) + bg_ref[...])
    merged = (gates[:, :D_MODEL] * jnp.dot(oret_ref[...], wr_ref[...], preferred_element_type=F32)
              + gates[:, D_MODEL:2 * D_MODEL] * jnp.dot(osg_ref[...], wsg_ref[...], preferred_element_type=F32)
              + gates[:, 2 * D_MODEL:] * jnp.dot(ox_ref[...], wx_ref[...], preferred_element_type=F32))
    xo_ref[...] = x + _mm(merged, wo_ref[...])
    _cast_blocks(((wup_ref, wup_o_ref), (wdn_ref, wdn_o_ref)))


def _prompt_mixer(x, tabs, mk_b, mv_b, dec, p, cast_weights):
    b, l, _ = x.shape
    t = PROMPT_TILE
    nt = l // t
    cspec, cshape = _cast_specs(cast_weights, b * nt, lambda i, j: i * nt + j)
    tok = lambda w: pl.BlockSpec((None, t, w), lambda i, j: (i, j, 0))
    tab = pl.BlockSpec((t, 4 * DK), lambda i, j: (j, 0))
    mem = pl.BlockSpec((None, MEM_LEN, X_W), lambda i, j: (i, 0, 0))
    consts = [p["norm_mix_g"], p["w_in"], p["b_gate"], p["gn_g"], p["sg_ln_g"], p["sg_ws"], p["sg_bias"],
              dec["dmat"], dec["qdec"], dec["kdec"], p["w_br_ret"], p["w_br_sg"], p["w_br_x"], p["w_o"]]
    return pl.pallas_call(
        _mixer_kernel,
        grid=(b, l // t),
        in_specs=[_smem_spec(), tok(D_MODEL), tab, mem, mem]
        + [_const_spec(c.shape) for c in consts] + cspec,
        out_specs=[tok(D_MODEL), pl.BlockSpec((None, HEADS, DK, DV), lambda i, j: (i, 0, 0, 0))] + cspec,
        out_shape=[jax.ShapeDtypeStruct((b, l, D_MODEL), F32),
                   jax.ShapeDtypeStruct((b, HEADS, DK, DV), F32)] + cshape,
        scratch_shapes=[pltpu.VMEM((t, V_W), BF16), pltpu.VMEM((t, SG_W), BF16), pltpu.VMEM((t, X_W), BF16)],
        compiler_params=_params(("arbitrary", "arbitrary")),
        name="prompt_mixer",
    )(dec["gc"], x, np.concatenate(tabs, axis=1), mk_b, mv_b, *consts, *cast_weights)


def _row_head(n):
    return (lax.broadcasted_iota(jnp.int32, (n, 1), 0) >> 2) & (HEADS - 1)


def _state_matmuls(gc_ref, qe_ref, kre_ref, kde_ref, ve_ref, xqe_ref, s_ref, mk_ref, dmat_ref, so_ref):
    rows_n = qe_ref.shape[0]
    per = HEADS * EXP
    batches = range(rows_n // per)
    rows = [slice(b * per, (b + 1) * per) for b in batches]
    q, ve, xq = qe_ref[...], ve_ref[...], xqe_ref[...]
    sc = _mm_nt(q, kre_ref[...]) * dmat_ref[...]
    kd = kde_ref[...]
    rowb = lax.broadcasted_iota(jnp.int32, (rows_n, 1), 0) >> 4
    cross = jnp.concatenate(
        [jnp.dot(q[rows[b]], s_ref[b].astype(BF16), preferred_element_type=F32) for b in batches], axis=0)
    upd = [_mm_tn(kd, jnp.where(rowb == b, ve, jnp.zeros_like(ve))) for b in batches]
    for b in batches:
        for hd in range(HEADS):
            hr = slice(hd * DK, (hd + 1) * DK)
            so_ref[b, hr, :] = gc_ref[hd] * s_ref[b, hr, :] + upd[b][hr]
    sx = jnp.concatenate([_mm_nt(xq[rows[b]], mk_ref[b]) for b in batches], axis=0) * np.float32(DK ** -0.5)
    return sc, cross, sx


def _state_outputs(sc, cross, sx, ve_ref, ge_ref, mv_ref, qdec_ref, gn_ref, oret_ref, ox_ref):
    rows_n = ve_ref.shape[0]
    per = HEADS * EXP
    batches = range(rows_n // per)
    rows = [slice(b * per, (b + 1) * per) for b in batches]
    inner = jnp.dot(sc.astype(BF16), ve_ref[...], preferred_element_type=F32)
    hh = _row_head(rows_n)
    own_col = (lax.broadcasted_iota(jnp.int32, (rows_n, MEM_LEN * HEADS), 1) & (HEADS - 1)) == hh
    p = _softmax(jnp.where(own_col, sx, np.float32(-1e30))).astype(BF16)
    oxs = jnp.concatenate([_mm(p[rows[b]], mv_ref[b]) for b in batches], axis=0)
    o = inner + cross * qdec_ref[...]
    og = ge_ref[...] * (_stdnorm(o) * gn_ref[...])
    for hd in range(HEADS):
        oret_ref[:, hd * DV:(hd + 1) * DV] = jnp.where(hh == hd, og, 0.0).astype(BF16)
        ox_ref[:, hd * DK:(hd + 1) * DK] = jnp.where(hh == hd, oxs, 0.0).astype(BF16)


def _ffn_kernel(gc_ref, x_ref, ng_ref, wup_ref, cw_ref, cb_ref, wdn_ref, nf_ref,
                qe_ref, kre_ref, kde_ref, ve_ref, ge_ref, xqe_ref, s_ref, mk_ref, mv_ref, dmat_ref, qdec_ref, gn_ref,
                y_ref, tail_ref, oret_ref, ox_ref, so_ref, zs_ref):
    tile = x_ref.shape[0]
    first = pl.program_id(1) == 0

    @pl.when(first)
    def _():
        zs_ref[0:8, :] = jnp.zeros((8, 2 * D_FF), F32)

    @pl.when(jnp.logical_not(first))
    def _():
        zs_ref[0:8, :] = zs_ref[tile:tile + 8, :]

    sc, cross, sx = _state_matmuls(gc_ref, qe_ref, kre_ref, kde_ref, ve_ref, xqe_ref, s_ref, mk_ref, dmat_ref,
                                   so_ref)
    x = x_ref[...]
    zs_ref[8:8 + tile, :] = _mm(_rms(x, ng_ref[...]), wup_ref[...])
    _state_outputs(sc, cross, sx, ve_ref, ge_ref, mv_ref, qdec_ref, gn_ref, oret_ref, ox_ref)
    ca, cb = slice(0, D_FF), slice(D_FF, 2 * D_FF)
    gate = _conv_gate(zs_ref, tile, ca, cb, cw_ref[:, ca], cb_ref[:, ca], cw_ref[:, cb], cb_ref[:, cb])
    gate = gate.astype(BF16)
    for rows in (slice(0, tile // 2), slice(tile // 2, tile)):
        y = x[rows] + jnp.dot(gate[rows], wdn_ref[...], preferred_element_type=F32)
        y_ref[rows, :] = _rms(y, nf_ref[...])
    tail_ref[...] = zs_ref[tile + 6:tile + 8, :]


def _prompt_ffn(x, p, pre, state, mk, mv, dec):
    b, l, _ = x.shape
    t = PROMPT_TILE
    nt = l // t
    qe, kre, kde, ve, ge, xqe = pre
    nb = state.shape[0]
    bb = nb // (b * nt)
    rb = bb * HEADS * EXP
    ne = qe.shape[0]
    tok = pl.BlockSpec((None, t, D_MODEL), lambda i, j: (i, j, 0))
    rowblk = lambda w: pl.BlockSpec((rb, w), lambda i, j: (i * nt + j, 0))
    batblk = lambda a, c: pl.BlockSpec((bb, a, c), lambda i, j: (i * nt + j, 0, 0))
    consts = [p["norm_ffn_g"], p["w_up"], p["conv_w"], p["conv_b"], p["w_down"], p["norm_final_g"]]
    return pl.pallas_call(
        _ffn_kernel,
        grid=(b, nt),
        in_specs=[_smem_spec(), tok] + [_const_spec(c.shape) for c in consts]
        + [rowblk(QK_W), rowblk(QK_W), rowblk(QK_W), rowblk(DV), rowblk(DV), rowblk(DK),
           batblk(HEADS * DK, DV), batblk(MEM_LEN * HEADS, DK), batblk(MEM_LEN * HEADS, DK),
           _const_spec((rb, rb)), _const_spec((rb, DV)), _const_spec((rb, DV))],
        out_specs=[tok, pl.BlockSpec((None, 2, 2 * D_FF), lambda i, j: (i, 0, 0)),
                   rowblk(V_W), rowblk(X_W), batblk(HEADS * DK, DV)],
        out_shape=[jax.ShapeDtypeStruct((b, l, D_MODEL), F32), jax.ShapeDtypeStruct((b, 2, 2 * D_FF), F32),
                   jax.ShapeDtypeStruct((ne, V_W), BF16), jax.ShapeDtypeStruct((ne, X_W), BF16),
                   jax.ShapeDtypeStruct(state.shape, F32)],
        scratch_shapes=[pltpu.VMEM((t + 8, 2 * D_FF), F32)],
        compiler_params=_params(("arbitrary", "arbitrary")),
        name="prompt_ffn",
    )(dec["gc"], x, *consts, qe, kre, kde, ve, ge, xqe, state, mk, mv, dec["dmat"], dec["qdec"], dec["gn"])


def _sample_pre_kernel(ws_ref, bs_ref, x_ref, cq_ref, sq_ref, ck_ref, sk_ref, kdec_ref, ng_ref, win_ref, bg_ref,
                       lng_ref, wr_ref, wsg_ref, wx_ref, wo_ref,
                       qe_ref, kre_ref, kde_ref, ve_ref, ge_ref, xqe_ref, osg_ref, vrows_ref, gates_ref,
                       wr_o_ref, wsg_o_ref, wx_o_ref, wo_o_ref, h_ref, vn_ref):
    n = x_ref.shape[0]
    nb = n // EXP
    ne = qe_ref.shape[0]
    i = pl.program_id(0)

    @pl.when(i == 0)
    def _():
        h_ref[...] = _rms(x_ref[...], ng_ref[...]).astype(BF16)
        vn_ref[...] = jnp.zeros_like(vn_ref)

    r = lax.broadcasted_iota(jnp.int32, (ne, n), 0)
    c = lax.broadcasted_iota(jnp.int32, (ne, n), 1)
    rep = jnp.where(c == (r & (EXP - 1)) * nb + i * (ne // (HEADS * EXP)) + (r >> 4), 1.0, 0.0).astype(BF16)
    he = jnp.dot(rep, h_ref[...], preferred_element_type=F32).astype(BF16)
    hh = _row_head(ne)
    h = h_ref[pl.ds(pl.multiple_of(i * nb, nb), nb), :]

    def proj(hm, a, b):
        return jnp.dot(hm, win_ref[:, a:b], preferred_element_type=F32)

    q, k = proj(he, C_Q, C_K), proj(he, C_K, C_V)
    xq = proj(he, C_XQ, C_GT)
    cq, sq, ck, sk, kdec = cq_ref[...], sq_ref[...], ck_ref[...], sk_ref[...], kdec_ref[...]
    for hd in range(HEADS):
        cols = slice(hd * DK, (hd + 1) * DK)
        own = hh == hd
        kr = _rope(k[:, cols], ck, sk)
        qe_ref[:, cols] = jnp.where(own, _rope(q[:, cols], cq, sq), 0.0).astype(BF16)
        kre_ref[:, cols] = jnp.where(own, kr, 0.0).astype(BF16)
        kde_ref[:, cols] = jnp.where(own, kr * kdec, 0.0).astype(BF16)
    v = proj(he, C_V, C_G)
    g = proj(he, C_G, C_SU)
    ve = jnp.zeros((ne, DV), F32)
    ge = jnp.zeros((ne, DV), F32)
    xqc = jnp.zeros((ne, DK), F32)
    for hd in range(HEADS):
        cols = slice(hd * DV, (hd + 1) * DV)
        own = hh == hd
        ve = jnp.where(own, v[:, cols], ve)
        ge = jnp.where(own, g[:, cols], ge)
        xqc = jnp.where(own, xq[:, hd * DK:(hd + 1) * DK], xqc)
    xqe_ref[...] = xqc.astype(BF16)
    ve_ref[...] = ve.astype(BF16)
    ge_ref[...] = _silu(ge)

    u = _gelu(proj(h, C_SU, C_SV))
    vn = _stdnorm(_gelu(proj(h, C_SV, C_XQ))) * lng_ref[...]
    vrows_ref[...] = vn
    vn_ref[i] = vn
    for g in range(HEADS):
        cols = slice(g * DK, (g + 1) * DK)
        mixed = jnp.full((nb, DK), bs_ref[g * EXP + i], F32)
        for s in range(EXP):
            w = jnp.where(s <= i, ws_ref[(g * EXP + i) * EXP + s], 0.0)
            mixed = mixed + w * vn_ref[s, :, cols]
        osg_ref[:, cols] = (u[:, cols] * mixed).astype(BF16)
    gates_ref[...] = _sigmoid(proj(h, C_GT, C_END) + bg_ref[...])
    _cast_blocks(((wr_ref, wr_o_ref), (wsg_ref, wsg_o_ref), (wx_ref, wx_o_ref), (wo_ref, wo_o_ref)))


def _sample_pre(x, tabs, kdec_rows, p, ws4, bs4, cast_weights):
    n = x.shape[0]
    cspec, cshape = _cast_specs(cast_weights, EXP, lambda i: i)
    t = n // EXP
    te = (t // EXP) * HEADS * EXP
    consts = [p["norm_mix_g"], p["w_in"], p["b_gate"], p["sg_ln_g"]]
    rowblk = lambda rows, w: pl.BlockSpec((rows, w), lambda i: (i, 0))
    outs = [(te, QK_W, BF16), (te, QK_W, BF16), (te, QK_W, BF16), (te, DV, BF16), (te, DV, F32),
            (te, DK, BF16), (t, SG_W, BF16), (t, SG_W, F32), (t, 3 * D_MODEL, F32)]
    return pl.pallas_call(
        _sample_pre_kernel,
        grid=(EXP,),
        in_specs=[_smem_spec(), _smem_spec(), _const_spec(x.shape)] + [_const_spec((te, DK))] * 5
        + [_const_spec(c.shape) for c in consts] + cspec,
        out_specs=[rowblk(r, w) for r, w, _ in outs] + cspec,
        out_shape=[jax.ShapeDtypeStruct((r * EXP, w), d) for r, w, d in outs] + cshape,
        scratch_shapes=[pltpu.VMEM((n, D_MODEL), BF16), pltpu.VMEM((EXP, t, SG_W), F32)],
        compiler_params=_params(("arbitrary",)),
        name="sample_pre",
    )(ws4, bs4, x, *tabs, kdec_rows, *consts, *cast_weights)


def _sample_merge_kernel(x_ref, orete_ref, oxe_ref, osg_ref, gates_ref, wr_ref, wsg_ref, wx_ref, wo_ref, xo_ref):
    n = x_ref.shape[0]
    ne = n * EXP
    r = lax.broadcasted_iota(jnp.int32, (n, ne), 0)
    c = lax.broadcasted_iota(jnp.int32, (n, ne), 1)
    col = jnp.where(r == (c & (EXP - 1)) * (n // EXP) + (c >> 4), 1.0, 0.0).astype(BF16)
    oret = jnp.dot(col, orete_ref[...], preferred_element_type=F32)
    ox = jnp.dot(col, oxe_ref[...], preferred_element_type=F32)
    gates = gates_ref[...]
    merged = (gates[:, :D_MODEL] * _mm(oret, wr_ref[...])
              + gates[:, D_MODEL:2 * D_MODEL] * jnp.dot(osg_ref[...], wsg_ref[...], preferred_element_type=F32)
              + gates[:, 2 * D_MODEL:] * _mm(ox, wx_ref[...]))
    xo_ref[...] = x_ref[...] + _mm(merged, wo_ref[...])


def _sample_merge(x, orete, oxe, osg, gates, p):
    ins = [x, orete, oxe, osg, gates, p["w_br_ret"], p["w_br_sg"], p["w_br_x"], p["w_o"]]
    return pl.pallas_call(
        _sample_merge_kernel,
        grid=(1,),
        in_specs=[_const_spec(a.shape) for a in ins],
        out_specs=pl.BlockSpec(x.shape, lambda i: (0, 0)),
        out_shape=jax.ShapeDtypeStruct(x.shape, F32),
        compiler_params=_params(("arbitrary",)),
        name="sample_merge",
    )(*ins)


def _sample_ffn_kernel(x_ref, ng_ref, wa_ref, wb_ref, cwa_ref, cwb_ref, cba_ref, cbb_ref, s0a_ref, s0b_ref,
                       s1a_ref, s1b_ref, wdn_ref, nf_ref,
                       y_ref, c2a_ref, c2b_ref, c3a_ref, c3b_ref, h_ref, acc_ref):
    nb = x_ref.shape[0] // EXP
    j = pl.program_id(0)

    @pl.when(j == 0)
    def _():
        h_ref[...] = _rms(x_ref[...], ng_ref[...]).astype(BF16)
        acc_ref[...] = jnp.zeros_like(acc_ref)

    h = h_ref[...]

    def conv(w_ref, cw_ref, cb_ref, s0_ref, s1_ref, c2_ref, c3_ref):
        z = jnp.dot(h, w_ref[...], preferred_element_type=F32)
        zp = [s0_ref[...], s1_ref[...]] + [z[l * nb:(l + 1) * nb] for l in range(EXP)]
        c2_ref[...] = zp[EXP]
        c3_ref[...] = zp[EXP + 1]
        cw, cb = cw_ref[...], cb_ref[...]
        return jnp.concatenate([cb + cw[0:1] * zp[l] + cw[1:2] * zp[l + 1] + cw[2:3] * zp[l + 2]
                                for l in range(EXP)], axis=0)

    a = conv(wa_ref, cwa_ref, cba_ref, s0a_ref, s1a_ref, c2a_ref, c3a_ref)
    b = conv(wb_ref, cwb_ref, cbb_ref, s0b_ref, s1b_ref, c2b_ref, c3b_ref)
    acc_ref[...] += _mm(_gelu(a) * b, wdn_ref[...])

    @pl.when(j == pl.num_programs(0) - 1)
    def _():
        y_ref[...] = _rms(x_ref[...] + acc_ref[...], nf_ref[...])


def _sample_ffn(x, sc, p):
    n = x.shape[0]
    nb = n // EXP
    cw = FFN_CW
    nch = D_FF // cw
    ca = lambda rows: pl.BlockSpec((rows, cw), lambda j: (0, j))
    cb = lambda rows: pl.BlockSpec((rows, cw), lambda j: (0, nch + j))
    sa = lambda k: pl.BlockSpec((None, nb, cw), lambda j: (k, 0, j))
    sb = lambda k: pl.BlockSpec((None, nb, cw), lambda j: (k, 0, nch + j))
    full = pl.BlockSpec((n, D_MODEL), lambda j: (0, 0))
    vec = pl.BlockSpec((1, D_MODEL), lambda j: (0, 0))
    return pl.pallas_call(
        _sample_ffn_kernel,
        grid=(nch,),
        in_specs=[full, vec, ca(D_MODEL), cb(D_MODEL), ca(3), cb(3), ca(1), cb(1), sa(0), sb(0), sa(1), sb(1),
                  pl.BlockSpec((cw, D_MODEL), lambda j: (j, 0)), vec],
        out_specs=[full] + [ca(nb)] * 4,
        out_shape=[jax.ShapeDtypeStruct((n, D_MODEL), F32)] + [jax.ShapeDtypeStruct((nb, D_FF), F32)] * 4,
        scratch_shapes=[pltpu.VMEM((n, D_MODEL), BF16), pltpu.VMEM((n, D_MODEL), F32)],
        compiler_params=_params(("arbitrary",)),
        name="sample_ffn",
    )(x, p["norm_ffn_g"], p["w_up"], p["w_up"], p["conv_w"], p["conv_w"], p["conv_b"], p["conv_b"],
      sc, sc, sc, sc, p["w_down"], p["norm_final_g"])


def _rope_tables(pos, scale):
    inv = ROPE_BASE ** (-np.arange(0, DK, 2, dtype=np.float64) / DK)
    ang = np.asarray(pos, np.float64)[:, None] * inv[None, :]
    cos, sin = np.cos(ang), np.sin(ang)
    return (np.concatenate([cos, cos], -1) * scale).astype(np.float32), \
        (np.concatenate([-sin, sin], -1) * scale).astype(np.float32)


def _decay(chunk):
    lg = np.log1p(-np.exp2(-5.0 - np.arange(HEADS, dtype=np.float64)))
    n = np.arange(chunk, dtype=np.float64)
    diff = n[:, None] - n[None, :]
    dmat = np.where(diff >= 0, np.exp(np.maximum(diff, 0.0)[None] * lg[:, None, None]), 0.0)
    qdec = np.exp((n + 1.0)[None, :] * lg[:, None])
    kdec = np.exp((chunk - 1.0 - n)[None, :] * lg[:, None])
    f32 = lambda a: a.astype(np.float32)
    return f32(dmat), f32(qdec), f32(kdec), f32(np.exp(chunk * lg))


def kernel(x_prompt, x_sample, mem_prompt, state_ret, state_conv, cache_mem_k, cache_mem_v, norm_mix_g, w_in,
           b_gate, ret_gn_g, sg_ln_g, sg_ws, sg_bs, mem_norm_g, w_mem_kv, w_br_ret, w_br_sg, w_br_x, w_o,
           norm_ffn_g, w_up, conv_w, conv_b, w_down, norm_final_g):
    bp, lp, _ = x_prompt.shape
    bs, ls, _ = x_sample.shape
    assert state_ret.shape[0] == 1 and ls == EXP and lp % PROMPT_TILE == 0
    assert bs % (bp * (lp // PROMPT_TILE)) == 0
    assert bs == CHUNK
    row = lambda a: a.reshape(1, -1)
    scale = DK ** -0.5

    mk, mv, mk_b, mv_b, w_in_b = _memkv(mem_prompt, row(mem_norm_g[0]), w_mem_kv[0], [w_in[0]])
    p = dict(norm_mix_g=row(norm_mix_g[0]), w_in=w_in_b, b_gate=row(b_gate[0]),
             gn_g=row(ret_gn_g[0]), sg_ln_g=row(sg_ln_g[0]),
             norm_ffn_g=row(norm_ffn_g[0]), conv_w=conv_w[0], conv_b=row(conv_b[0]),
             norm_final_g=row(norm_final_g))

    n = bs * ls
    per = HEADS * EXP
    te = (bs // EXP) * per
    pos_s = PAST_LEN + (np.arange(te) & (ls - 1))
    tabs_s = (*_rope_tables(pos_s, 1.0), *_rope_tables(pos_s, scale))
    dmat4, qdec4, kdec4, gc4 = _decay(ls)
    kdec_rows = np.ascontiguousarray(np.broadcast_to(np.tile(kdec4.reshape(per), te // per)[:, None], (te, DK)))
    xs = jnp.swapaxes(x_sample, 0, 1).reshape(n, D_MODEL)
    pre = _sample_pre(xs, tabs_s, kdec_rows, p, sg_ws[0][:, :ls, :ls].reshape(-1), sg_bs[0][:, :ls].reshape(-1),
                      [w_br_ret[0], w_br_sg[0], w_br_x[0], w_o[0]])
    qe, kre, kde, ve, ge, xqe, osg, vrows, gates, w_br_ret_b, w_br_sg_b, w_br_x_b, w_o_b = pre
    p = dict(p, w_br_ret=w_br_ret_b, w_br_sg=w_br_sg_b, w_br_x=w_br_x_b, w_o=w_o_b)

    pos_p = np.arange(lp)
    tabs_p = (*_rope_tables(pos_p, 1.0), *_rope_tables(pos_p, scale))
    dmat, qdec, kdec, gc = _decay(CHUNK)
    bcast = lambda a: np.ascontiguousarray(np.broadcast_to(a[:, :, None], (HEADS, CHUNK, DK)))
    dec_p = dict(dmat=dmat, qdec=bcast(qdec), kdec=bcast(kdec), gc=gc)
    pp = dict(p, sg_ws=sg_ws[0], sg_bias=jnp.repeat(sg_bs[0].T, DK, axis=1))
    x_mid, s_prompt, w_up_b, w_down_b = _prompt_mixer(x_prompt, tabs_p, mk_b, mv_b, dec_p, pp,
                                                      [w_up[0], w_down[0]])
    p = dict(p, w_up=w_up_b, w_down=w_down_b)

    sbb = bs // (bp * (lp // PROMPT_TILE))
    rb = sbb * per
    blk16 = np.einsum("hk,hls->hlks", np.eye(HEADS, dtype=np.float32), dmat4).reshape(per, per)
    dec_s = dict(gc=gc4,
                 dmat=np.kron(np.eye(sbb, dtype=np.float32), blk16),
                 qdec=np.ascontiguousarray(np.broadcast_to(np.tile(qdec4.reshape(per), sbb)[:, None], (rb, DV))),
                 gn=jnp.tile(jnp.repeat(ret_gn_g[0], ls, axis=0), (sbb, 1)))
    y_prompt, tail, orete, oxe, s_sample = _prompt_ffn(
        x_mid, p, (qe, kre, kde, ve, ge, xqe), state_ret[0].reshape(bs, HEADS * DK, DV),
        cache_mem_k[0].reshape(bs, MEM_LEN * HEADS, DK), cache_mem_v[0].reshape(bs, MEM_LEN * HEADS, DK), dec_s)
    xs_mid = _sample_merge(xs, orete, oxe, osg, gates, p)
    y_sample, c2a, c2b, c3a, c3b = _sample_ffn(xs_mid, jnp.swapaxes(state_conv[0], 0, 1), p)
    conv_s = jnp.stack([jnp.concatenate([c2a, c2b], -1), jnp.concatenate([c3a, c3b], -1)], axis=1)
    unpos = lambda a: jnp.swapaxes(a.reshape(ls, bs, a.shape[-1]), 0, 1)

    return (y_prompt, unpos(y_sample),
            s_prompt[None], tail[None],
            mk.reshape(1, bp, MEM_LEN, HEADS, DK), mv.reshape(1, bp, MEM_LEN, HEADS, DK),
            s_sample.reshape(1, bs, HEADS, DK, DV), conv_s[None],
            unpos(vrows)[None])
```

```python
import numpy as np
import jax
import jax.numpy as jnp
from jax import lax
from jax.experimental import pallas as pl
from jax.experimental.pallas import tpu as pltpu

F32 = jnp.float32
BF16 = jnp.bfloat16

D_MODEL = 1024
HEADS = 4
DK = 128
DV = 256
QK_W = HEADS * DK
V_W = HEADS * DV
SG_W = 512
X_W = 512
MEM_LEN = 256
D_FF = 2816
CHUNK = 128
ROPE_BASE = 10000.0
EPS = 1e-6
PAST_LEN = 16384

C_Q, C_K, C_V, C_G, C_SU, C_SV, C_XQ, C_GT, C_END = 0, 512, 1024, 2048, 3072, 3584, 4096, 4608, 7680

PROMPT_TILE = 512
EXP = HEADS
FFN_CW = 1408
VMEM_BYTES_V7X = 64 * 1024 * 1024
VMEM_LIMIT = VMEM_BYTES_V7X - 3 * 1024 * 1024


def _rms(x, g):
    return x * lax.rsqrt(jnp.mean(x * x, axis=-1, keepdims=True) + EPS) * g


def _stdnorm(x):
    mu = jnp.mean(x, axis=-1, keepdims=True)
    xc = x - mu
    var = jnp.mean(xc * xc, axis=-1, keepdims=True)
    return xc * lax.rsqrt(var + EPS)


_GELU_C0 = np.float32(np.sqrt(2.0 / np.pi))
_GELU_C1 = np.float32(np.sqrt(2.0 / np.pi) * 0.044715)


def _gelu_tanh(x):
    return jnp.tanh(x * (_GELU_C0 + _GELU_C1 * (x * x)))


def _gelu(x):
    hx = 0.5 * x
    return hx + hx * _gelu_tanh(x)


def _sigmoid(x):
    return 0.5 + 0.5 * jnp.tanh(0.5 * x)


def _silu(x):
    hx = 0.5 * x
    return hx + hx * jnp.tanh(hx)


def _softmax(s):
    e = jnp.exp(s - jnp.max(s, axis=-1, keepdims=True))
    return e * (1.0 / jnp.sum(e, axis=-1, keepdims=True))


def _mm(a, b):
    return jnp.dot(a.astype(BF16), b.astype(BF16), preferred_element_type=F32)


def _mm_nt(a, b):
    return lax.dot_general(a.astype(BF16), b.astype(BF16), (((1,), (1,)), ((), ())),
                           preferred_element_type=F32)


def _mm_tn(a, b):
    return lax.dot_general(a.astype(BF16), b.astype(BF16), (((0,), (0,)), ((), ())),
                           preferred_element_type=F32)


def _rope(x, cos, sin):
    return x * cos + pltpu.roll(x, DK // 2, 1) * sin


def _tril(w):
    r = lax.broadcasted_iota(jnp.int32, w.shape, 0)
    c = lax.broadcasted_iota(jnp.int32, w.shape, 1)
    return jnp.where(r >= c, w, 0.0)


def _conv_gate(zs_ref, n, cols_a, cols_b, cw_a, cb_a, cw_b, cb_b):
    def conv(cols, cw, cb):
        zz = zs_ref[0:8 + n, cols]
        blocks = zz.reshape(n // 8 + 1, 8, zz.shape[-1])
        prev, cur = blocks[:-1], blocks[1:]
        sub = lax.broadcasted_iota(jnp.int32, (1, 8, 1), 1)
        z1 = pltpu.roll(jnp.where(sub < 7, cur, prev), 1, 1).reshape(n, zz.shape[-1])
        z2 = pltpu.roll(jnp.where(sub < 6, cur, prev), 2, 1).reshape(n, zz.shape[-1])
        return cb + cw[0:1] * z2 + cw[1:2] * z1 + cw[2:3] * zz[8:]
    a = conv(cols_a, cw_a, cb_a)
    return (a + a * _gelu_tanh(a)) * conv(cols_b, 0.5 * cw_b, 0.5 * cb_b)


def _const_spec(shape):
    nd = len(shape)
    return pl.BlockSpec(shape, lambda *_: (0,) * nd, pipeline_mode=pl.Buffered(1))


def _smem_spec():
    return pl.BlockSpec(memory_space=pltpu.SMEM)


def _params(sem):
    return pltpu.CompilerParams(dimension_semantics=sem, vmem_limit_bytes=VMEM_LIMIT)


def _cast_blocks(refs):
    for src, dst in refs:
        dst[...] = src[...].astype(BF16)


def _cast_specs(weights, steps, flat_step):
    specs, shapes = [], []
    for w in weights:
        rows, cols = w.shape
        nblk = max(n for n in range(1, steps + 1) if steps % n == 0 and rows % (16 * n) == 0)
        specs.append(pl.BlockSpec((rows // nblk, cols), lambda *g, r=steps // nblk: (flat_step(*g) // r, 0)))
        shapes.append(jax.ShapeDtypeStruct(w.shape, BF16))
    return specs, shapes


def _memkv_kernel(mem_ref, g_ref, w_ref, *refs):
    ncast = (len(refs) - 4) // 2
    casts, (k_ref, v_ref, kb_ref, vb_ref) = refs[:ncast], refs[ncast:ncast + 4]
    kv = _mm(_rms(mem_ref[...], g_ref[...]), w_ref[...])
    k, v = kv[:, :X_W], kv[:, X_W:]
    for hd in range(HEADS):
        cols = slice(hd * DK, (hd + 1) * DK)
        k_ref[pl.ds(hd, MEM_LEN, stride=HEADS), :] = k[:, cols]
        v_ref[pl.ds(hd, MEM_LEN, stride=HEADS), :] = v[:, cols]
    kb_ref[...] = k.astype(BF16)
    vb_ref[...] = v.astype(BF16)
    _cast_blocks(zip(casts, refs[ncast + 4:]))


def _memkv(mem, g, w, cast_weights):
    b = mem.shape[0]
    blk = pl.BlockSpec((None, MEM_LEN, X_W), lambda i: (i, 0, 0))
    flat = pl.BlockSpec((None, MEM_LEN * HEADS, DK), lambda i: (i, 0, 0))
    cspec, cshape = _cast_specs(cast_weights, b, lambda i: i)
    return pl.pallas_call(
        _memkv_kernel,
        grid=(b,),
        in_specs=[pl.BlockSpec((None, MEM_LEN, D_MODEL), lambda i: (i, 0, 0)),
                  _const_spec((1, D_MODEL)), _const_spec((D_MODEL, 2 * X_W))] + cspec,
        out_specs=[flat, flat, blk, blk] + cspec,
        out_shape=[jax.ShapeDtypeStruct((b, MEM_LEN * HEADS, DK), F32)] * 2
        + [jax.ShapeDtypeStruct((b, MEM_LEN, X_W), BF16)] * 2 + cshape,
        compiler_params=_params(("arbitrary",)),
        name="mem_kv",
    )(mem, g, w, *cast_weights)


def _mixer_kernel(gc_ref, x_ref, tab_ref, mk_ref, mv_ref, ng_ref, win_ref, bg_ref,
                  gn_ref, lng_ref, ws_ref, sgb_ref, dmat_ref, qdec_ref, kdec_ref,
                  wr_ref, wsg_ref, wx_ref, wo_ref, wup_ref, wdn_ref,
                  xo_ref, s_ref, wup_o_ref, wdn_o_ref, oret_ref, osg_ref, ox_ref):
    tile = x_ref.shape[0]

    @pl.when(pl.program_id(1) == 0)
    def _():
        s_ref[...] = jnp.zeros_like(s_ref)

    x = x_ref[...]
    h = _rms(x, ng_ref[...]).astype(BF16)

    def proj(a, b):
        return jnp.dot(h, win_ref[:, a:b], preferred_element_type=F32)

    heads, chunks = range(HEADS), range(tile // CHUNK)
    rows = [slice(c * CHUNK, (c + 1) * CHUNK) for c in chunks]
    kcols = [slice(hd * DK, (hd + 1) * DK) for hd in heads]
    vcols = [slice(hd * DV, (hd + 1) * DV) for hd in heads]

    suv = proj(C_SU, C_XQ)
    su, sv = suv[:, :SG_W], suv[:, SG_W:]
    qk = proj(C_Q, C_V)
    q, k = qk[:, :QK_W], qk[:, QK_W:]
    u = _gelu(su)
    vn = (_stdnorm(_gelu(sv)) * lng_ref[...]).astype(BF16)
    wsg = [_tril(ws_ref[g]).astype(BF16) for g in heads]
    vb = proj(C_V, C_G).astype(BF16)
    mixed = [[jnp.dot(wsg[g], vn[rows[c], kcols[g]], preferred_element_type=F32) + sgb_ref[:, kcols[g]]
              for c in chunks] for g in heads]
    gsil = _silu(proj(C_G, C_SU))
    for g in heads:
        osg_ref[:, kcols[g]] = (u[:, kcols[g]] * jnp.concatenate(mixed[g], axis=0)).astype(BF16)

    cq, sq, ck, sk = (tab_ref[:, i * DK:(i + 1) * DK] for i in range(4))
    qr = [_rope(q[:, kcols[hd]], cq, sq) for hd in heads]
    kr = [_rope(k[:, kcols[hd]], ck, sk) for hd in heads]
    sc = [[_mm_nt(qr[hd][rows[c]], kr[hd][rows[c]]) for c in chunks] for hd in heads]
    upd = [[_mm_tn(kr[hd][rows[c]] * kdec_ref[hd], vb[rows[c], vcols[hd]]) for c in chunks] for hd in heads]
    xq = proj(C_XQ, C_GT)
    gmid = C_GT + (C_END - C_GT) // 2
    gt0 = proj(C_GT, gmid)
    states = []
    for hd in heads:
        st, before = s_ref[hd], []
        for c in chunks:
            before.append(st.astype(BF16))
            st = gc_ref[hd] * st + upd[hd][c]
        s_ref[hd] = st
        states.append(before)
    o = [[jnp.dot(jnp.concatenate([(sc[hd][c] * dmat_ref[hd]).astype(BF16),
                                   (qr[hd][rows[c]] * qdec_ref[hd]).astype(BF16)], axis=1),
                  jnp.concatenate([vb[rows[c], vcols[hd]], states[hd][c]], axis=0),
                  preferred_element_type=F32)
          for c in chunks] for hd in heads]
    sx = [_mm_nt(xq[:, kcols[hd]], mk_ref[:, kcols[hd]]) * np.float32(DK ** -0.5) for hd in heads]
    for hd in heads:
        on = _stdnorm(jnp.concatenate(o[hd], axis=0)) * gn_ref[:, vcols[hd]]
        oret_ref[:, vcols[hd]] = (gsil[:, vcols[hd]] * on).astype(BF16)
    gt1 = proj(gmid, C_END)
    px = [_softmax(sx[hd]) for hd in heads]
    for hd in heads:
        ox_ref[:, kcols[hd]] = _mm(px[hd], mv_ref[:, kcols[hd]]).astype(BF16)

    gates = _sigmoid(jnp.concatenate([gt0, gt1], axis=1) + bg_ref[...])
    merged = (gates[:, :D_MODEL] * jnp.dot(oret_ref[...], wr_ref[...], preferred_element_type=F32)
              + gates[:, D_MODEL:2 * D_MODEL] * jnp.dot(osg_ref[...], wsg_ref[...], preferred_element_type=F32)
              + gates[:, 2 * D_MODEL:] * jnp.dot(ox_ref[...], wx_ref[...], preferred_element_type=F32))
    xo_ref[...] = x + _mm(merged, wo_ref[...])
    _cast_blocks(((wup_ref, wup_o_ref), (wdn_ref, wdn_o_ref)))


def _prompt_mixer(x, tabs, mk_b, mv_b, dec, p, cast_weights):
    b, l, _ = x.shape
    t = PROMPT_TILE
    nt = l // t
    cspec, cshape = _cast_specs(cast_weights, b * nt, lambda i, j: i * nt + j)
    tok = lambda w: pl.BlockSpec((None, t, w), lambda i, j: (i, j, 0))
    tab = pl.BlockSpec((t, 4 * DK), lambda i, j: (j, 0))
    mem = pl.BlockSpec((None, MEM_LEN, X_W), lambda i, j: (i, 0, 0))
    consts = [p["norm_mix_g"], p["w_in"], p["b_gate"], p["gn_g"], p["sg_ln_g"], p["sg_ws"], p["sg_bias"],
              dec["dmat"], dec["qdec"], dec["kdec"], p["w_br_ret"], p["w_br_sg"], p["w_br_x"], p["w_o"]]
    return pl.pallas_call(
        _mixer_kernel,
        grid=(b, l // t),
        in_specs=[_smem_spec(), tok(D_MODEL), tab, mem, mem]
        + [_const_spec(c.shape) for c in consts] + cspec,
        out_specs=[tok(D_MODEL), pl.BlockSpec((None, HEADS, DK, DV), lambda i, j: (i, 0, 0, 0))] + cspec,
        out_shape=[jax.ShapeDtypeStruct((b, l, D_MODEL), F32),
                   jax.ShapeDtypeStruct((b, HEADS, DK, DV), F32)] + cshape,
        scratch_shapes=[pltpu.VMEM((t, V_W), BF16), pltpu.VMEM((t, SG_W), BF16), pltpu.VMEM((t, X_W), BF16)],
        compiler_params=_params(("arbitrary", "arbitrary")),
        name="prompt_mixer",
    )(dec["gc"], x, np.concatenate(tabs, axis=1), mk_b, mv_b, *consts, *cast_weights)


def _row_head(n):
    return (lax.broadcasted_iota(jnp.int32, (n, 1), 0) >> 2) & (HEADS - 1)


def _state_matmuls(gc_ref, qe_ref, kre_ref, kde_ref, ve_ref, xqe_ref, s_ref, mk_ref, dmat_ref, so_ref):
    rows_n = qe_ref.shape[0]
    per = HEADS * EXP
    batches = range(rows_n // per)
    rows = [slice(b * per, (b + 1) * per) for b in batches]
    q, ve, xq = qe_ref[...], ve_ref[...], xqe_ref[...]
    sc = _mm_nt(q, kre_ref[...]) * dmat_ref[...]
    kd = kde_ref[...]
    rowb = lax.broadcasted_iota(jnp.int32, (rows_n, 1), 0) >> 4
    cross = jnp.concatenate(
        [jnp.dot(q[rows[b]], s_ref[b].astype(BF16), preferred_element_type=F32) for b in batches], axis=0)
    upd = [_mm_tn(kd, jnp.where(rowb == b, ve, jnp.zeros_like(ve))) for b in batches]
    for b in batches:
        for hd in range(HEADS):
            hr = slice(hd * DK, (hd + 1) * DK)
            so_ref[b, hr, :] = gc_ref[hd] * s_ref[b, hr, :] + upd[b][hr]
    sx = jnp.concatenate([_mm_nt(xq[rows[b]], mk_ref[b]) for b in batches], axis=0) * np.float32(DK ** -0.5)
    return sc, cross, sx


def _state_outputs(sc, cross, sx, ve_ref, ge_ref, mv_ref, qdec_ref, gn_ref, oret_ref, ox_ref):
    rows_n = ve_ref.shape[0]
    per = HEADS * EXP
    batches = range(rows_n // per)
    rows = [slice(b * per, (b + 1) * per) for b in batches]
    inner = jnp.dot(sc.astype(BF16), ve_ref[...], preferred_element_type=F32)
    hh = _row_head(rows_n)
    own_col = (lax.broadcasted_iota(jnp.int32, (rows_n, MEM_LEN * HEADS), 1) & (HEADS - 1)) == hh
    p = _softmax(jnp.where(own_col, sx, np.float32(-1e30))).astype(BF16)
    oxs = jnp.concatenate([_mm(p[rows[b]], mv_ref[b]) for b in batches], axis=0)
    o = inner + cross * qdec_ref[...]
    og = ge_ref[...] * (_stdnorm(o) * gn_ref[...])
    for hd in range(HEADS):
        oret_ref[:, hd * DV:(hd + 1) * DV] = jnp.where(hh == hd, og, 0.0).astype(BF16)
        ox_ref[:, hd * DK:(hd + 1) * DK] = jnp.where(hh == hd, oxs, 0.0).astype(BF16)


def _ffn_kernel(gc_ref, x_ref, ng_ref, wup_ref, cw_ref, cb_ref, wdn_ref, nf_ref,
                qe_ref, kre_ref, kde_ref, ve_ref, ge_ref, xqe_ref, s_ref, mk_ref, mv_ref, dmat_ref, qdec_ref, gn_ref,
                y_ref, tail_ref, oret_ref, ox_ref, so_ref, zs_ref):
    tile = x_ref.shape[0]
    first = pl.program_id(1) == 0

    @pl.when(first)
    def _():
        zs_ref[0:8, :] = jnp.zeros((8, 2 * D_FF), F32)

    @pl.when(jnp.logical_not(first))
    def _():
        zs_ref[0:8, :] = zs_ref[tile:tile + 8, :]

    sc, cross, sx = _state_matmuls(gc_ref, qe_ref, kre_ref, kde_ref, ve_ref, xqe_ref, s_ref, mk_ref, dmat_ref,
                                   so_ref)
    x = x_ref[...]
    zs_ref[8:8 + tile, :] = _mm(_rms(x, ng_ref[...]), wup_ref[...])
    _state_outputs(sc, cross, sx, ve_ref, ge_ref, mv_ref, qdec_ref, gn_ref, oret_ref, ox_ref)
    ca, cb = slice(0, D_FF), slice(D_FF, 2 * D_FF)
    gate = _conv_gate(zs_ref, tile, ca, cb, cw_ref[:, ca], cb_ref[:, ca], cw_ref[:, cb], cb_ref[:, cb])
    gate = gate.astype(BF16)
    for rows in (slice(0, tile // 2), slice(tile // 2, tile)):
        y = x[rows] + jnp.dot(gate[rows], wdn_ref[...], preferred_element_type=F32)
        y_ref[rows, :] = _rms(y, nf_ref[...])
    tail_ref[...] = zs_ref[tile + 6:tile + 8, :]


def _prompt_ffn(x, p, pre, state, mk, mv, dec):
    b, l, _ = x.shape
    t = PROMPT_TILE
    nt = l // t
    qe, kre, kde, ve, ge, xqe = pre
    nb = state.shape[0]
    bb = nb // (b * nt)
    rb = bb * HEADS * EXP
    ne = qe.shape[0]
    tok = pl.BlockSpec((None, t, D_MODEL), lambda i, j: (i, j, 0))
    rowblk = lambda w: pl.BlockSpec((rb, w), lambda i, j: (i * nt + j, 0))
    batblk = lambda a, c: pl.BlockSpec((bb, a, c), lambda i, j: (i * nt + j, 0, 0))
    consts = [p["norm_ffn_g"], p["w_up"], p["conv_w"], p["conv_b"], p["w_down"], p["norm_final_g"]]
    return pl.pallas_call(
        _ffn_kernel,
        grid=(b, nt),
        in_specs=[_smem_spec(), tok] + [_const_spec(c.shape) for c in consts]
        + [rowblk(QK_W), rowblk(QK_W), rowblk(QK_W), rowblk(DV), rowblk(DV), rowblk(DK),
           batblk(HEADS * DK, DV), batblk(MEM_LEN * HEADS, DK), batblk(MEM_LEN * HEADS, DK),
           _const_spec((rb, rb)), _const_spec((rb, DV)), _const_spec((rb, DV))],
        out_specs=[tok, pl.BlockSpec((None, 2, 2 * D_FF), lambda i, j: (i, 0, 0)),
                   rowblk(V_W), rowblk(X_W), batblk(HEADS * DK, DV)],
        out_shape=[jax.ShapeDtypeStruct((b, l, D_MODEL), F32), jax.ShapeDtypeStruct((b, 2, 2 * D_FF), F32),
                   jax.ShapeDtypeStruct((ne, V_W), BF16), jax.ShapeDtypeStruct((ne, X_W), BF16),
                   jax.ShapeDtypeStruct(state.shape, F32)],
        scratch_shapes=[pltpu.VMEM((t + 8, 2 * D_FF), F32)],
        compiler_params=_params(("arbitrary", "arbitrary")),
        name="prompt_ffn",
    )(dec["gc"], x, *consts, qe, kre, kde, ve, ge, xqe, state, mk, mv, dec["dmat"], dec["qdec"], dec["gn"])


def _sample_pre_kernel(ws_ref, bs_ref, x_ref, cq_ref, sq_ref, ck_ref, sk_ref, kdec_ref, ng_ref, win_ref, bg_ref,
                       lng_ref, wr_ref, wsg_ref, wx_ref, wo_ref,
                       qe_ref, kre_ref, kde_ref, ve_ref, ge_ref, xqe_ref, osg_ref, vrows_ref, gates_ref,
                       wr_o_ref, wsg_o_ref, wx_o_ref, wo_o_ref, h_ref, vn_ref):
    n = x_ref.shape[0]
    nb = n // EXP
    ne = qe_ref.shape[0]
    i = pl.program_id(0)

    @pl.when(i == 0)
    def _():
        h_ref[...] = _rms(x_ref[...], ng_ref[...]).astype(BF16)
        vn_ref[...] = jnp.zeros_like(vn_ref)

    r = lax.broadcasted_iota(jnp.int32, (ne, n), 0)
    c = lax.broadcasted_iota(jnp.int32, (ne, n), 1)
    rep = jnp.where(c == (r & (EXP - 1)) * nb + i * (ne // (HEADS * EXP)) + (r >> 4), 1.0, 0.0).astype(BF16)
    he = jnp.dot(rep, h_ref[...], preferred_element_type=F32).astype(BF16)
    hh = _row_head(ne)
    h = h_ref[pl.ds(pl.multiple_of(i * nb, nb), nb), :]

    def proj(hm, a, b):
        return jnp.dot(hm, win_ref[:, a:b], preferred_element_type=F32)

    q, k = proj(he, C_Q, C_K), proj(he, C_K, C_V)
    xq = proj(he, C_XQ, C_GT)
    cq, sq, ck, sk, kdec = cq_ref[...], sq_ref[...], ck_ref[...], sk_ref[...], kdec_ref[...]
    for hd in range(HEADS):
        cols = slice(hd * DK, (hd + 1) * DK)
        own = hh == hd
        kr = _rope(k[:, cols], ck, sk)
        qe_ref[:, cols] = jnp.where(own, _rope(q[:, cols], cq, sq), 0.0).astype(BF16)
        kre_ref[:, cols] = jnp.where(own, kr, 0.0).astype(BF16)
        kde_ref[:, cols] = jnp.where(own, kr * kdec, 0.0).astype(BF16)
    v = proj(he, C_V, C_G)
    g = proj(he, C_G, C_SU)
    ve = jnp.zeros((ne, DV), F32)
    ge = jnp.zeros((ne, DV), F32)
    xqc = jnp.zeros((ne, DK), F32)
    for hd in range(HEADS):
        cols = slice(hd * DV, (hd + 1) * DV)
        own = hh == hd
        ve = jnp.where(own, v[:, cols], ve)
        ge = jnp.where(own, g[:, cols], ge)
        xqc = jnp.where(own, xq[:, hd * DK:(hd + 1) * DK], xqc)
    xqe_ref[...] = xqc.astype(BF16)
    ve_ref[...] = ve.astype(BF16)
    ge_ref[...] = _silu(ge)

    u = _gelu(proj(h, C_SU, C_SV))
    vn = _stdnorm(_gelu(proj(h, C_SV, C_XQ))) * lng_ref[...]
    vrows_ref[...] = vn
    vn_ref[i] = vn
    for g in range(HEADS):
        cols = slice(g * DK, (g + 1) * DK)
        mixed = jnp.full((nb, DK), bs_ref[g * EXP + i], F32)
        for s in range(EXP):
            w = jnp.where(s <= i, ws_ref[(g * EXP + i) * EXP + s], 0.0)
            mixed = mixed + w * vn_ref[s, :, cols]
        osg_ref[:, cols] = (u[:, cols] * mixed).astype(BF16)
    gates_ref[...] = _sigmoid(proj(h, C_GT, C_END) + bg_ref[...])
    _cast_blocks(((wr_ref, wr_o_ref), (wsg_ref, wsg_o_ref), (wx_ref, wx_o_ref), (wo_ref, wo_o_ref)))


def _sample_pre(x, tabs, kdec_rows, p, ws4, bs4, cast_weights):
    n = x.shape[0]
    cspec, cshape = _cast_specs(cast_weights, EXP, lambda i: i)
    t = n // EXP
    te = (t // EXP) * HEADS * EXP
    consts = [p["norm_mix_g"], p["w_in"], p["b_gate"], p["sg_ln_g"]]
    rowblk = lambda rows, w: pl.BlockSpec((rows, w), lambda i: (i, 0))
    outs = [(te, QK_W, BF16), (te, QK_W, BF16), (te, QK_W, BF16), (te, DV, BF16), (te, DV, F32),
            (te, DK, BF16), (t, SG_W, BF16), (t, SG_W, F32), (t, 3 * D_MODEL, F32)]
    return pl.pallas_call(
        _sample_pre_kernel,
        grid=(EXP,),
        in_specs=[_smem_spec(), _smem_spec(), _const_spec(x.shape)] + [_const_spec((te, DK))] * 5
        + [_const_spec(c.shape) for c in consts] + cspec,
        out_specs=[rowblk(r, w) for r, w, _ in outs] + cspec,
        out_shape=[jax.ShapeDtypeStruct((r * EXP, w), d) for r, w, d in outs] + cshape,
        scratch_shapes=[pltpu.VMEM((n, D_MODEL), BF16), pltpu.VMEM((EXP, t, SG_W), F32)],
        compiler_params=_params(("arbitrary",)),
        name="sample_pre",
    )(ws4, bs4, x, *tabs, kdec_rows, *consts, *cast_weights)


def _sample_merge_kernel(x_ref, orete_ref, oxe_ref, osg_ref, gates_ref, wr_ref, wsg_ref, wx_ref, wo_ref, xo_ref):
    ls, nb, _ = x_ref.shape
    n = ls * nb
    ne = n * EXP
    rows = lambda ref: ref[...].reshape(n, ref.shape[-1])
    r = lax.broadcasted_iota(jnp.int32, (n, ne), 0)
    c = lax.broadcasted_iota(jnp.int32, (n, ne), 1)
    col = jnp.where(r == (c & (EXP - 1)) * (n // EXP) + (c >> 4), 1.0, 0.0).astype(BF16)
    oret = jnp.dot(col, orete_ref[...], preferred_element_type=F32)
    ox = jnp.dot(col, oxe_ref[...], preferred_element_type=F32)
    gates = rows(gates_ref)
    merged = (gates[:, :D_MODEL] * _mm(oret, wr_ref[...])
              + gates[:, D_MODEL:2 * D_MODEL] * jnp.dot(rows(osg_ref), wsg_ref[...], preferred_element_type=F32)
              + gates[:, 2 * D_MODEL:] * _mm(ox, wx_ref[...]))
    xo_ref[...] = (rows(x_ref) + _mm(merged, wo_ref[...])).reshape(xo_ref.shape)


def _sample_merge(x, orete, oxe, osg, gates, p):
    n = x.shape[0]
    ls = EXP
    bs = n // ls
    steps = 4
    nb = bs // steps
    tok = lambda a: a.reshape(ls, bs, a.shape[-1])
    tokspec = lambda w: pl.BlockSpec((ls, nb, w), lambda i: (0, i, 0))
    expspec = lambda w: pl.BlockSpec((nb * HEADS * ls, w), lambda i: (i, 0))
    weights = [p["w_br_ret"], p["w_br_sg"], p["w_br_x"], p["w_o"]]
    out = pl.pallas_call(
        _sample_merge_kernel,
        grid=(steps,),
        in_specs=[tokspec(D_MODEL), expspec(V_W), expspec(X_W), tokspec(SG_W), tokspec(3 * D_MODEL)]
        + [_const_spec(w.shape) for w in weights],
        out_specs=tokspec(D_MODEL),
        out_shape=jax.ShapeDtypeStruct((ls, bs, D_MODEL), F32),
        compiler_params=_params(("arbitrary",)),
        name="sample_merge",
    )(tok(x), orete, oxe, tok(osg), tok(gates), *weights)
    return out.reshape(n, D_MODEL)


def _sample_ffn_kernel(x_ref, ng_ref, wa_ref, wb_ref, cwa_ref, cwb_ref, cba_ref, cbb_ref, s0a_ref, s0b_ref,
                       s1a_ref, s1b_ref, wdn_ref, nf_ref,
                       y_ref, c2a_ref, c2b_ref, c3a_ref, c3b_ref, h_ref, acc_ref):
    nb = x_ref.shape[0] // EXP
    j = pl.program_id(0)

    @pl.when(j == 0)
    def _():
        h_ref[...] = _rms(x_ref[...], ng_ref[...]).astype(BF16)
        acc_ref[...] = jnp.zeros_like(acc_ref)

    h = h_ref[...]

    def conv(w_ref, cw_ref, cb_ref, s0_ref, s1_ref, c2_ref, c3_ref):
        z = jnp.dot(h, w_ref[...], preferred_element_type=F32)
        zp = [s0_ref[...], s1_ref[...]] + [z[l * nb:(l + 1) * nb] for l in range(EXP)]
        c2_ref[...] = zp[EXP]
        c3_ref[...] = zp[EXP + 1]
        cw, cb = cw_ref[...], cb_ref[...]
        return jnp.concatenate([cb + cw[0:1] * zp[l] + cw[1:2] * zp[l + 1] + cw[2:3] * zp[l + 2]
                                for l in range(EXP)], axis=0)

    a = conv(wa_ref, cwa_ref, cba_ref, s0a_ref, s1a_ref, c2a_ref, c3a_ref)
    b = conv(wb_ref, cwb_ref, cbb_ref, s0b_ref, s1b_ref, c2b_ref, c3b_ref)
    acc_ref[...] += _mm(_gelu(a) * b, wdn_ref[...])

    @pl.when(j == pl.num_programs(0) - 1)
    def _():
        y_ref[...] = _rms(x_ref[...] + acc_ref[...], nf_ref[...])


def _sample_ffn(x, sc, p):
    n = x.shape[0]
    nb = n // EXP
    cw = FFN_CW
    nch = D_FF // cw
    ca = lambda rows: pl.BlockSpec((rows, cw), lambda j: (0, j))
    cb = lambda rows: pl.BlockSpec((rows, cw), lambda j: (0, nch + j))
    sa = lambda k: pl.BlockSpec((None, nb, cw), lambda j: (k, 0, j))
    sb = lambda k: pl.BlockSpec((None, nb, cw), lambda j: (k, 0, nch + j))
    full = pl.BlockSpec((n, D_MODEL), lambda j: (0, 0))
    vec = pl.BlockSpec((1, D_MODEL), lambda j: (0, 0))
    return pl.pallas_call(
        _sample_ffn_kernel,
        grid=(nch,),
        in_specs=[full, vec, ca(D_MODEL), cb(D_MODEL), ca(3), cb(3), ca(1), cb(1), sa(0), sb(0), sa(1), sb(1),
                  pl.BlockSpec((cw, D_MODEL), lambda j: (j, 0)), vec],
        out_specs=[full] + [ca(nb)] * 4,
        out_shape=[jax.ShapeDtypeStruct((n, D_MODEL), F32)] + [jax.ShapeDtypeStruct((nb, D_FF), F32)] * 4,
        scratch_shapes=[pltpu.VMEM((n, D_MODEL), BF16), pltpu.VMEM((n, D_MODEL), F32)],
        compiler_params=_params(("arbitrary",)),
        name="sample_ffn",
    )(x, p["norm_ffn_g"], p["w_up"], p["w_up"], p["conv_w"], p["conv_w"], p["conv_b"], p["conv_b"],
      sc, sc, sc, sc, p["w_down"], p["norm_final_g"])


def _rope_tables(pos, scale):
    inv = ROPE_BASE ** (-np.arange(0, DK, 2, dtype=np.float64) / DK)
    ang = np.asarray(pos, np.float64)[:, None] * inv[None, :]
    cos, sin = np.cos(ang), np.sin(ang)
    return (np.concatenate([cos, cos], -1) * scale).astype(np.float32), \
        (np.concatenate([-sin, sin], -1) * scale).astype(np.float32)


def _decay(chunk):
    lg = np.log1p(-np.exp2(-5.0 - np.arange(HEADS, dtype=np.float64)))
    n = np.arange(chunk, dtype=np.float64)
    diff = n[:, None] - n[None, :]
    dmat = np.where(diff >= 0, np.exp(np.maximum(diff, 0.0)[None] * lg[:, None, None]), 0.0)
    qdec = np.exp((n + 1.0)[None, :] * lg[:, None])
    kdec = np.exp((chunk - 1.0 - n)[None, :] * lg[:, None])
    f32 = lambda a: a.astype(np.float32)
    return f32(dmat), f32(qdec), f32(kdec), f32(np.exp(chunk * lg))


def kernel(x_prompt, x_sample, mem_prompt, state_ret, state_conv, cache_mem_k, cache_mem_v, norm_mix_g, w_in,
           b_gate, ret_gn_g, sg_ln_g, sg_ws, sg_bs, mem_norm_g, w_mem_kv, w_br_ret, w_br_sg, w_br_x, w_o,
           norm_ffn_g, w_up, conv_w, conv_b, w_down, norm_final_g):
    bp, lp, _ = x_prompt.shape
    bs, ls, _ = x_sample.shape
    assert state_ret.shape[0] == 1 and ls == EXP and lp % PROMPT_TILE == 0
    assert bs % (bp * (lp // PROMPT_TILE)) == 0
    assert bs == CHUNK
    row = lambda a: a.reshape(1, -1)
    scale = DK ** -0.5

    mk, mv, mk_b, mv_b, w_in_b = _memkv(mem_prompt, row(mem_norm_g[0]), w_mem_kv[0], [w_in[0]])
    p = dict(norm_mix_g=row(norm_mix_g[0]), w_in=w_in_b, b_gate=row(b_gate[0]),
             gn_g=row(ret_gn_g[0]), sg_ln_g=row(sg_ln_g[0]),
             norm_ffn_g=row(norm_ffn_g[0]), conv_w=conv_w[0], conv_b=row(conv_b[0]),
             norm_final_g=row(norm_final_g))

    n = bs * ls
    per = HEADS * EXP
    te = (bs // EXP) * per
    pos_s = PAST_LEN + (np.arange(te) & (ls - 1))
    tabs_s = (*_rope_tables(pos_s, 1.0), *_rope_tables(pos_s, scale))
    dmat4, qdec4, kdec4, gc4 = _decay(ls)
    kdec_rows = np.ascontiguousarray(np.broadcast_to(np.tile(kdec4.reshape(per), te // per)[:, None], (te, DK)))
    xs = jnp.swapaxes(x_sample, 0, 1).reshape(n, D_MODEL)
    pre = _sample_pre(xs, tabs_s, kdec_rows, p, sg_ws[0][:, :ls, :ls].reshape(-1), sg_bs[0][:, :ls].reshape(-1),
                      [w_br_ret[0], w_br_sg[0], w_br_x[0], w_o[0]])
    qe, kre, kde, ve, ge, xqe, osg, vrows, gates, w_br_ret_b, w_br_sg_b, w_br_x_b, w_o_b = pre
    p = dict(p, w_br_ret=w_br_ret_b, w_br_sg=w_br_sg_b, w_br_x=w_br_x_b, w_o=w_o_b)

    pos_p = np.arange(lp)
    tabs_p = (*_rope_tables(pos_p, 1.0), *_rope_tables(pos_p, scale))
    dmat, qdec, kdec, gc = _decay(CHUNK)
    bcast = lambda a: np.ascontiguousarray(np.broadcast_to(a[:, :, None], (HEADS, CHUNK, DK)))
    dec_p = dict(dmat=dmat, qdec=bcast(qdec), kdec=bcast(kdec), gc=gc)
    pp = dict(p, sg_ws=sg_ws[0], sg_bias=jnp.repeat(sg_bs[0].T, DK, axis=1))
    x_mid, s_prompt, w_up_b, w_down_b = _prompt_mixer(x_prompt, tabs_p, mk_b, mv_b, dec_p, pp,
                                                      [w_up[0], w_down[0]])
    p = dict(p, w_up=w_up_b, w_down=w_down_b)

    sbb = bs // (bp * (lp // PROMPT_TILE))
    rb = sbb * per
    blk16 = np.einsum("hk,hls->hlks", np.eye(HEADS, dtype=np.float32), dmat4).reshape(per, per)
    dec_s = dict(gc=gc4,
                 dmat=np.kron(np.eye(sbb, dtype=np.float32), blk16),
                 qdec=np.ascontiguousarray(np.broadcast_to(np.tile(qdec4.reshape(per), sbb)[:, None], (rb, DV))),
                 gn=jnp.tile(jnp.repeat(ret_gn_g[0], ls, axis=0), (sbb, 1)))
    y_prompt, tail, orete, oxe, s_sample = _prompt_ffn(
        x_mid, p, (qe, kre, kde, ve, ge, xqe), state_ret[0].reshape(bs, HEADS * DK, DV),
        cache_mem_k[0].reshape(bs, MEM_LEN * HEADS, DK), cache_mem_v[0].reshape(bs, MEM_LEN * HEADS, DK), dec_s)
    xs_mid = _sample_merge(xs, orete, oxe, osg, gates, p)
    y_sample, c2a, c2b, c3a, c3b = _sample_ffn(xs_mid, jnp.swapaxes(state_conv[0], 0, 1), p)
    conv_s = jnp.stack([jnp.concatenate([c2a, c2b], -1), jnp.concatenate([c3a, c3b], -1)], axis=1)
    unpos = lambda a: jnp.swapaxes(a.reshape(ls, bs, a.shape[-1]), 0, 1)

    return (y_prompt, unpos(y_sample),
            s_prompt[None], tail[None],
            mk.reshape(1, bp, MEM_LEN, HEADS, DK), mv.reshape(1, bp, MEM_LEN, HEADS, DK),
            s_sample.reshape(1, bs, HEADS, DK, DV), conv_s[None],
            unpos(vrows)[None])
```

```python
import numpy as np
import jax
import jax.numpy as jnp
from jax import lax
from jax.experimental import pallas as pl
from jax.experimental.pallas import tpu as pltpu

F32 = jnp.float32
BF16 = jnp.bfloat16

D_MODEL = 1024
HEADS = 4
DK = 128
DV = 256
QK_W = HEADS * DK
V_W = HEADS * DV
SG_W = 512
X_W = 512
MEM_LEN = 256
D_FF = 2816
CHUNK = 128
ROPE_BASE = 10000.0
EPS = 1e-6
PAST_LEN = 16384

C_Q, C_K, C_V, C_G, C_SU, C_SV, C_XQ, C_GT, C_END = 0, 512, 1024, 2048, 3072, 3584, 4096, 4608, 7680

PROMPT_TILE = 512
EXP = HEADS
FFN_CW = 1408
VMEM_BYTES_V7X = 64 * 1024 * 1024
VMEM_LIMIT = VMEM_BYTES_V7X - 3 * 1024 * 1024


def _rms(x, g):
    return x * lax.rsqrt(jnp.mean(x * x, axis=-1, keepdims=True) + EPS) * g


def _stdnorm(x):
    mu = jnp.mean(x, axis=-1, keepdims=True)
    xc = x - mu
    var = jnp.mean(xc * xc, axis=-1, keepdims=True)
    return xc * lax.rsqrt(var + EPS)


_GELU_C0 = np.float32(np.sqrt(2.0 / np.pi))
_GELU_C1 = np.float32(np.sqrt(2.0 / np.pi) * 0.044715)


def _gelu_tanh(x):
    return jnp.tanh(x * (_GELU_C0 + _GELU_C1 * (x * x)))


def _gelu(x):
    hx = 0.5 * x
    return hx + hx * _gelu_tanh(x)


def _sigmoid(x):
    return 0.5 + 0.5 * jnp.tanh(0.5 * x)


def _silu(x):
    hx = 0.5 * x
    return hx + hx * jnp.tanh(hx)


def _softmax(s):
    e = jnp.exp(s - jnp.max(s, axis=-1, keepdims=True))
    return e * (1.0 / jnp.sum(e, axis=-1, keepdims=True))


def _mm(a, b):
    return jnp.dot(a.astype(BF16), b.astype(BF16), preferred_element_type=F32)


def _mm_nt(a, b):
    return lax.dot_general(a.astype(BF16), b.astype(BF16), (((1,), (1,)), ((), ())),
                           preferred_element_type=F32)


def _mm_tn(a, b):
    return lax.dot_general(a.astype(BF16), b.astype(BF16), (((0,), (0,)), ((), ())),
                           preferred_element_type=F32)


def _rope(x, cos, sin):
    return x * cos + pltpu.roll(x, DK // 2, 1) * sin


def _tril(w):
    r = lax.broadcasted_iota(jnp.int32, w.shape, 0)
    c = lax.broadcasted_iota(jnp.int32, w.shape, 1)
    return jnp.where(r >= c, w, 0.0)


def _conv_gate(zs_ref, n, cols_a, cols_b, cw_a, cb_a, cw_b, cb_b):
    def conv(cols, cw, cb):
        zz = zs_ref[0:8 + n, cols]
        blocks = zz.reshape(n // 8 + 1, 8, zz.shape[-1])
        prev, cur = blocks[:-1], blocks[1:]
        sub = lax.broadcasted_iota(jnp.int32, (1, 8, 1), 1)
        z1 = pltpu.roll(jnp.where(sub < 7, cur, prev), 1, 1).reshape(n, zz.shape[-1])
        z2 = pltpu.roll(jnp.where(sub < 6, cur, prev), 2, 1).reshape(n, zz.shape[-1])
        return cb + cw[0:1] * z2 + cw[1:2] * z1 + cw[2:3] * zz[8:]
    a = conv(cols_a, cw_a, cb_a)
    return (a + a * _gelu_tanh(a)) * conv(cols_b, 0.5 * cw_b, 0.5 * cb_b)


def _const_spec(shape):
    nd = len(shape)
    return pl.BlockSpec(shape, lambda *_: (0,) * nd, pipeline_mode=pl.Buffered(1))


def _smem_spec():
    return pl.BlockSpec(memory_space=pltpu.SMEM)


def _params(sem):
    return pltpu.CompilerParams(dimension_semantics=sem, vmem_limit_bytes=VMEM_LIMIT)


def _cast_blocks(refs):
    for src, dst in refs:
        dst[...] = src[...].astype(BF16)


def _cast_specs(weights, steps, flat_step):
    specs, shapes = [], []
    for w in weights:
        rows, cols = w.shape
        nblk = max(n for n in range(1, steps + 1) if steps % n == 0 and rows % (16 * n) == 0)
        specs.append(pl.BlockSpec((rows // nblk, cols), lambda *g, r=steps // nblk: (flat_step(*g) // r, 0)))
        shapes.append(jax.ShapeDtypeStruct(w.shape, BF16))
    return specs, shapes


def _memkv_kernel(mem_ref, g_ref, w_ref, *refs):
    ncast = (len(refs) - 4) // 2
    casts, (k_ref, v_ref, kb_ref, vb_ref) = refs[:ncast], refs[ncast:ncast + 4]
    kv = _mm(_rms(mem_ref[...], g_ref[...]), w_ref[...])
    k, v = kv[:, :X_W], kv[:, X_W:]
    for hd in range(HEADS):
        cols = slice(hd * DK, (hd + 1) * DK)
        k_ref[pl.ds(hd, MEM_LEN, stride=HEADS), :] = k[:, cols]
        v_ref[pl.ds(hd, MEM_LEN, stride=HEADS), :] = v[:, cols]
    kb_ref[...] = k.astype(BF16)
    vb_ref[...] = v.astype(BF16)
    _cast_blocks(zip(casts, refs[ncast + 4:]))


def _memkv(mem, g, w, cast_weights):
    b = mem.shape[0]
    blk = pl.BlockSpec((None, MEM_LEN, X_W), lambda i: (i, 0, 0))
    flat = pl.BlockSpec((None, MEM_LEN * HEADS, DK), lambda i: (i, 0, 0))
    cspec, cshape = _cast_specs(cast_weights, b, lambda i: i)
    return pl.pallas_call(
        _memkv_kernel,
        grid=(b,),
        in_specs=[pl.BlockSpec((None, MEM_LEN, D_MODEL), lambda i: (i, 0, 0)),
                  _const_spec((1, D_MODEL)), _const_spec((D_MODEL, 2 * X_W))] + cspec,
        out_specs=[flat, flat, blk, blk] + cspec,
        out_shape=[jax.ShapeDtypeStruct((b, MEM_LEN * HEADS, DK), F32)] * 2
        + [jax.ShapeDtypeStruct((b, MEM_LEN, X_W), BF16)] * 2 + cshape,
        compiler_params=_params(("arbitrary",)),
        name="mem_kv",
    )(mem, g, w, *cast_weights)


def _mixer_kernel(gc_ref, x_ref, tab_ref, mk_ref, mv_ref, ng_ref, win_ref, bg_ref,
                  gn_ref, lng_ref, ws_ref, sgb_ref, dmat_ref, qdec_ref, kdec_ref,
                  wr_ref, wsg_ref, wx_ref, wo_ref, wup_ref, wdn_ref,
                  xo_ref, s_ref, wup_o_ref, wdn_o_ref, oret_ref, osg_ref, ox_ref):
    tile = x_ref.shape[0]

    @pl.when(pl.program_id(1) == 0)
    def _():
        s_ref[...] = jnp.zeros_like(s_ref)

    x = x_ref[...]
    h = _rms(x, ng_ref[...]).astype(BF16)

    def proj(a, b):
        return jnp.dot(h, win_ref[:, a:b], preferred_element_type=F32)

    heads, chunks = range(HEADS), range(tile // CHUNK)
    rows = [slice(c * CHUNK, (c + 1) * CHUNK) for c in chunks]
    kcols = [slice(hd * DK, (hd + 1) * DK) for hd in heads]
    vcols = [slice(hd * DV, (hd + 1) * DV) for hd in heads]

    suv = proj(C_SU, C_XQ)
    su, sv = suv[:, :SG_W], suv[:, SG_W:]
    qk = proj(C_Q, C_V)
    q, k = qk[:, :QK_W], qk[:, QK_W:]
    u = _gelu(su)
    vn = (_stdnorm(_gelu(sv)) * lng_ref[...]).astype(BF16)
    wsg = [_tril(ws_ref[g]).astype(BF16) for g in heads]
    vb = proj(C_V, C_G).astype(BF16)
    mixed = [[jnp.dot(wsg[g], vn[rows[c], kcols[g]], preferred_element_type=F32) + sgb_ref[:, kcols[g]]
              for c in chunks] for g in heads]
    gsil = _silu(proj(C_G, C_SU))
    for g in heads:
        osg_ref[:, kcols[g]] = (u[:, kcols[g]] * jnp.concatenate(mixed[g], axis=0)).astype(BF16)

    cq, sq, ck, sk = (tab_ref[:, i * DK:(i + 1) * DK] for i in range(4))
    qr = [_rope(q[:, kcols[hd]], cq, sq) for hd in heads]
    kr = [_rope(k[:, kcols[hd]], ck, sk) for hd in heads]
    sc = [[_mm_nt(qr[hd][rows[c]], kr[hd][rows[c]]) for c in chunks] for hd in heads]
    upd = [[_mm_tn(kr[hd][rows[c]] * kdec_ref[hd], vb[rows[c], vcols[hd]]) for c in chunks] for hd in heads]
    xq = proj(C_XQ, C_GT)
    gmid = C_GT + (C_END - C_GT) // 2
    gt0 = proj(C_GT, gmid)
    states = []
    for hd in heads:
        st, before = s_ref[hd], []
        for c in chunks:
            before.append(st.astype(BF16))
            st = gc_ref[hd] * st + upd[hd][c]
        s_ref[hd] = st
        states.append(before)
    o = [[jnp.dot(jnp.concatenate([(sc[hd][c] * dmat_ref[hd]).astype(BF16),
                                   (qr[hd][rows[c]] * qdec_ref[hd]).astype(BF16)], axis=1),
                  jnp.concatenate([vb[rows[c], vcols[hd]], states[hd][c]], axis=0),
                  preferred_element_type=F32)
          for c in chunks] for hd in heads]
    sx = [_mm_nt(xq[:, kcols[hd]], mk_ref[:, kcols[hd]]) * np.float32(DK ** -0.5) for hd in heads]
    for hd in heads:
        on = _stdnorm(jnp.concatenate(o[hd], axis=0)) * gn_ref[:, vcols[hd]]
        oret_ref[:, vcols[hd]] = (gsil[:, vcols[hd]] * on).astype(BF16)
    gt1 = proj(gmid, C_END)
    px = [_softmax(sx[hd]) for hd in heads]
    for hd in heads:
        ox_ref[:, kcols[hd]] = _mm(px[hd], mv_ref[:, kcols[hd]]).astype(BF16)

    gates = _sigmoid(jnp.concatenate([gt0, gt1], axis=1) + bg_ref[...])
    merged = (gates[:, :D_MODEL] * jnp.dot(oret_ref[...], wr_ref[...], preferred_element_type=F32)
              + gates[:, D_MODEL:2 * D_MODEL] * jnp.dot(osg_ref[...], wsg_ref[...], preferred_element_type=F32)
              + gates[:, 2 * D_MODEL:] * jnp.dot(ox_ref[...], wx_ref[...], preferred_element_type=F32))
    xo_ref[...] = x + _mm(merged, wo_ref[...])
    _cast_blocks(((wup_ref, wup_o_ref), (wdn_ref, wdn_o_ref)))


def _prompt_mixer(x, tabs, mk_b, mv_b, dec, p, cast_weights):
    b, l, _ = x.shape
    t = PROMPT_TILE
    nt = l // t
    cspec, cshape = _cast_specs(cast_weights, b * nt, lambda i, j: i * nt + j)
    tok = lambda w: pl.BlockSpec((None, t, w), lambda i, j: (i, j, 0))
    tab = pl.BlockSpec((t, 4 * DK), lambda i, j: (j, 0))
    mem = pl.BlockSpec((None, MEM_LEN, X_W), lambda i, j: (i, 0, 0))
    consts = [p["norm_mix_g"], p["w_in"], p["b_gate"], p["gn_g"], p["sg_ln_g"], p["sg_ws"], p["sg_bias"],
              dec["dmat"], dec["qdec"], dec["kdec"], p["w_br_ret"], p["w_br_sg"], p["w_br_x"], p["w_o"]]
    return pl.pallas_call(
        _mixer_kernel,
        grid=(b, l // t),
        in_specs=[_smem_spec(), tok(D_MODEL), tab, mem, mem]
        + [_const_spec(c.shape) for c in consts] + cspec,
        out_specs=[tok(D_MODEL), pl.BlockSpec((None, HEADS, DK, DV), lambda i, j: (i, 0, 0, 0))] + cspec,
        out_shape=[jax.ShapeDtypeStruct((b, l, D_MODEL), F32),
                   jax.ShapeDtypeStruct((b, HEADS, DK, DV), F32)] + cshape,
        scratch_shapes=[pltpu.VMEM((t, V_W), BF16), pltpu.VMEM((t, SG_W), BF16), pltpu.VMEM((t, X_W), BF16)],
        compiler_params=_params(("arbitrary", "arbitrary")),
        name="prompt_mixer",
    )(dec["gc"], x, np.concatenate(tabs, axis=1), mk_b, mv_b, *consts, *cast_weights)


def _row_head(n):
    return (lax.broadcasted_iota(jnp.int32, (n, 1), 0) >> 2) & (HEADS - 1)


def _state_matmuls(gc_ref, qe_ref, kre_ref, kde_ref, ve_ref, xqe_ref, s_ref, mk_ref, dmat_ref, so_ref):
    rows_n = qe_ref.shape[0]
    per = HEADS * EXP
    batches = range(rows_n // per)
    rows = [slice(b * per, (b + 1) * per) for b in batches]
    q, ve, xq = qe_ref[...], ve_ref[...], xqe_ref[...]
    sc = _mm_nt(q, kre_ref[...]) * dmat_ref[...]
    kd = kde_ref[...]
    rowb = lax.broadcasted_iota(jnp.int32, (rows_n, 1), 0) >> 4
    cross = jnp.concatenate(
        [jnp.dot(q[rows[b]], s_ref[b].astype(BF16), preferred_element_type=F32) for b in batches], axis=0)
    upd = [_mm_tn(kd, jnp.where(rowb == b, ve, jnp.zeros_like(ve))) for b in batches]
    for b in batches:
        for hd in range(HEADS):
            hr = slice(hd * DK, (hd + 1) * DK)
            so_ref[b, hr, :] = gc_ref[hd] * s_ref[b, hr, :] + upd[b][hr]
    sx = jnp.concatenate([_mm_nt(xq[rows[b]], mk_ref[b]) for b in batches], axis=0) * np.float32(DK ** -0.5)
    return sc, cross, sx


def _state_outputs(sc, cross, sx, ve_ref, ge_ref, mv_ref, qdec_ref, gn_ref, oret_ref, ox_ref):
    rows_n = ve_ref.shape[0]
    per = HEADS * EXP
    batches = range(rows_n // per)
    rows = [slice(b * per, (b + 1) * per) for b in batches]
    inner = jnp.dot(sc.astype(BF16), ve_ref[...], preferred_element_type=F32)
    hh = _row_head(rows_n)
    own_col = (lax.broadcasted_iota(jnp.int32, (rows_n, MEM_LEN * HEADS), 1) & (HEADS - 1)) == hh
    p = _softmax(jnp.where(own_col, sx, np.float32(-1e30))).astype(BF16)
    oxs = jnp.concatenate([_mm(p[rows[b]], mv_ref[b]) for b in batches], axis=0)
    o = inner + cross * qdec_ref[...]
    og = ge_ref[...] * (_stdnorm(o) * gn_ref[...])
    for hd in range(HEADS):
        oret_ref[:, hd * DV:(hd + 1) * DV] = jnp.where(hh == hd, og, 0.0).astype(BF16)
        ox_ref[:, hd * DK:(hd + 1) * DK] = jnp.where(hh == hd, oxs, 0.0).astype(BF16)


def _ffn_kernel(gc_ref, x_ref, ng_ref, wup_ref, cw_ref, cb_ref, wdn_ref, nf_ref,
                qe_ref, kre_ref, kde_ref, ve_ref, ge_ref, xqe_ref, s_ref, mk_ref, mv_ref, dmat_ref, qdec_ref, gn_ref,
                y_ref, tail_ref, oret_ref, ox_ref, so_ref, zs_ref):
    tile = x_ref.shape[0]
    first = pl.program_id(1) == 0

    @pl.when(first)
    def _():
        zs_ref[0:8, :] = jnp.zeros((8, 2 * D_FF), F32)

    @pl.when(jnp.logical_not(first))
    def _():
        zs_ref[0:8, :] = zs_ref[tile:tile + 8, :]

    sc, cross, sx = _state_matmuls(gc_ref, qe_ref, kre_ref, kde_ref, ve_ref, xqe_ref, s_ref, mk_ref, dmat_ref,
                                   so_ref)
    x = x_ref[...]
    zs_ref[8:8 + tile, :] = _mm(_rms(x, ng_ref[...]), wup_ref[...])
    _state_outputs(sc, cross, sx, ve_ref, ge_ref, mv_ref, qdec_ref, gn_ref, oret_ref, ox_ref)
    ca, cb = slice(0, D_FF), slice(D_FF, 2 * D_FF)
    gate = _conv_gate(zs_ref, tile, ca, cb, cw_ref[:, ca], cb_ref[:, ca], cw_ref[:, cb], cb_ref[:, cb])
    gate = gate.astype(BF16)
    for rows in (slice(0, tile // 2), slice(tile // 2, tile)):
        y = x[rows] + jnp.dot(gate[rows], wdn_ref[...], preferred_element_type=F32)
        y_ref[rows, :] = _rms(y, nf_ref[...])
    tail_ref[...] = zs_ref[tile + 6:tile + 8, :]


def _prompt_ffn(x, p, pre, state, mk, mv, dec):
    b, l, _ = x.shape
    t = PROMPT_TILE
    nt = l // t
    qe, kre, kde, ve, ge, xqe = pre
    nb = state.shape[0]
    bb = nb // (b * nt)
    rb = bb * HEADS * EXP
    ne = qe.shape[0]
    tok = pl.BlockSpec((None, t, D_MODEL), lambda i, j: (i, j, 0))
    rowblk = lambda w: pl.BlockSpec((rb, w), lambda i, j: (i * nt + j, 0))
    batblk = lambda a, c: pl.BlockSpec((bb, a, c), lambda i, j: (i * nt + j, 0, 0))
    consts = [p["norm_ffn_g"], p["w_up"], p["conv_w"], p["conv_b"], p["w_down"], p["norm_final_g"]]
    return pl.pallas_call(
        _ffn_kernel,
        grid=(b, nt),
        in_specs=[_smem_spec(), tok] + [_const_spec(c.shape) for c in consts]
        + [rowblk(QK_W), rowblk(QK_W), rowblk(QK_W), rowblk(DV), rowblk(DV), rowblk(DK),
           batblk(HEADS * DK, DV), batblk(MEM_LEN * HEADS, DK), batblk(MEM_LEN * HEADS, DK),
           _const_spec((rb, rb)), _const_spec((rb, DV)), _const_spec((rb, DV))],
        out_specs=[tok, pl.BlockSpec((None, 2, 2 * D_FF), lambda i, j: (i, 0, 0)),
                   rowblk(V_W), rowblk(X_W), batblk(HEADS * DK, DV)],
        out_shape=[jax.ShapeDtypeStruct((b, l, D_MODEL), F32), jax.ShapeDtypeStruct((b, 2, 2 * D_FF), F32),
                   jax.ShapeDtypeStruct((ne, V_W), BF16), jax.ShapeDtypeStruct((ne, X_W), BF16),
                   jax.ShapeDtypeStruct(state.shape, F32)],
        scratch_shapes=[pltpu.VMEM((t + 8, 2 * D_FF), F32)],
        compiler_params=_params(("arbitrary", "arbitrary")),
        name="prompt_ffn",
    )(dec["gc"], x, *consts, qe, kre, kde, ve, ge, xqe, state, mk, mv, dec["dmat"], dec["qdec"], dec["gn"])


def _sample_pre_kernel(ws_ref, bs_ref, x_ref, cq_ref, sq_ref, ck_ref, sk_ref, kdec_ref, ng_ref, win_ref, bg_ref,
                       lng_ref, wr_ref, wsg_ref, wx_ref, wo_ref,
                       qe_ref, kre_ref, kde_ref, ve_ref, ge_ref, xqe_ref, osg_ref, vrows_ref, gates_ref,
                       wr_o_ref, wsg_o_ref, wx_o_ref, wo_o_ref, h_ref, vn_ref):
    n = x_ref.shape[0]
    nb = n // EXP
    ne = qe_ref.shape[0]
    i = pl.program_id(0)

    @pl.when(i == 0)
    def _():
        h_ref[...] = _rms(x_ref[...], ng_ref[...]).astype(BF16)
        vn_ref[...] = jnp.zeros_like(vn_ref)

    bq = ne // (HEADS * EXP)
    hq = jnp.concatenate([h_ref[pl.ds(pl.multiple_of(l * nb + i * bq, bq), bq), :] for l in range(EXP)], axis=0)
    r = lax.broadcasted_iota(jnp.int32, (ne, EXP * bq), 0)
    c = lax.broadcasted_iota(jnp.int32, (ne, EXP * bq), 1)
    rep = jnp.where(c == (r & (EXP - 1)) * bq + (r >> 4), 1.0, 0.0).astype(BF16)
    he = jnp.dot(rep, hq, preferred_element_type=F32).astype(BF16)
    hh = _row_head(ne)
    h = h_ref[pl.ds(pl.multiple_of(i * nb, nb), nb), :]

    def proj(hm, a, b):
        return jnp.dot(hm, win_ref[:, a:b], preferred_element_type=F32)

    q, k = proj(he, C_Q, C_K), proj(he, C_K, C_V)
    xq = proj(he, C_XQ, C_GT)
    cq, sq, ck, sk, kdec = cq_ref[...], sq_ref[...], ck_ref[...], sk_ref[...], kdec_ref[...]
    for hd in range(HEADS):
        cols = slice(hd * DK, (hd + 1) * DK)
        own = hh == hd
        kr = _rope(k[:, cols], ck, sk)
        qe_ref[:, cols] = jnp.where(own, _rope(q[:, cols], cq, sq), 0.0).astype(BF16)
        kre_ref[:, cols] = jnp.where(own, kr, 0.0).astype(BF16)
        kde_ref[:, cols] = jnp.where(own, kr * kdec, 0.0).astype(BF16)
    v = proj(he, C_V, C_G)
    g = proj(he, C_G, C_SU)
    ve = jnp.zeros((ne, DV), F32)
    ge = jnp.zeros((ne, DV), F32)
    xqc = jnp.zeros((ne, DK), F32)
    for hd in range(HEADS):
        cols = slice(hd * DV, (hd + 1) * DV)
        own = hh == hd
        ve = jnp.where(own, v[:, cols], ve)
        ge = jnp.where(own, g[:, cols], ge)
        xqc = jnp.where(own, xq[:, hd * DK:(hd + 1) * DK], xqc)
    xqe_ref[...] = xqc.astype(BF16)
    ve_ref[...] = ve.astype(BF16)
    ge_ref[...] = _silu(ge)

    u = _gelu(proj(h, C_SU, C_SV))
    vn = _stdnorm(_gelu(proj(h, C_SV, C_XQ))) * lng_ref[...]
    vrows_ref[...] = vn
    vn_ref[i] = vn
    for g in range(HEADS):
        cols = slice(g * DK, (g + 1) * DK)
        mixed = jnp.full((nb, DK), bs_ref[g * EXP + i], F32)
        for s in range(EXP):
            w = jnp.where(s <= i, ws_ref[(g * EXP + i) * EXP + s], 0.0)
            mixed = mixed + w * vn_ref[s, :, cols]
        osg_ref[:, cols] = (u[:, cols] * mixed).astype(BF16)
    gates_ref[...] = _sigmoid(proj(h, C_GT, C_END) + bg_ref[...])
    _cast_blocks(((wr_ref, wr_o_ref), (wsg_ref, wsg_o_ref), (wx_ref, wx_o_ref), (wo_ref, wo_o_ref)))


def _sample_pre(x, tabs, kdec_rows, p, ws4, bs4, cast_weights):
    n = x.shape[0]
    cspec, cshape = _cast_specs(cast_weights, EXP, lambda i: i)
    t = n // EXP
    te = (t // EXP) * HEADS * EXP
    consts = [p["norm_mix_g"], p["w_in"], p["b_gate"], p["sg_ln_g"]]
    rowblk = lambda rows, w: pl.BlockSpec((rows, w), lambda i: (i, 0))
    outs = [(te, QK_W, BF16), (te, QK_W, BF16), (te, QK_W, BF16), (te, DV, BF16), (te, DV, F32),
            (te, DK, BF16), (t, SG_W, BF16), (t, SG_W, F32), (t, 3 * D_MODEL, F32)]
    return pl.pallas_call(
        _sample_pre_kernel,
        grid=(EXP,),
        in_specs=[_smem_spec(), _smem_spec(), _const_spec(x.shape)] + [_const_spec((te, DK))] * 5
        + [_const_spec(c.shape) for c in consts] + cspec,
        out_specs=[rowblk(r, w) for r, w, _ in outs] + cspec,
        out_shape=[jax.ShapeDtypeStruct((r * EXP, w), d) for r, w, d in outs] + cshape,
        scratch_shapes=[pltpu.VMEM((n, D_MODEL), BF16), pltpu.VMEM((EXP, t, SG_W), F32)],
        compiler_params=_params(("arbitrary",)),
        name="sample_pre",
    )(ws4, bs4, x, *tabs, kdec_rows, *consts, *cast_weights)


def _sample_merge_kernel(x_ref, orete_ref, oxe_ref, osg_ref, gates_ref, wr_ref, wsg_ref, wx_ref, wo_ref, xo_ref):
    ls, nb, _ = x_ref.shape
    n = ls * nb
    ne = n * EXP
    rows = lambda ref: ref[...].reshape(n, ref.shape[-1])
    r = lax.broadcasted_iota(jnp.int32, (n, ne), 0)
    c = lax.broadcasted_iota(jnp.int32, (n, ne), 1)
    col = jnp.where(r == (c & (EXP - 1)) * (n // EXP) + (c >> 4), 1.0, 0.0).astype(BF16)
    oret = jnp.dot(col, orete_ref[...], preferred_element_type=F32)
    ox = jnp.dot(col, oxe_ref[...], preferred_element_type=F32)
    gates = rows(gates_ref)
    merged = (gates[:, :D_MODEL] * _mm(oret, wr_ref[...])
              + gates[:, D_MODEL:2 * D_MODEL] * jnp.dot(rows(osg_ref), wsg_ref[...], preferred_element_type=F32)
              + gates[:, 2 * D_MODEL:] * _mm(ox, wx_ref[...]))
    xo_ref[...] = (rows(x_ref) + _mm(merged, wo_ref[...])).reshape(xo_ref.shape)


def _sample_merge(x, orete, oxe, osg, gates, p):
    n = x.shape[0]
    ls = EXP
    bs = n // ls
    steps = 4
    nb = bs // steps
    tok = lambda a: a.reshape(ls, bs, a.shape[-1])
    tokspec = lambda w: pl.BlockSpec((ls, nb, w), lambda i: (0, i, 0))
    expspec = lambda w: pl.BlockSpec((nb * HEADS * ls, w), lambda i: (i, 0))
    weights = [p["w_br_ret"], p["w_br_sg"], p["w_br_x"], p["w_o"]]
    out = pl.pallas_call(
        _sample_merge_kernel,
        grid=(steps,),
        in_specs=[tokspec(D_MODEL), expspec(V_W), expspec(X_W), tokspec(SG_W), tokspec(3 * D_MODEL)]
        + [_const_spec(w.shape) for w in weights],
        out_specs=tokspec(D_MODEL),
        out_shape=jax.ShapeDtypeStruct((ls, bs, D_MODEL), F32),
        compiler_params=_params(("arbitrary",)),
        name="sample_merge",
    )(tok(x), orete, oxe, tok(osg), tok(gates), *weights)
    return out.reshape(n, D_MODEL)


def _sample_ffn_kernel(x_ref, ng_ref, wa_ref, wb_ref, cwa_ref, cwb_ref, cba_ref, cbb_ref, s0a_ref, s0b_ref,
                       s1a_ref, s1b_ref, wdn_ref, nf_ref,
                       y_ref, c2a_ref, c2b_ref, c3a_ref, c3b_ref, h_ref, acc_ref):
    nb = x_ref.shape[0] // EXP
    j = pl.program_id(0)

    @pl.when(j == 0)
    def _():
        h_ref[...] = _rms(x_ref[...], ng_ref[...]).astype(BF16)
        acc_ref[...] = jnp.zeros_like(acc_ref)

    h = h_ref[...]

    def conv(w_ref, cw_ref, cb_ref, s0_ref, s1_ref, c2_ref, c3_ref):
        z = jnp.dot(h, w_ref[...], preferred_element_type=F32)
        zp = [s0_ref[...], s1_ref[...]] + [z[l * nb:(l + 1) * nb] for l in range(EXP)]
        c2_ref[...] = zp[EXP]
        c3_ref[...] = zp[EXP + 1]
        cw, cb = cw_ref[...], cb_ref[...]
        return jnp.concatenate([cb + cw[0:1] * zp[l] + cw[1:2] * zp[l + 1] + cw[2:3] * zp[l + 2]
                                for l in range(EXP)], axis=0)

    a = conv(wa_ref, cwa_ref, cba_ref, s0a_ref, s1a_ref, c2a_ref, c3a_ref)
    b = conv(wb_ref, cwb_ref, cbb_ref, s0b_ref, s1b_ref, c2b_ref, c3b_ref)
    acc_ref[...] += _mm(_gelu(a) * b, wdn_ref[...])

    @pl.when(j == pl.num_programs(0) - 1)
    def _():
        y_ref[...] = _rms(x_ref[...] + acc_ref[...], nf_ref[...])


def _sample_ffn(x, sc, p):
    n = x.shape[0]
    nb = n // EXP
    cw = FFN_CW
    nch = D_FF // cw
    ca = lambda rows: pl.BlockSpec((rows, cw), lambda j: (0, j))
    cb = lambda rows: pl.BlockSpec((rows, cw), lambda j: (0, nch + j))
    sa = lambda k: pl.BlockSpec((None, nb, cw), lambda j: (k, 0, j))
    sb = lambda k: pl.BlockSpec((None, nb, cw), lambda j: (k, 0, nch + j))
    full = pl.BlockSpec((n, D_MODEL), lambda j: (0, 0))
    vec = pl.BlockSpec((1, D_MODEL), lambda j: (0, 0))
    return pl.pallas_call(
        _sample_ffn_kernel,
        grid=(nch,),
        in_specs=[full, vec, ca(D_MODEL), cb(D_MODEL), ca(3), cb(3), ca(1), cb(1), sa(0), sb(0), sa(1), sb(1),
                  pl.BlockSpec((cw, D_MODEL), lambda j: (j, 0)), vec],
        out_specs=[full] + [ca(nb)] * 4,
        out_shape=[jax.ShapeDtypeStruct((n, D_MODEL), F32)] + [jax.ShapeDtypeStruct((nb, D_FF), F32)] * 4,
        scratch_shapes=[pltpu.VMEM((n, D_MODEL), BF16), pltpu.VMEM((n, D_MODEL), F32)],
        compiler_params=_params(("arbitrary",)),
        name="sample_ffn",
    )(x, p["norm_ffn_g"], p["w_up"], p["w_up"], p["conv_w"], p["conv_w"], p["conv_b"], p["conv_b"],
      sc, sc, sc, sc, p["w_down"], p["norm_final_g"])


def _rope_tables(pos, scale):
    inv = ROPE_BASE ** (-np.arange(0, DK, 2, dtype=np.float64) / DK)
    ang = np.asarray(pos, np.float64)[:, None] * inv[None, :]
    cos, sin = np.cos(ang), np.sin(ang)
    return (np.concatenate([cos, cos], -1) * scale).astype(np.float32), \
        (np.concatenate([-sin, sin], -1) * scale).astype(np.float32)


def _decay(chunk):
    lg = np.log1p(-np.exp2(-5.0 - np.arange(HEADS, dtype=np.float64)))
    n = np.arange(chunk, dtype=np.float64)
    diff = n[:, None] - n[None, :]
    dmat = np.where(diff >= 0, np.exp(np.maximum(diff, 0.0)[None] * lg[:, None, None]), 0.0)
    qdec = np.exp((n + 1.0)[None, :] * lg[:, None])
    kdec = np.exp((chunk - 1.0 - n)[None, :] * lg[:, None])
    f32 = lambda a: a.astype(np.float32)
    return f32(dmat), f32(qdec), f32(kdec), f32(np.exp(chunk * lg))


def kernel(x_prompt, x_sample, mem_prompt, state_ret, state_conv, cache_mem_k, cache_mem_v, norm_mix_g, w_in,
           b_gate, ret_gn_g, sg_ln_g, sg_ws, sg_bs, mem_norm_g, w_mem_kv, w_br_ret, w_br_sg, w_br_x, w_o,
           norm_ffn_g, w_up, conv_w, conv_b, w_down, norm_final_g):
    bp, lp, _ = x_prompt.shape
    bs, ls, _ = x_sample.shape
    assert state_ret.shape[0] == 1 and ls == EXP and lp % PROMPT_TILE == 0
    assert bs % (bp * (lp // PROMPT_TILE)) == 0
    assert bs == CHUNK
    row = lambda a: a.reshape(1, -1)
    scale = DK ** -0.5

    mk, mv, mk_b, mv_b, w_in_b = _memkv(mem_prompt, row(mem_norm_g[0]), w_mem_kv[0], [w_in[0]])
    p = dict(norm_mix_g=row(norm_mix_g[0]), w_in=w_in_b, b_gate=row(b_gate[0]),
             gn_g=row(ret_gn_g[0]), sg_ln_g=row(sg_ln_g[0]),
             norm_ffn_g=row(norm_ffn_g[0]), conv_w=conv_w[0], conv_b=row(conv_b[0]),
             norm_final_g=row(norm_final_g))

    n = bs * ls
    per = HEADS * EXP
    te = (bs // EXP) * per
    pos_s = PAST_LEN + (np.arange(te) & (ls - 1))
    tabs_s = (*_rope_tables(pos_s, 1.0), *_rope_tables(pos_s, scale))
    dmat4, qdec4, kdec4, gc4 = _decay(ls)
    kdec_rows = np.ascontiguousarray(np.broadcast_to(np.tile(kdec4.reshape(per), te // per)[:, None], (te, DK)))
    xs = jnp.swapaxes(x_sample, 0, 1).reshape(n, D_MODEL)
    pre = _sample_pre(xs, tabs_s, kdec_rows, p, sg_ws[0][:, :ls, :ls].reshape(-1), sg_bs[0][:, :ls].reshape(-1),
                      [w_br_ret[0], w_br_sg[0], w_br_x[0], w_o[0]])
    qe, kre, kde, ve, ge, xqe, osg, vrows, gates, w_br_ret_b, w_br_sg_b, w_br_x_b, w_o_b = pre
    p = dict(p, w_br_ret=w_br_ret_b, w_br_sg=w_br_sg_b, w_br_x=w_br_x_b, w_o=w_o_b)

    pos_p = np.arange(lp)
    tabs_p = (*_rope_tables(pos_p, 1.0), *_rope_tables(pos_p, scale))
    dmat, qdec, kdec, gc = _decay(CHUNK)
    bcast = lambda a: np.ascontiguousarray(np.broadcast_to(a[:, :, None], (HEADS, CHUNK, DK)))
    dec_p = dict(dmat=dmat, qdec=bcast(qdec), kdec=bcast(kdec), gc=gc)
    pp = dict(p, sg_ws=sg_ws[0], sg_bias=jnp.repeat(sg_bs[0].T, DK, axis=1))
    x_mid, s_prompt, w_up_b, w_down_b = _prompt_mixer(x_prompt, tabs_p, mk_b, mv_b, dec_p, pp,
                                                      [w_up[0], w_down[0]])
    p = dict(p, w_up=w_up_b, w_down=w_down_b)

    sbb = bs // (bp * (lp // PROMPT_TILE))
    rb = sbb * per
    blk16 = np.einsum("hk,hls->hlks", np.eye(HEADS, dtype=np.float32), dmat4).reshape(per, per)
    dec_s = dict(gc=gc4,
                 dmat=np.kron(np.eye(sbb, dtype=np.float32), blk16),
                 qdec=np.ascontiguousarray(np.broadcast_to(np.tile(qdec4.reshape(per), sbb)[:, None], (rb, DV))),
                 gn=jnp.tile(jnp.repeat(ret_gn_g[0], ls, axis=0), (sbb, 1)))
    y_prompt, tail, orete, oxe, s_sample = _prompt_ffn(
        x_mid, p, (qe, kre, kde, ve, ge, xqe), state_ret[0].reshape(bs, HEADS * DK, DV),
        cache_mem_k[0].reshape(bs, MEM_LEN * HEADS, DK), cache_mem_v[0].reshape(bs, MEM_LEN * HEADS, DK), dec_s)
    xs_mid = _sample_merge(xs, orete, oxe, osg, gates, p)
    y_sample, c2a, c2b, c3a, c3b = _sample_ffn(xs_mid, jnp.swapaxes(state_conv[0], 0, 1), p)
    conv_s = jnp.stack([jnp.concatenate([c2a, c2b], -1), jnp.concatenate([c3a, c3b], -1)], axis=1)
    unpos = lambda a: jnp.swapaxes(a.reshape(ls, bs, a.shape[-1]), 0, 1)

    return (y_prompt, unpos(y_sample),
            s_prompt[None], tail[None],
            mk.reshape(1, bp, MEM_LEN, HEADS, DK), mv.reshape(1, bp, MEM_LEN, HEADS, DK),
            s_sample.reshape(1, bs, HEADS, DK, DV), conv_s[None],
            unpos(vrows)[None])
```
